```python
import jax, jax.numpy as jnp
from jax import lax
import numpy as np

D_MODEL = 1024
BATCH = 4
SEQ = 8192
DEPTH = 4

CHUNK = 64
Q_BLOCK = 128
SB_HEADS = 4
SB_HEAD_DIM = 64
FOX_HEADS = 4
FOX_HEAD_DIM = 64
HGRN_HEADS = 4
HGRN_KEY_DIM = 128
HGRN_VAL_DIM = 128
N_BRANCHES = 3
D_FF = 3072
N_EXPERTS = 8
TOP_K = 2
D_FF_EXPERT = 3584
EXPERT_BLOCK = 256
RMS_EPS = 1e-6

SB_WIDTH = SB_HEADS * SB_HEAD_DIM
FOX_WIDTH = FOX_HEADS * FOX_HEAD_DIM
HGRN_KEY_WIDTH = HGRN_HEADS * HGRN_KEY_DIM
HGRN_VAL_WIDTH = HGRN_HEADS * HGRN_VAL_DIM
IN_WIDTHS = (SB_WIDTH, SB_WIDTH, SB_WIDTH,
             FOX_WIDTH, FOX_WIDTH, FOX_WIDTH, FOX_HEADS,
             HGRN_KEY_WIDTH, HGRN_KEY_WIDTH, HGRN_VAL_WIDTH, HGRN_VAL_WIDTH,
             N_BRANCHES * D_MODEL)
D_IN = sum(IN_WIDTHS)
N_DENSE = (DEPTH + 1) // 2
N_MOE = DEPTH // 2

kernel_name = 'hybrid_sb_fox_hgrn2_moe_trunk'

F32 = jnp.float32


def rmsnorm(x, g):
    xf = x.astype(F32)
    xf = xf * lax.rsqrt(jnp.mean(xf * xf, axis=-1, keepdims=True) + RMS_EPS)
    return (xf * g.astype(F32)).astype(x.dtype)


def split_heads(t, n_heads):
    b, s, w = t.shape
    return t.reshape(b, s, n_heads, w // n_heads).transpose(0, 2, 1, 3)


def merge_heads(t):
    b, h, s, d = t.shape
    return t.transpose(0, 2, 1, 3).reshape(b, s, h * d)


def stick_breaking_attention(q, k, v):
    seq = q.shape[2]
    scale = SB_HEAD_DIM ** -0.5
    outs = []
    for blk in range(seq // Q_BLOCK):
        t0, t1 = blk * Q_BLOCK, (blk + 1) * Q_BLOCK
        z = jnp.einsum('bhtd,bhsd->bhts', q[:, :, t0:t1], k[:, :, :t1]).astype(F32) * scale
        mask = jnp.arange(t1)[None, :] < jnp.arange(t0, t1)[:, None]
        log_1m = jnp.where(mask, jax.nn.log_sigmoid(-z), 0.0)
        suffix = lax.cumsum(log_1m, axis=3, reverse=True) - log_1m
        w = jnp.where(mask, jnp.exp(jax.nn.log_sigmoid(z) + suffix), 0.0)
        outs.append(jnp.einsum('bhts,bhsd->bhtd', w.astype(v.dtype), v[:, :, :t1]))
    return jnp.concatenate(outs, axis=2)


def forgetting_attention(q, k, v, log_f):
    seq = q.shape[2]
    scale = FOX_HEAD_DIM ** -0.5
    cum = jnp.cumsum(log_f, axis=-1)
    outs = []
    for blk in range(seq // Q_BLOCK):
        t0, t1 = blk * Q_BLOCK, (blk + 1) * Q_BLOCK
        logits = jnp.einsum('bhtd,bhsd->bhts', q[:, :, t0:t1], k[:, :, :t1]).astype(F32) * scale
        logits = logits + cum[:, :, t0:t1, None] - cum[:, :, None, :t1]
        mask = jnp.arange(t1)[None, :] <= jnp.arange(t0, t1)[:, None]
        p = jax.nn.softmax(jnp.where(mask, logits, -jnp.inf), axis=-1)
        outs.append(jnp.einsum('bhts,bhsd->bhtd', p.astype(v.dtype), v[:, :, :t1]))
    return jnp.concatenate(outs, axis=2)


def hgrn2_chunkwise(q, k, v, log_f):
    b_, h_, seq, dk = q.shape
    dv = v.shape[-1]
    nc = seq // CHUNK

    def to_chunks(t):
        return jnp.moveaxis(t.reshape(b_, h_, nc, CHUNK, t.shape[-1]), 2, 0)

    causal = jnp.tril(jnp.ones((CHUNK, CHUNK), dtype=bool))

    def step(state, inp):
        qc, kc, vc, lfc = inp
        b = jnp.cumsum(lfc, axis=2)
        rel = jnp.where(causal[:, :, None], b[:, :, :, None, :] - b[:, :, None, :, :], -jnp.inf)
        scores = jnp.einsum('bhtk,bhsk,bhtsk->bhts', qc, kc, jnp.exp(rel))
        o = (jnp.einsum('bhts,bhsv->bhtv', scores, vc)
             + jnp.einsum('bhtk,bhkv->bhtv', qc * jnp.exp(b), state))
        b_last = b[:, :, -1:, :]
        new_state = (jnp.exp(b_last[:, :, 0, :, None]) * state
                     + jnp.einsum('bhsk,bhsv->bhkv', kc * jnp.exp(b_last - b), vc))
        return new_state, o

    s0 = jnp.zeros((b_, h_, dk, dv), F32)
    _, o = lax.scan(step, s0, (to_chunks(q), to_chunks(k), to_chunks(v), to_chunks(log_f)))
    return jnp.moveaxis(o, 0, 2).reshape(b_, h_, seq, dv)


def hgrn2_branch(q_in, f_in, i_in, g_in, lb, norm_g):
    dt = q_in.dtype
    lb = lb.astype(F32)
    z = f_in.astype(F32)
    log_f = jnp.logaddexp(jnp.log(lb), jnp.log1p(-lb) + jax.nn.log_sigmoid(z))
    k = (1.0 - lb) * jax.nn.sigmoid(-z)
    q = jax.nn.silu(q_in.astype(F32))
    o = hgrn2_chunkwise(split_heads(q, HGRN_HEADS), split_heads(k, HGRN_HEADS),
                        split_heads(i_in.astype(F32), HGRN_HEADS), split_heads(log_f, HGRN_HEADS))
    o = rmsnorm(o, norm_g) * jax.nn.silu(split_heads(g_in.astype(F32), HGRN_HEADS))
    return merge_heads(o).astype(dt)


def hybrid_mixer(hn, w_in, b_fox, b_gate, lb, hgrn_norm_g, w_br_sb, w_br_fox, w_br_hgrn, w_out):
    b_, s_, d_ = hn.shape
    proj = hn @ w_in
    splits = [int(c) for c in np.cumsum(IN_WIDTHS)[:-1]]
    (qa, ka, va, qb, kb, vb, fb, qc, fc, ic, gc, gate_in) = jnp.split(proj, splits, axis=-1)
    oa = stick_breaking_attention(split_heads(qa, SB_HEADS), split_heads(ka, SB_HEADS),
                                  split_heads(va, SB_HEADS))
    log_fb = jax.nn.log_sigmoid((fb + b_fox).astype(F32)).transpose(0, 2, 1)
    ob = forgetting_attention(split_heads(qb, FOX_HEADS), split_heads(kb, FOX_HEADS),
                              split_heads(vb, FOX_HEADS), log_fb)
    oc = hgrn2_branch(qc, fc, ic, gc, lb, hgrn_norm_g)
    ya = merge_heads(oa) @ w_br_sb
    yb = merge_heads(ob) @ w_br_fox
    yc = oc @ w_br_hgrn
    gates = jax.nn.sigmoid((gate_in.reshape(b_, s_, N_BRANCHES, d_) + b_gate).astype(F32)).astype(hn.dtype)
    merged = gates[:, :, 0] * ya + gates[:, :, 1] * yb + gates[:, :, 2] * yc
    return merged @ w_out


def swiglu(h, w_gate, w_up, w_down):
    return (jax.nn.silu(h @ w_gate) * (h @ w_up)) @ w_down


def moe_swiglu(h, w_router, w_gate, w_up, w_down):
    b_, s_, d_ = h.shape
    xt = h.reshape(-1, d_)
    logits = (xt @ w_router).astype(F32)
    top_logit, top_idx = lax.top_k(logits, TOP_K)
    top_w = jax.nn.softmax(top_logit, axis=-1)
    flat_e = top_idx.reshape(-1)
    nk = flat_e.shape[0]
    order = jnp.argsort(flat_e)
    sorted_e = flat_e[order]
    sorted_tok = order // TOP_K
    sorted_w = top_w.reshape(-1)[order]
    counts = jnp.bincount(flat_e, length=N_EXPERTS)
    padded = ((counts + EXPERT_BLOCK - 1) // EXPERT_BLOCK) * EXPERT_BLOCK
    start = jnp.cumsum(counts) - counts
    pend = jnp.cumsum(padded)
    pstart = pend - padded
    dest = pstart[sorted_e] + jnp.arange(nk) - start[sorted_e]
    n_rows = nk + N_EXPERTS * EXPERT_BLOCK
    n_blocks = n_rows // EXPERT_BLOCK
    buf = jnp.zeros((n_rows, d_), h.dtype).at[dest].set(xt[sorted_tok])
    block_start = jnp.arange(n_blocks) * EXPERT_BLOCK
    block_e = jnp.minimum(jnp.sum(block_start[:, None] >= pend[None, :], axis=1), N_EXPERTS - 1)

    def expert_block(args):
        xb, e = args
        return swiglu(xb, w_gate[e], w_up[e], w_down[e])

    yb = lax.map(expert_block, (buf.reshape(n_blocks, EXPERT_BLOCK, d_), block_e))
    y_sorted = yb.reshape(n_rows, d_)[dest] * sorted_w[:, None].astype(h.dtype)
    y = jnp.zeros_like(xt).at[sorted_tok].add(y_sorted)
    return y.reshape(b_, s_, d_)


def setup_inputs(seed: int = 0) -> dict:
    key = jax.random.key(seed)
    ks = jax.random.split(key, 24)

    def normal(k, shape, scale):
        return jax.random.normal(k, shape, jnp.float32) * scale

    return {
        'x': normal(ks[0], (BATCH, SEQ, D_MODEL), 1.0),
        'norm_mix_g': 1.0 + normal(ks[1], (DEPTH, D_MODEL), 0.02),
        'w_in': normal(ks[2], (DEPTH, D_MODEL, D_IN), D_MODEL ** -0.5),
        'b_fox': 3.0 + normal(ks[3], (DEPTH, FOX_HEADS), 0.1),
        'b_gate': normal(ks[4], (DEPTH, N_BRANCHES, D_MODEL), 0.02),
        'hgrn_lb_logits': normal(ks[5], (DEPTH, HGRN_KEY_WIDTH), 0.1),
        'hgrn_norm_g': 1.0 + normal(ks[6], (DEPTH, HGRN_VAL_DIM), 0.02),
        'w_branch_sb': normal(ks[7], (DEPTH, SB_WIDTH, D_MODEL), SB_WIDTH ** -0.5),
        'w_branch_fox': normal(ks[8], (DEPTH, FOX_WIDTH, D_MODEL), FOX_WIDTH ** -0.5),
        'w_branch_hgrn': normal(ks[9], (DEPTH, HGRN_VAL_WIDTH, D_MODEL), HGRN_VAL_WIDTH ** -0.5),
        'w_out': normal(ks[10], (DEPTH, D_MODEL, D_MODEL), D_MODEL ** -0.5),
        'norm_ffn_g': 1.0 + normal(ks[11], (DEPTH, D_MODEL), 0.02),
        'w_ffn_gate': normal(ks[12], (N_DENSE, D_MODEL, D_FF), D_MODEL ** -0.5),
        'w_ffn_up': normal(ks[13], (N_DENSE, D_MODEL, D_FF), D_MODEL ** -0.5),
        'w_ffn_down': normal(ks[14], (N_DENSE, D_FF, D_MODEL), D_FF ** -0.5),
        'w_router': normal(ks[15], (N_MOE, D_MODEL, N_EXPERTS), D_MODEL ** -0.5),
        'w_exp_gate': normal(ks[16], (N_MOE, N_EXPERTS, D_MODEL, D_FF_EXPERT), D_MODEL ** -0.5),
        'w_exp_up': normal(ks[17], (N_MOE, N_EXPERTS, D_MODEL, D_FF_EXPERT), D_MODEL ** -0.5),
        'w_exp_down': normal(ks[18], (N_MOE, N_EXPERTS, D_FF_EXPERT, D_MODEL), D_FF_EXPERT ** -0.5),
        'final_norm_g': 1.0 + normal(ks[19], (D_MODEL,), 0.02),
    }


def reference(x, norm_mix_g, w_in, b_fox, b_gate, hgrn_lb_logits, hgrn_norm_g,
              w_branch_sb, w_branch_fox, w_branch_hgrn, w_out, norm_ffn_g,
              w_ffn_gate, w_ffn_up, w_ffn_down, w_router, w_exp_gate, w_exp_up,
              w_exp_down, final_norm_g):
    lb_cum = jnp.cumsum(jax.nn.softmax(hgrn_lb_logits.astype(F32), axis=0), axis=0)
    lower_bounds = lb_cum - lb_cum[0:1]
    h = x
    for l in range(DEPTH):
        hn = rmsnorm(h, norm_mix_g[l])
        h = h + hybrid_mixer(hn, w_in[l], b_fox[l], b_gate[l], lower_bounds[l], hgrn_norm_g[l],
                             w_branch_sb[l], w_branch_fox[l], w_branch_hgrn[l], w_out[l])
        hn = rmsnorm(h, norm_ffn_g[l])
        if l % 2 == 0:
            i = l // 2
            h = h + swiglu(hn, w_ffn_gate[i], w_ffn_up[i], w_ffn_down[i])
        else:
            i = l // 2
            h = h + moe_swiglu(hn, w_router[i], w_exp_gate[i], w_exp_up[i], w_exp_down[i])
    return rmsnorm(h, final_norm_g)
```

```python
import functools

import jax
import jax.numpy as jnp
from jax import lax
from jax.experimental import pallas as pl
from jax.experimental.pallas import tpu as pltpu

F32 = jnp.float32
BF16 = jnp.bfloat16
I32 = jnp.int32

RMS_EPS = 1e-6
CHUNK = 64
SB_HEADS = 4
FOX_HEADS = 4
HGRN_HEADS = 4
HEAD_DIM = 64
HGRN_DIM = 128
N_EXPERTS = 8
TOP_K = 2

LANES = 128
SUBLANES = 8
VMEM_LIMIT = 56 * 1024 * 1024

EXP_ZERO_BELOW = -105.0
NEG_BIG = -1e30


def _cparams(*sem):
    return pltpu.CompilerParams(dimension_semantics=sem, vmem_limit_bytes=VMEM_LIMIT)


def _dot(a, b):
    return jnp.dot(a, b, preferred_element_type=F32)


def _dot_nt(a, b):
    return lax.dot_general(a, b, (((1,), (1,)), ((), ())), preferred_element_type=F32)


def _dot_tn(a, b):
    return lax.dot_general(a, b, (((0,), (0,)), ((), ())), preferred_element_type=F32)


def _split2(x):
    hi = x.astype(BF16)
    lo = (x - hi.astype(F32)).astype(BF16)
    return hi, lo


def _split3(x):
    hi = x.astype(BF16)
    r = x - hi.astype(F32)
    mid = r.astype(BF16)
    lo = (r - mid.astype(F32)).astype(BF16)
    return hi, mid, lo


def _log_sigmoid(x):
    return jnp.minimum(x, 0.0) - jnp.log(1.0 + jnp.exp(-jnp.abs(x)))


def _rmsnorm(x, g):
    ms = jnp.mean(x * x, axis=-1, keepdims=True)
    return x * lax.rsqrt(ms + RMS_EPS) * g


def _tri(n, fn, dtype=BF16):
    r = lax.broadcasted_iota(I32, (n, n), 0)
    c = lax.broadcasted_iota(I32, (n, n), 1)
    return fn(r, c).astype(dtype)


def _lb_kernel(x_ref, o_ref):
    x = x_ref[...]
    depth = x.shape[0]
    m = jnp.max(x, axis=0, keepdims=True)
    e = jnp.exp(x - m)
    sm = e / jnp.sum(e, axis=0, keepdims=True)
    run = sm[0:1]
    first = run
    o_ref[0:1, :] = run - first
    for l in range(1, depth):
        run = run + sm[l:l + 1]
        o_ref[l:l + 1, :] = run - first


def _lower_bounds(logits):
    return pl.pallas_call(
        _lb_kernel,
        out_shape=jax.ShapeDtypeStruct(logits.shape, F32),
        name="hgrn_lower_bounds",
    )(logits.astype(F32))


def _norm_matmul_kernel(x_ref, g_ref, w_ref, o_ref, xn_ref):
    @pl.when(pl.program_id(1) == 0)
    def _():
        xn_ref[...] = _rmsnorm(x_ref[...], g_ref[...]).astype(BF16)

    o_ref[...] = _dot(xn_ref[...], w_ref[...]).astype(o_ref.dtype)


def _norm_matmul(x, g, w, tm, tn):
    n, d = x.shape
    nout = w.shape[1]
    return pl.pallas_call(
        _norm_matmul_kernel,
        grid=(n // tm, nout // tn),
        in_specs=[
            pl.BlockSpec((tm, d), lambda i, j: (i, 0)),
            pl.BlockSpec((1, d), lambda i, j: (0, 0)),
            pl.BlockSpec((d, tn), lambda i, j: (0, j)),
        ],
        out_specs=pl.BlockSpec((tm, tn), lambda i, j: (i, j)),
        out_shape=jax.ShapeDtypeStruct((n, nout), BF16),
        scratch_shapes=[pltpu.VMEM((tm, d), BF16)],
        compiler_params=_cparams("parallel", "arbitrary"),
        name="norm_in_proj",
    )(x, g, w)


def _gate_cum_kernel(x_ref, g_ref, wf_ref, wc_ref, wr_ref, bc_ref, br_ref, lt_ref, ut_ref,
                     f_ref, cc_ref, cr_ref, carc_ref, carr_ref):
    tm = x_ref.shape[1]

    @pl.when(pl.program_id(1) == 0)
    def _():
        carc_ref[...] = jnp.zeros_like(carc_ref)
        carr_ref[...] = jnp.zeros_like(carr_ref)

    xn = _rmsnorm(x_ref[0], g_ref[...]).astype(BF16)
    f_ref[0] = _dot(xn, wf_ref[...])

    lc = _log_sigmoid(_dot(xn, wc_ref[...]) + bc_ref[...])
    lt = lt_ref[...]
    c_hi, c_mid, c_lo = _split3(lc)
    cum_c = _dot(lt, c_hi) + _dot(lt, c_mid) + _dot(lt, c_lo) + carc_ref[0:1, :]
    cc_ref[0] = cum_c
    carc_ref[...] = jnp.broadcast_to(cum_c[tm - 1:tm, :], carc_ref.shape)

    lr = _log_sigmoid(_dot_nt(wr_ref[...], xn) + br_ref[:, 0:1])
    ut = ut_ref[...]
    r_hi, r_mid, r_lo = _split3(lr)
    cum_r = _dot(r_hi, ut) + _dot(r_mid, ut) + _dot(r_lo, ut) + carr_ref[:, 0:1]
    cr_ref[0] = cum_r
    carr_ref[...] = jnp.broadcast_to(cum_r[:, tm - 1:tm], carr_ref.shape)


def _gate_cum(h, g, wf, wc, wr, bc, br, tm):
    b, s, d = h.shape
    nf = wf.shape[1]
    lt = _tri(tm, lambda r, c: c <= r)
    ut = _tri(tm, lambda r, c: r <= c)
    full = lambda shape: pl.BlockSpec(shape, lambda bi, si: (0,) * len(shape))
    return pl.pallas_call(
        _gate_cum_kernel,
        grid=(b, s // tm),
        in_specs=[
            pl.BlockSpec((1, tm, d), lambda bi, si: (bi, si, 0)),
            full((1, d)), full((d, nf)), full((d, LANES)), full((SUBLANES, d)),
            full((1, LANES)), full((SUBLANES, LANES)), full((tm, tm)), full((tm, tm)),
        ],
        out_specs=[
            pl.BlockSpec((1, tm, nf), lambda bi, si: (bi, si, 0)),
            pl.BlockSpec((1, tm, LANES), lambda bi, si: (bi, si, 0)),
            pl.BlockSpec((1, SUBLANES, tm), lambda bi, si: (bi, 0, si)),
        ],
        out_shape=[
            jax.ShapeDtypeStruct((b, s, nf), F32),
            jax.ShapeDtypeStruct((b, s, LANES), F32),
            jax.ShapeDtypeStruct((b, SUBLANES, s), F32),
        ],
        scratch_shapes=[pltpu.VMEM((SUBLANES, LANES), F32), pltpu.VMEM((SUBLANES, LANES), F32)],
        compiler_params=_cparams("parallel", "arbitrary"),
        name="gate_proj_cum",
    )(h, g, wf, wc, wr, bc, br, lt, ut)


def _sb_kernel(q_ref, k_ref, v_ref, u_ref, o_ref, *, tq, tk, nheads, hd):
    i = pl.program_id(1)
    scale = hd ** -0.5
    row = lax.broadcasted_iota(I32, (tq, tk), 0)
    col = lax.broadcasted_iota(I32, (tq, tk), 1)
    strict = col < row
    u = u_ref[...]

    for h in range(nheads):
        sl = slice(h * hd, (h + 1) * hd)
        q = q_ref[0, :, sl]

        def tile(j, carry, acc, diag):
            start = pl.multiple_of(j * tk, tk)
            kb = k_ref[0, pl.ds(start, tk), sl]
            vb = v_ref[0, pl.ds(start, tk), sl]
            z = _dot_nt(q, kb) * scale
            lg = -(jnp.maximum(z, 0.0) + jnp.log(1.0 + jnp.exp(-jnp.abs(z))))
            if diag:
                lg = jnp.where(strict, lg, 0.0)
            l_hi, l_lo = _split2(lg)
            suf = _dot(l_hi, u) + _dot(l_lo, u)
            tot = suf[:, 0:1] + lg[:, 0:1]
            w = jnp.exp(z + lg + suf + carry)
            if diag:
                w = jnp.where(strict, w, 0.0)
            acc = acc + _dot(w.astype(BF16), vb)
            return carry + tot, acc

        carry0 = jnp.zeros((tq, 1), F32)
        acc0 = jnp.zeros((tq, hd), F32)
        carry1, acc1 = tile(i, carry0, acc0, True)

        def cond(c):
            j, alive, _, _ = c
            return jnp.logical_and(j >= 0, alive > EXP_ZERO_BELOW)

        def body(c):
            j, _, carry, acc = c
            carry, acc = tile(j, carry, acc, False)
            return j - 1, jnp.max(carry), carry, acc

        _, _, _, acc = lax.while_loop(cond, body, (i - 1, jnp.max(carry1), carry1, acc1))
        o_ref[0, :, sl] = acc.astype(o_ref.dtype)


def _sb_attention(proj, col0, tq, tk):
    b, s, _ = proj.shape
    width = SB_HEADS * HEAD_DIM
    u = _tri(tk, lambda r, c: r > c)
    kern = functools.partial(_sb_kernel, tq=tq, tk=tk, nheads=SB_HEADS, hd=HEAD_DIM)
    return pl.pallas_call(
        kern,
        grid=(b, s // tq),
        in_specs=[
            pl.BlockSpec((1, tq, width), lambda bi, i: (bi, i, col0)),
            pl.BlockSpec((1, s, width), lambda bi, i: (bi, 0, col0 + 1)),
            pl.BlockSpec((1, s, width), lambda bi, i: (bi, 0, col0 + 2)),
            pl.BlockSpec((tk, tk), lambda bi, i: (0, 0)),
        ],
        out_specs=pl.BlockSpec((1, tq, width), lambda bi, i: (bi, i, 0)),
        out_shape=jax.ShapeDtypeStruct((b, s, width), BF16),
        compiler_params=_cparams("parallel", "arbitrary"),
        name="stick_breaking_attention",
    )(proj, proj, proj, u)


def _fox_kernel(q_ref, k_ref, v_ref, cc_ref, cr_ref, o_ref, *, tq, tk, nheads, hd):
    i = pl.program_id(1)
    scale = hd ** -0.5
    row = lax.broadcasted_iota(I32, (tq, tk), 0)
    col = lax.broadcasted_iota(I32, (tq, tk), 1)
    causal = col <= row

    for h in range(nheads):
        sl = slice(h * hd, (h + 1) * hd)
        q = q_ref[0, :, sl]
        ct = cc_ref[0, :, h:h + 1]

        def scores(j):
            start = pl.multiple_of(j * tk, tk)
            kb = k_ref[0, pl.ds(start, tk), sl]
            vb = v_ref[0, pl.ds(start, tk), sl]
            cs = cr_ref[0, h:h + 1, pl.ds(start, tk)]
            return _dot_nt(q, kb) * scale + (ct - cs), vb

        s0, vb0 = scores(i)
        s0 = jnp.where(causal, s0, NEG_BIG)
        m0 = jnp.max(s0, axis=-1, keepdims=True)
        p0 = jnp.exp(s0 - m0)
        l0 = jnp.sum(p0, axis=-1, keepdims=True)
        acc0 = _dot(p0.astype(BF16), vb0)

        def body(it, c):
            m, l, acc = c
            sj, vb = scores(i - 1 - it)
            m_new = jnp.maximum(m, jnp.max(sj, axis=-1, keepdims=True))
            alpha = jnp.exp(m - m_new)
            p = jnp.exp(sj - m_new)
            l = alpha * l + jnp.sum(p, axis=-1, keepdims=True)
            acc = alpha * acc + _dot(p.astype(BF16), vb)
            return m_new, l, acc

        _, l, acc = lax.fori_loop(0, i, body, (m0, l0, acc0))
        o_ref[0, :, sl] = (acc / l).astype(o_ref.dtype)


def _fox_attention(proj, col0, cum_col, cum_row, tq, tk):
    b, s, _ = proj.shape
    width = FOX_HEADS * HEAD_DIM
    kern = functools.partial(_fox_kernel, tq=tq, tk=tk, nheads=FOX_HEADS, hd=HEAD_DIM)
    return pl.pallas_call(
        kern,
        grid=(b, s // tq),
        in_specs=[
            pl.BlockSpec((1, tq, width), lambda bi, i: (bi, i, col0)),
            pl.BlockSpec((1, s, width), lambda bi, i: (bi, 0, col0 + 1)),
            pl.BlockSpec((1, s, width), lambda bi, i: (bi, 0, col0 + 2)),
            pl.BlockSpec((1, tq, LANES), lambda bi, i: (bi, i, 0)),
            pl.BlockSpec((1, SUBLANES, s), lambda bi, i: (bi, 0, 0)),
        ],
        out_specs=pl.BlockSpec((1, tq, width), lambda bi, i: (bi, i, 0)),
        out_shape=jax.ShapeDtypeStruct((b, s, width), BF16),
        compiler_params=_cparams("parallel", "arbitrary"),
        name="forgetting_attention",
    )(proj, proj, proj, cum_col, cum_row)


def _hgrn_kernel(q_ref, f_ref, i_ref, g_ref, lb_ref, ng_ref, lt_ref, o_ref,
                 lf_ref, kk_ref, qq_ref, b_ref, st_ref, *, ts):
    @pl.when(pl.program_id(2) == 0)
    def _():
        st_ref[...] = jnp.zeros_like(st_ref)

    lb = lb_ref[...]
    z = f_ref[0]
    lsz = _log_sigmoid(z)
    a = jnp.log(lb)
    c = jnp.log(1.0 - lb) + lsz
    lf_ref[...] = jnp.maximum(a, c) + jnp.log(1.0 + jnp.exp(-jnp.abs(a - c)))
    kk_ref[...] = (1.0 - lb) * jnp.exp(lsz - z)
    qin = q_ref[0].astype(F32)
    qq_ref[...] = qin / (1.0 + jnp.exp(-qin))

    lt = lt_ref[...]
    rr = lax.broadcasted_iota(I32, (CHUNK, CHUNK), 0)
    cc = lax.broadcasted_iota(I32, (CHUNK, CHUNK), 1)
    causal = cc <= rr
    sub = lax.broadcasted_iota(I32, (SUBLANES, CHUNK), 0)
    ng = ng_ref[...]

    def chunk(ci, carry):
        r0 = pl.multiple_of(ci * CHUNK, CHUNK)
        rows = pl.ds(r0, CHUNK)
        lf = lf_ref[rows, :]
        h_hi, h_mid, h_lo = _split3(lf)
        bcum = _dot(lt, h_hi) + _dot(lt, h_mid) + _dot(lt, h_lo)
        b_ref[...] = bcum
        kk = kk_ref[rows, :]
        qq = qq_ref[rows, :]
        vv = i_ref[0, rows, :]

        groups = []
        for gi in range(CHUNK // SUBLANES):
            qg = qq[gi * SUBLANES:(gi + 1) * SUBLANES, :].astype(BF16)
            dg = jnp.zeros((SUBLANES, CHUNK), F32)
            for r in range(SUBLANES):
                t = gi * SUBLANES + r
                bt = b_ref[t:t + 1, :]
                ke = kk * jnp.exp(jnp.minimum(bt - bcum, 0.0))
                res = _dot_nt(qg, ke.astype(BF16))
                dg = jnp.where(sub == r, res, dg)
            groups.append(dg)
        scores = jnp.where(causal, jnp.concatenate(groups, axis=0), 0.0)

        st = st_ref[...]
        qe = (qq * jnp.exp(bcum)).astype(BF16)
        o = _dot(scores.astype(BF16), vv) + _dot_nt(qe, st.astype(BF16))
        b_last = bcum[CHUNK - 1:CHUNK, :]
        kd = (kk * jnp.exp(b_last - bcum)).astype(BF16)
        st_ref[...] = st * jnp.exp(b_last) + _dot_tn(vv, kd)

        gate = g_ref[0, rows, :].astype(F32)
        on = _rmsnorm(o, ng) * (gate / (1.0 + jnp.exp(-gate)))
        o_ref[0, rows, :] = on.astype(o_ref.dtype)
        return carry

    lax.fori_loop(0, ts // CHUNK, chunk, 0)


def _hgrn(proj, f_hgrn, lb, ng, colq, coli, colg, ts):
    b, s, _ = proj.shape
    dk = HGRN_DIM
    lt = _tri(CHUNK, lambda r, c: c <= r)
    kern = functools.partial(_hgrn_kernel, ts=ts)
    return pl.pallas_call(
        kern,
        grid=(b, HGRN_HEADS, s // ts),
        in_specs=[
            pl.BlockSpec((1, ts, dk), lambda bi, h, si: (bi, si, colq + h)),
            pl.BlockSpec((1, ts, dk), lambda bi, h, si: (bi, si, h)),
            pl.BlockSpec((1, ts, dk), lambda bi, h, si: (bi, si, coli + h)),
            pl.BlockSpec((1, ts, dk), lambda bi, h, si: (bi, si, colg + h)),
            pl.BlockSpec((1, dk), lambda bi, h, si: (0, h)),
            pl.BlockSpec((1, dk), lambda bi, h, si: (0, 0)),
            pl.BlockSpec((CHUNK, CHUNK), lambda bi, h, si: (0, 0)),
        ],
        out_specs=pl.BlockSpec((1, ts, dk), lambda bi, h, si: (bi, si, h)),
        out_shape=jax.ShapeDtypeStruct((b, s, HGRN_HEADS * dk), BF16),
        scratch_shapes=[
            pltpu.VMEM((ts, dk), F32), pltpu.VMEM((ts, dk), F32), pltpu.VMEM((ts, dk), F32),
            pltpu.VMEM((CHUNK, dk), F32), pltpu.VMEM((dk, dk), F32),
        ],
        compiler_params=_cparams("parallel", "parallel", "arbitrary"),
        name="hgrn2_recurrence",
    )(proj, f_hgrn, proj, proj, lb, ng, lt)


def _merge_kernel(h_ref, oa_ref, ob_ref, oc_ref, g0_ref, g1_ref, g2_ref, bg_ref,
                  wa_ref, wb_ref, wc_ref, wo_ref, ng_ref, hout_ref, hn_ref):
    def gate(gref, k):
        x = gref[...].astype(F32) + bg_ref[k:k + 1, :]
        return 1.0 / (1.0 + jnp.exp(-x))

    merged = (gate(g0_ref, 0) * _dot(oa_ref[...], wa_ref[...])
              + gate(g1_ref, 1) * _dot(ob_ref[...], wb_ref[...])
              + gate(g2_ref, 2) * _dot(oc_ref[...], wc_ref[...]))
    hnew = h_ref[...] + _dot(merged.astype(BF16), wo_ref[...])
    hout_ref[...] = hnew
    hn_ref[...] = _rmsnorm(hnew, ng_ref[...]).astype(hn_ref.dtype)


def _merge_out(h, oa, ob, oc, proj, gcol0, bg, wa, wb, wc, wo, ng, tm):
    n, d = h.shape
    full = lambda shape: pl.BlockSpec(shape, lambda i: (0,) * len(shape))
    rows = lambda w: pl.BlockSpec((tm, w), lambda i: (i, 0))
    return pl.pallas_call(
        _merge_kernel,
        grid=(n // tm,),
        in_specs=[
            rows(d), rows(oa.shape[1]), rows(ob.shape[1]), rows(oc.shape[1]),
            pl.BlockSpec((tm, d), lambda i: (i, gcol0)),
            pl.BlockSpec((tm, d), lambda i: (i, gcol0 + 1)),
            pl.BlockSpec((tm, d), lambda i: (i, gcol0 + 2)),
            full(bg.shape), full(wa.shape), full(wb.shape), full(wc.shape), full(wo.shape),
            full((1, d)),
        ],
        out_specs=[rows(d), rows(d)],
        out_shape=[jax.ShapeDtypeStruct((n, d), F32), jax.ShapeDtypeStruct((n, d), BF16)],
        compiler_params=_cparams("parallel"),
        name="merge_out_proj",
    )(h, oa, ob, oc, proj, proj, proj, bg, wa, wb, wc, wo, ng)


def _ffn_kernel(hn_ref, h_ref, wg_ref, wu_ref, wd_ref, o_ref, acc_ref):
    j = pl.program_id(1)

    @pl.when(j == 0)
    def _():
        acc_ref[...] = jnp.zeros_like(acc_ref)

    x = hn_ref[...]
    gt = _dot(x, wg_ref[...])
    up = _dot(x, wu_ref[...])
    act = (gt / (1.0 + jnp.exp(-gt))) * up
    acc_ref[...] += _dot(act.astype(BF16), wd_ref[...])

    @pl.when(j == pl.num_programs(1) - 1)
    def _():
        o_ref[...] = h_ref[...] + acc_ref[...]


def _dense_ffn(hn, h, wg, wu, wd, tm, tf):
    n, d = h.shape
    ff = wg.shape[1]
    return pl.pallas_call(
        _ffn_kernel,
        grid=(n // tm, ff // tf),
        in_specs=[
            pl.BlockSpec((tm, d), lambda i, j: (i, 0)),
            pl.BlockSpec((tm, d), lambda i, j: (i, 0)),
            pl.BlockSpec((d, tf), lambda i, j: (0, j)),
            pl.BlockSpec((d, tf), lambda i, j: (0, j)),
            pl.BlockSpec((tf, d), lambda i, j: (j, 0)),
        ],
        out_specs=pl.BlockSpec((tm, d), lambda i, j: (i, 0)),
        out_shape=jax.ShapeDtypeStruct((n, d), F32),
        scratch_shapes=[pltpu.VMEM((tm, d), F32)],
        compiler_params=_cparams("parallel", "arbitrary"),
        name="dense_swiglu",
    )(hn, h, wg, wu, wd)


def _router_kernel(hn_ref, wr_ref, u_ref, idx_ref, wt_ref, cnt_ref, car_ref):
    t = hn_ref.shape[0]

    @pl.when(pl.program_id(0) == 0)
    def _():
        car_ref[...] = jnp.zeros_like(car_ref)

    logits = _dot_nt(wr_ref[...], hn_ref[...])
    eidx = lax.broadcasted_iota(I32, (N_EXPERTS, t), 0)
    m1 = jnp.max(logits, axis=0, keepdims=True)
    i1 = jnp.min(jnp.where(logits == m1, eidx, N_EXPERTS), axis=0, keepdims=True)
    sel1 = eidx == i1
    rest = jnp.where(sel1, -jnp.inf, logits)
    m2 = jnp.max(rest, axis=0, keepdims=True)
    i2 = jnp.min(jnp.where(rest == m2, eidx, N_EXPERTS), axis=0, keepdims=True)
    sel2 = eidx == i2
    e21 = jnp.exp(m2 - m1)
    w1 = 1.0 / (1.0 + e21)
    w2 = e21 / (1.0 + e21)

    cnt = jnp.where(jnp.logical_or(sel1, sel2), 1.0, 0.0)
    excl = _dot(cnt.astype(BF16), u_ref[...])
    rank = excl + car_ref[:, 0:1]
    p1 = jnp.sum(jnp.where(sel1, rank, 0.0), axis=0, keepdims=True)
    p2 = jnp.sum(jnp.where(sel2, rank, 0.0), axis=0, keepdims=True)
    total = rank[:, t - 1:t] + cnt[:, t - 1:t]
    car_ref[...] = jnp.broadcast_to(total, car_ref.shape)
    cnt_ref[...] = jnp.broadcast_to(total, cnt_ref.shape)

    zi = jnp.zeros((1, t), I32)
    idx_ref[...] = jnp.concatenate(
        [i1, i2, p1.astype(I32), p2.astype(I32), zi, zi, zi, zi], axis=0)
    zf = jnp.zeros((1, t), F32)
    wt_ref[...] = jnp.concatenate([w1, w2, zf, zf, zf, zf, zf, zf], axis=0)


def _router(hn, wr_t, t):
    n, d = hn.shape
    u = _tri(t, lambda r, c: r < c)
    return pl.pallas_call(
        _router_kernel,
        grid=(n // t,),
        in_specs=[
            pl.BlockSpec((t, d), lambda i: (i, 0)),
            pl.BlockSpec((N_EXPERTS, d), lambda i: (0, 0)),
            pl.BlockSpec((t, t), lambda i: (0, 0)),
        ],
        out_specs=[
            pl.BlockSpec((SUBLANES, t), lambda i: (0, i)),
            pl.BlockSpec((SUBLANES, t), lambda i: (0, i)),
            pl.BlockSpec((N_EXPERTS, LANES), lambda i: (0, 0)),
        ],
        out_shape=[
            jax.ShapeDtypeStruct((SUBLANES, n), I32),
            jax.ShapeDtypeStruct((SUBLANES, n), F32),
            jax.ShapeDtypeStruct((N_EXPERTS, LANES), F32),
        ],
        scratch_shapes=[pltpu.VMEM((N_EXPERTS, LANES), F32)],
        compiler_params=_cparams("arbitrary"),
        name="moe_router_top2",
    )(hn, wr_t, u)


def _dest_kernel(idx_ref, cnt_ref, dest_ref, blk_ref, *, rblk):
    n = idx_ref.shape[1]
    nb = blk_ref.shape[1]
    shift = rblk.bit_length() - 1
    assert rblk == 1 << shift
    e1 = idx_ref[0:1, :]
    e2 = idx_ref[1:2, :]
    d1 = idx_ref[2:3, :]
    d2 = idx_ref[3:4, :]
    bstart = lax.broadcasted_iota(I32, (1, nb), 1) * rblk
    bexp = jnp.zeros((1, nb), I32)
    pend = jnp.zeros((1, 1), I32)
    for e in range(N_EXPERTS):
        c = cnt_ref[e:e + 1, 0:1].astype(I32)
        padded = lax.shift_left(lax.shift_right_logical(c + (rblk - 1), shift), shift)
        pstart = pend
        pend = pend + padded
        d1 = d1 + jnp.where(e1 == e, pstart, 0)
        d2 = d2 + jnp.where(e2 == e, pstart, 0)
        bexp = bexp + jnp.where(bstart >= pend, 1, 0)
    zi = jnp.zeros((1, n), I32)
    dest_ref[...] = jnp.concatenate([d1, d2, zi, zi, zi, zi, zi, zi], axis=0)
    zb = jnp.zeros((1, nb), I32)
    nvalid = jnp.broadcast_to(lax.shift_right_logical(pend, shift), (1, nb))
    blk_ref[...] = jnp.concatenate(
        [jnp.minimum(bexp, N_EXPERTS - 1), nvalid, zb, zb, zb, zb, zb, zb], axis=0)


def _dest(idx, cnt, rblk, nb_pad):
    n = idx.shape[1]
    return pl.pallas_call(
        functools.partial(_dest_kernel, rblk=rblk),
        out_shape=[
            jax.ShapeDtypeStruct((SUBLANES, n), I32),
            jax.ShapeDtypeStruct((SUBLANES, nb_pad), I32),
        ],
        compiler_params=pltpu.CompilerParams(vmem_limit_bytes=VMEM_LIMIT),
        name="moe_slot_assign",
    )(idx, cnt)


def _dispatch_kernel(d1_ref, d2_ref, x_hbm, buf_in, buf_hbm, sem, *, t):
    del buf_in
    base = pl.program_id(0) * t

    def row_copy(src_row, dst_row):
        return pltpu.make_async_copy(x_hbm.at[pl.ds(src_row, 1)], buf_hbm.at[pl.ds(dst_row, 1)], sem)

    def start(r, c):
        row_copy(base + r, d1_ref[r]).start()
        row_copy(base + r, d2_ref[r]).start()
        return c

    lax.fori_loop(0, t, start, 0)

    def wait(r, c):
        row_copy(base + r, d1_ref[r]).wait()
        row_copy(base + r, d2_ref[r]).wait()
        return c

    lax.fori_loop(0, t, wait, 0)


def _dispatch(x_words, d1, d2, nrows, t):
    n, w = x_words.shape
    buf0 = jnp.zeros((nrows, w), x_words.dtype)
    smem = lambda: pl.BlockSpec((t,), lambda i: (i,), memory_space=pltpu.SMEM)
    return pl.pallas_call(
        functools.partial(_dispatch_kernel, t=t),
        grid=(n // t,),
        in_specs=[smem(), smem(),
                  pl.BlockSpec(memory_space=pl.ANY), pl.BlockSpec(memory_space=pl.ANY)],
        out_specs=pl.BlockSpec(memory_space=pl.ANY),
        out_shape=jax.ShapeDtypeStruct((nrows, w), x_words.dtype),
        scratch_shapes=[pltpu.SemaphoreType.DMA(())],
        input_output_aliases={3: 0},
        compiler_params=_cparams("arbitrary"),
        name="moe_dispatch_rows",
    )(d1, d2, x_words, buf0)


def _expert_kernel(be_ref, nv_ref, x_ref, wg_ref, wu_ref, wd_ref, o_ref):
    i = pl.program_id(0)
    j = pl.program_id(1)

    @pl.when(j == 0)
    def _():
        o_ref[...] = jnp.zeros_like(o_ref)

    @pl.when(i < nv_ref[0])
    def _():
        x = x_ref[...]
        gt = _dot(x, wg_ref[0])
        up = _dot(x, wu_ref[0])
        act = (gt / (1.0 + jnp.exp(-gt))) * up
        o_ref[...] += _dot(act.astype(BF16), wd_ref[0])


def _experts(buf, be, nv, wg, wu, wd, rblk, tf):
    nrows, d = buf.shape
    ff = wg.shape[2]
    nj = ff // tf

    def wcol(i, j, be_ref, nv_ref):
        return (be_ref[i], 0, jnp.where(i < nv_ref[0], j, nj - 1))

    def wrow(i, j, be_ref, nv_ref):
        return (be_ref[i], jnp.where(i < nv_ref[0], j, nj - 1), 0)

    grid_spec = pltpu.PrefetchScalarGridSpec(
        num_scalar_prefetch=2,
        grid=(nrows // rblk, nj),
        in_specs=[
            pl.BlockSpec((rblk, d), lambda i, j, be_ref, nv_ref: (i, 0)),
            pl.BlockSpec((1, d, tf), wcol),
            pl.BlockSpec((1, d, tf), wcol),
            pl.BlockSpec((1, tf, d), wrow),
        ],
        out_specs=pl.BlockSpec((rblk, d), lambda i, j, be_ref, nv_ref: (i, 0)),
    )
    return pl.pallas_call(
        _expert_kernel,
        grid_spec=grid_spec,
        out_shape=jax.ShapeDtypeStruct((nrows, d), F32),
        compiler_params=_cparams("parallel", "arbitrary"),
        name="moe_grouped_swiglu",
    )(be, nv, buf, wg, wu, wd)


def _combine_kernel(d1_ref, d2_ref, yb_hbm, h_ref, wt_ref, ng_ref, o_ref, y_ref, sem, *, t, final):
    def row_copy(slot, dst_row, src_row):
        return pltpu.make_async_copy(yb_hbm.at[pl.ds(src_row, 1)],
                                     y_ref.at[slot, pl.ds(dst_row, 1)], sem)

    def start(r, c):
        row_copy(0, r, d1_ref[r]).start()
        row_copy(1, r, d2_ref[r]).start()
        return c

    lax.fori_loop(0, t, start, 0)

    def wait(r, c):
        row_copy(0, r, d1_ref[r]).wait()
        row_copy(1, r, d2_ref[r]).wait()
        return c

    lax.fori_loop(0, t, wait, 0)

    rr = lax.broadcasted_iota(I32, (t, t), 0)
    cc = lax.broadcasted_iota(I32, (t, t), 1)
    eye = rr == cc
    w1 = jnp.sum(jnp.where(eye, wt_ref[0:1, :], 0.0), axis=1, keepdims=True)
    w2 = jnp.sum(jnp.where(eye, wt_ref[1:2, :], 0.0), axis=1, keepdims=True)
    out = h_ref[...] + w1 * y_ref[0] + w2 * y_ref[1]
    if final:
        out = _rmsnorm(out, ng_ref[...])
    o_ref[...] = out


def _combine(yb, d1, d2, h, wt, ng, t, final):
    n, d = h.shape
    smem = lambda: pl.BlockSpec((t,), lambda i: (i,), memory_space=pltpu.SMEM)
    return pl.pallas_call(
        functools.partial(_combine_kernel, t=t, final=final),
        grid=(n // t,),
        in_specs=[
            smem(), smem(),
            pl.BlockSpec(memory_space=pl.ANY),
            pl.BlockSpec((t, d), lambda i: (i, 0)),
            pl.BlockSpec((SUBLANES, t), lambda i: (0, i)),
            pl.BlockSpec((1, d), lambda i: (0, 0)),
        ],
        out_specs=pl.BlockSpec((t, d), lambda i: (i, 0)),
        out_shape=jax.ShapeDtypeStruct((n, d), F32),
        scratch_shapes=[pltpu.VMEM((2, t, d), F32), pltpu.SemaphoreType.DMA(())],
        compiler_params=_cparams("arbitrary"),
        name="moe_combine",
    )(d1, d2, yb, h, wt, ng)


def _final_norm_kernel(x_ref, g_ref, o_ref):
    o_ref[...] = _rmsnorm(x_ref[...], g_ref[...])


def _final_norm(h, g, tm):
    n, d = h.shape
    return pl.pallas_call(
        _final_norm_kernel,
        grid=(n // tm,),
        in_specs=[pl.BlockSpec((tm, d), lambda i: (i, 0)), pl.BlockSpec((1, d), lambda i: (0, 0))],
        out_specs=pl.BlockSpec((tm, d), lambda i: (i, 0)),
        out_shape=jax.ShapeDtypeStruct((n, d), F32),
        compiler_params=_cparams("parallel"),
        name="final_rmsnorm",
    )(h, g)


def _tile(n, pref):
    t = min(n, pref)
    assert n % t == 0, (n, pref)
    return t


def _mixer(h, g_mix, w_in, b_fox, b_gate, lb, hgrn_ng, w_sb, w_fox, w_hgrn, w_out, g_ffn):
    b, s, d = h.shape
    n = b * s
    sbw = SB_HEADS * HEAD_DIM
    foxw = FOX_HEADS * HEAD_DIM
    hw = HGRN_HEADS * HGRN_DIM
    o_fox = 3 * sbw
    o_fb = o_fox + 3 * foxw
    o_hq = o_fb + FOX_HEADS
    o_hf, o_hi, o_hg = o_hq + hw, o_hq + 2 * hw, o_hq + 3 * hw
    o_gate = o_hq + 4 * hw
    w_main = jnp.concatenate(
        [w_in[:, :o_fb], w_in[:, o_hq:o_hf], w_in[:, o_hi:o_gate], w_in[:, o_gate:]], axis=1).astype(BF16)
    w_f = w_in[:, o_hf:o_hi].astype(BF16)
    w_fb = w_in[:, o_fb:o_hq]
    w_fbc = jnp.pad(w_fb, ((0, 0), (0, LANES - FOX_HEADS))).astype(BF16)
    w_fbr = jnp.pad(w_fb.T, ((0, SUBLANES - FOX_HEADS), (0, 0))).astype(BF16)
    bc = jnp.pad(b_fox, (0, LANES - FOX_HEADS)).reshape(1, LANES)
    br = jnp.broadcast_to(jnp.pad(b_fox, (0, SUBLANES - FOX_HEADS)).reshape(SUBLANES, 1), (SUBLANES, LANES))

    g_mix = g_mix.reshape(1, d)
    tm = _tile(n, 1024)
    proj = _norm_matmul(h.reshape(n, d), g_mix, w_main, tm, _tile(w_main.shape[1], 2048))
    proj = proj.reshape(b, s, -1)
    f_hgrn, cum_col, cum_row = _gate_cum(h, g_mix, w_f, w_fbc, w_fbr, bc, br, _tile(s, 1024))

    tq = _tile(s, 256)
    oa = _sb_attention(proj, 0, tq, tq)
    ob = _fox_attention(proj, o_fox // foxw, cum_col, cum_row, tq, tq)
    c_hq = (o_fox + 3 * foxw) // HGRN_DIM
    oc = _hgrn(proj, f_hgrn, lb.reshape(1, hw), hgrn_ng.reshape(1, HGRN_DIM),
               c_hq, c_hq + HGRN_HEADS, c_hq + 2 * HGRN_HEADS, _tile(s, 512))

    gcol0 = (o_fox + 3 * foxw + 3 * hw) // d
    hnew, hn = _merge_out(
        h.reshape(n, d), oa.reshape(n, sbw), ob.reshape(n, foxw), oc.reshape(n, hw),
        proj.reshape(n, -1), gcol0, b_gate,
        w_sb.astype(BF16), w_fox.astype(BF16), w_hgrn.astype(BF16), w_out.astype(BF16),
        g_ffn.reshape(1, d), _tile(n, 512))
    return hnew, hn


def _moe(h, hn, w_router, w_gate, w_up, w_down, final_g, final):
    n, d = h.shape
    rblk = 512
    nrows = n * TOP_K + N_EXPERTS * rblk
    nb = nrows // rblk
    nb_pad = -(-nb // LANES) * LANES
    idx, wt, cnt = _router(hn, w_router.T.astype(BF16), _tile(n, 1024))
    dest, blk = _dest(idx, cnt, rblk, nb_pad)
    d1, d2 = dest[0], dest[1]
    words = lax.bitcast_convert_type(hn.reshape(n, d // 2, 2), jnp.uint32)
    buf = _dispatch(words, d1, d2, nrows, _tile(n, 512))
    buf = lax.bitcast_convert_type(buf, BF16).reshape(nrows, d)
    yb = _experts(buf, blk[0, :nb], blk[1, :1], w_gate.astype(BF16), w_up.astype(BF16),
                  w_down.astype(BF16), rblk, 512)
    return _combine(yb, d1, d2, h, wt, final_g.reshape(1, d), _tile(n, 256), final)


def kernel(x, norm_mix_g, w_in, b_fox, b_gate, hgrn_lb_logits, hgrn_norm_g, w_branch_sb, w_branch_fox, w_branch_hgrn, w_out, norm_ffn_g, w_ffn_gate, w_ffn_up, w_ffn_down, w_router, w_exp_gate, w_exp_up, w_exp_down, final_norm_g):
    b, s, d = x.shape
    n = b * s
    depth = w_in.shape[0]
    lbs = _lower_bounds(hgrn_lb_logits)
    h = x
    for l in range(depth):
        h2, hn = _mixer(h, norm_mix_g[l], w_in[l], b_fox[l], b_gate[l], lbs[l], hgrn_norm_g[l],
                        w_branch_sb[l], w_branch_fox[l], w_branch_hgrn[l], w_out[l], norm_ffn_g[l])
        last = l == depth - 1
        if l % 2 == 0:
            i = l // 2
            h2 = _dense_ffn(hn, h2, w_ffn_gate[i].astype(BF16), w_ffn_up[i].astype(BF16),
                            w_ffn_down[i].astype(BF16), _tile(n, 1024), 768)
            if last:
                h2 = _final_norm(h2, final_norm_g.reshape(1, d), _tile(n, 1024))
        else:
            i = l // 2
            h2 = _moe(h2, hn, w_router[i], w_exp_gate[i], w_exp_up[i], w_exp_down[i],
                      final_norm_g, last)
        h = h2.reshape(b, s, d)
    return h
```

```python
import functools

import jax
import jax.numpy as jnp
from jax import lax
from jax.experimental import pallas as pl
from jax.experimental.pallas import tpu as pltpu

F32 = jnp.float32
BF16 = jnp.bfloat16
I32 = jnp.int32

RMS_EPS = 1e-6
CHUNK = 64
SUBCHUNK = 16
SB_HEADS = 4
FOX_HEADS = 4
HGRN_HEADS = 4
HEAD_DIM = 64
HGRN_DIM = 128
N_EXPERTS = 8
TOP_K = 2

LANES = 128
SUBLANES = 8
VMEM_LIMIT = 56 * 1024 * 1024

EXP_ZERO_BELOW = -105.0
NEG_BIG = -1e30


def _cparams(*sem):
    return pltpu.CompilerParams(dimension_semantics=sem, vmem_limit_bytes=VMEM_LIMIT)


def _dot(a, b):
    return jnp.dot(a, b, preferred_element_type=F32)


def _dot_nt(a, b):
    return lax.dot_general(a, b, (((1,), (1,)), ((), ())), preferred_element_type=F32)


def _dot_tn(a, b):
    return lax.dot_general(a, b, (((0,), (0,)), ((), ())), preferred_element_type=F32)


def _split2(x):
    hi = x.astype(BF16)
    lo = (x - hi.astype(F32)).astype(BF16)
    return hi, lo


def _split3(x):
    hi = x.astype(BF16)
    r = x - hi.astype(F32)
    mid = r.astype(BF16)
    lo = (r - mid.astype(F32)).astype(BF16)
    return hi, mid, lo


def _log_sigmoid(x):
    return jnp.minimum(x, 0.0) - jnp.log(1.0 + jnp.exp(-jnp.abs(x)))


def _rmsnorm(x, g):
    ms = jnp.mean(x * x, axis=-1, keepdims=True)
    return x * lax.rsqrt(ms + RMS_EPS) * g


def _tri(n, fn, dtype=BF16):
    r = lax.broadcasted_iota(I32, (n, n), 0)
    c = lax.broadcasted_iota(I32, (n, n), 1)
    return fn(r, c).astype(dtype)


def _lb_kernel(x_ref, o_ref):
    x = x_ref[...]
    depth = x.shape[0]
    m = jnp.max(x, axis=0, keepdims=True)
    e = jnp.exp(x - m)
    sm = e / jnp.sum(e, axis=0, keepdims=True)
    run = sm[0:1]
    first = run
    o_ref[0:1, :] = run - first
    for l in range(1, depth):
        run = run + sm[l:l + 1]
        o_ref[l:l + 1, :] = run - first


def _lower_bounds(logits):
    return pl.pallas_call(
        _lb_kernel,
        out_shape=jax.ShapeDtypeStruct(logits.shape, F32),
        name="hgrn_lower_bounds",
    )(logits.astype(F32))


def _norm_matmul_kernel(x_ref, g_ref, w_ref, wvt_ref, o_ref, vt_ref, xn_ref):
    @pl.when(pl.program_id(1) == 0)
    def _():
        xn = _rmsnorm(x_ref[...], g_ref[...]).astype(BF16)
        xn_ref[...] = xn
        vt_ref[0] = _dot_nt(wvt_ref[...], xn).astype(vt_ref.dtype)

    o_ref[...] = _dot(xn_ref[...], w_ref[...]).astype(o_ref.dtype)


def _norm_matmul(x, g, w, wvt, seq, tm, tn):
    n, d = x.shape
    nout = w.shape[1]
    nv = wvt.shape[0]
    per = seq // tm
    return pl.pallas_call(
        _norm_matmul_kernel,
        grid=(n // tm, nout // tn),
        in_specs=[
            pl.BlockSpec((tm, d), lambda i, j: (i, 0)),
            pl.BlockSpec((1, d), lambda i, j: (0, 0)),
            pl.BlockSpec((d, tn), lambda i, j: (0, j)),
            pl.BlockSpec((nv, d), lambda i, j: (0, 0)),
        ],
        out_specs=[
            pl.BlockSpec((tm, tn), lambda i, j: (i, j)),
            pl.BlockSpec((1, nv, tm), lambda i, j: (i // per, 0, i % per)),
        ],
        out_shape=[
            jax.ShapeDtypeStruct((n, nout), BF16),
            jax.ShapeDtypeStruct((n // seq, nv, seq), BF16),
        ],
        scratch_shapes=[pltpu.VMEM((tm, d), BF16)],
        compiler_params=_cparams("parallel", "arbitrary"),
        name="norm_in_proj",
    )(x, g, w, wvt)


def _gate_cum_kernel(x_ref, g_ref, wf_ref, wc_ref, wr_ref, bc_ref, br_ref, lt_ref, ut_ref,
                     f_ref, cc_ref, cr_ref, carc_ref, carr_ref):
    tm = x_ref.shape[1]

    @pl.when(pl.program_id(1) == 0)
    def _():
        carc_ref[...] = jnp.zeros_like(carc_ref)
        carr_ref[...] = jnp.zeros_like(carr_ref)

    xn = _rmsnorm(x_ref[0], g_ref[...]).astype(BF16)
    f_ref[0] = _dot(xn, wf_ref[...])

    lc = _log_sigmoid(_dot(xn, wc_ref[...]) + bc_ref[...])
    lt = lt_ref[...]
    c_hi, c_mid, c_lo = _split3(lc)
    cum_c = _dot(lt, c_hi) + _dot(lt, c_mid) + _dot(lt, c_lo) + carc_ref[0:1, :]
    cc_ref[0] = cum_c
    carc_ref[...] = jnp.broadcast_to(cum_c[tm - 1:tm, :], carc_ref.shape)

    lr = _log_sigmoid(_dot_nt(wr_ref[...], xn) + br_ref[:, 0:1])
    ut = ut_ref[...]
    r_hi, r_mid, r_lo = _split3(lr)
    cum_r = _dot(r_hi, ut) + _dot(r_mid, ut) + _dot(r_lo, ut) + carr_ref[:, 0:1]
    cr_ref[0] = cum_r
    carr_ref[...] = jnp.broadcast_to(cum_r[:, tm - 1:tm], carr_ref.shape)


def _gate_cum(h, g, wf, wc, wr, bc, br, tm):
    b, s, d = h.shape
    nf = wf.shape[1]
    lt = _tri(tm, lambda r, c: c <= r)
    ut = _tri(tm, lambda r, c: r <= c)
    full = lambda shape: pl.BlockSpec(shape, lambda bi, si: (0,) * len(shape))
    return pl.pallas_call(
        _gate_cum_kernel,
        grid=(b, s // tm),
        in_specs=[
            pl.BlockSpec((1, tm, d), lambda bi, si: (bi, si, 0)),
            full((1, d)), full((d, nf)), full((d, LANES)), full((SUBLANES, d)),
            full((1, LANES)), full((SUBLANES, LANES)), full((tm, tm)), full((tm, tm)),
        ],
        out_specs=[
            pl.BlockSpec((1, tm, nf), lambda bi, si: (bi, si, 0)),
            pl.BlockSpec((1, tm, LANES), lambda bi, si: (bi, si, 0)),
            pl.BlockSpec((1, SUBLANES, tm), lambda bi, si: (bi, 0, si)),
        ],
        out_shape=[
            jax.ShapeDtypeStruct((b, s, nf), F32),
            jax.ShapeDtypeStruct((b, s, LANES), F32),
            jax.ShapeDtypeStruct((b, SUBLANES, s), F32),
        ],
        scratch_shapes=[pltpu.VMEM((SUBLANES, LANES), F32), pltpu.VMEM((SUBLANES, LANES), F32)],
        compiler_params=_cparams("parallel", "arbitrary"),
        name="gate_proj_cum",
    )(h, g, wf, wc, wr, bc, br, lt, ut)


def _masked_queries(q_ref, qm_ref, nheads, hd, scale):
    q = q_ref[0]
    lane = lax.broadcasted_iota(I32, q.shape, 1)
    qs = q * jnp.asarray(scale, q.dtype)
    for h in range(nheads):
        sel = jnp.logical_and(lane >= h * hd, lane < (h + 1) * hd)
        qm_ref[h] = jnp.where(sel, qs, jnp.zeros_like(qs))


def _sb_kernel(q_ref, k_ref, vt_ref, u_ref, o_ref, qm_ref, car_ref, acc_ref, *, tq, tk, nheads, hd, vrow0):
    i = pl.program_id(1)
    _masked_queries(q_ref, qm_ref, nheads, hd, hd ** -0.5)
    car_ref[...] = jnp.zeros_like(car_ref)
    acc_ref[...] = jnp.zeros_like(acc_ref)
    u = u_ref[...]

    def tile(j, diag):
        start = pl.multiple_of(j * tk, tk)
        kb = k_ref[0, pl.ds(start, tk), :]
        if diag:
            srow = start + lax.broadcasted_iota(I32, (tk, tq), 0)
            tcol = i * tq + lax.broadcasted_iota(I32, (tk, tq), 1)
            strict = srow < tcol
        alive = None
        for h in range(nheads):
            z = _dot_nt(kb, qm_ref[h])
            lg = -(jnp.maximum(z, 0.0) + jnp.log(1.0 + jnp.exp(-jnp.abs(z))))
            if diag:
                lg = jnp.where(strict, lg, 0.0)
            l_hi, l_lo = _split2(lg)
            suf = _dot(u, l_hi) + _dot(u, l_lo)
            carry = car_ref[h]
            w = jnp.exp(z + lg + suf + carry)
            if diag:
                w = jnp.where(strict, w, 0.0)
            vt = vt_ref[0, vrow0 + h * hd:vrow0 + (h + 1) * hd, pl.ds(start, tk)]
            acc_ref[h] += _dot(vt, w.astype(BF16))
            carry = carry + (suf[0:1, :] + lg[0:1, :])
            car_ref[h] = carry
            top = jnp.max(carry)
            alive = top if alive is None else jnp.maximum(alive, top)
        return alive

    j0 = (i * tq) // tk
    alive0 = tile(j0, True)

    def cond(c):
        j, alive = c
        return jnp.logical_and(j >= 0, alive > EXP_ZERO_BELOW)

    def body(c):
        j, _ = c
        return j - 1, tile(j, False)

    lax.while_loop(cond, body, (j0 - 1, alive0))
    o_ref[0] = acc_ref[...].reshape(nheads * hd, tq).T.astype(o_ref.dtype)


def _sb_attention(proj, vt, col0, vrow0, tq, tk):
    b, s, _ = proj.shape
    width = SB_HEADS * HEAD_DIM
    nvt = vt.shape[1]
    u = _tri(tk, lambda r, c: c > r)
    kern = functools.partial(_sb_kernel, tq=tq, tk=tk, nheads=SB_HEADS, hd=HEAD_DIM, vrow0=vrow0)
    return pl.pallas_call(
        kern,
        grid=(b, s // tq),
        in_specs=[
            pl.BlockSpec((1, tq, width), lambda bi, i: (bi, i, col0)),
            pl.BlockSpec((1, s, width), lambda bi, i: (bi, 0, col0 + 1)),
            pl.BlockSpec((1, nvt, s), lambda bi, i: (bi, 0, 0)),
            pl.BlockSpec((tk, tk), lambda bi, i: (0, 0)),
        ],
        out_specs=pl.BlockSpec((1, tq, width), lambda bi, i: (bi, i, 0)),
        out_shape=jax.ShapeDtypeStruct((b, s, width), BF16),
        scratch_shapes=[pltpu.VMEM((SB_HEADS, tq, width), BF16),
                        pltpu.VMEM((SB_HEADS, 1, tq), F32),
                        pltpu.VMEM((SB_HEADS, HEAD_DIM, tq), F32)],
        compiler_params=_cparams("parallel", "arbitrary"),
        name="stick_breaking_attention",
    )(proj, proj, vt, u)


def _fox_kernel(q_ref, k_ref, vt_ref, cc_ref, cr_ref, o_ref, qm_ref, m_ref, l_ref, acc_ref,
                *, tq, tk, nheads, hd, vrow0):
    i = pl.program_id(1)
    _masked_queries(q_ref, qm_ref, nheads, hd, hd ** -0.5)

    def tile(j, diag):
        start = pl.multiple_of(j * tk, tk)
        kb = k_ref[0, pl.ds(start, tk), :]
        if diag:
            srow = start + lax.broadcasted_iota(I32, (tk, tq), 0)
            tcol = i * tq + lax.broadcasted_iota(I32, (tk, tq), 1)
            causal = srow <= tcol
        for h in range(nheads):
            cs = cc_ref[0, pl.ds(start, tk), h:h + 1]
            ct = cr_ref[0, h:h + 1, :]
            vt = vt_ref[0, vrow0 + h * hd:vrow0 + (h + 1) * hd, pl.ds(start, tk)]
            s = _dot_nt(kb, qm_ref[h]) + (ct - cs)
            if diag:
                s = jnp.where(causal, s, NEG_BIG)
                m_new = jnp.max(s, axis=0, keepdims=True)
                p = jnp.exp(s - m_new)
                l_ref[h] = jnp.sum(p, axis=0, keepdims=True)
                acc_ref[h] = _dot(vt, p.astype(BF16))
            else:
                m = m_ref[h]
                m_new = jnp.maximum(m, jnp.max(s, axis=0, keepdims=True))
                alpha = jnp.exp(m - m_new)
                p = jnp.exp(s - m_new)
                l_ref[h] = alpha * l_ref[h] + jnp.sum(p, axis=0, keepdims=True)
                acc_ref[h] = alpha * acc_ref[h] + _dot(vt, p.astype(BF16))
            m_ref[h] = m_new

    j0 = (i * tq) // tk
    tile(j0, True)

    def body(it, c):
        tile(j0 - 1 - it, False)
        return c

    lax.fori_loop(0, j0, body, 0)
    out = acc_ref[...] / l_ref[...]
    o_ref[0] = out.reshape(nheads * hd, tq).T.astype(o_ref.dtype)


def _fox_attention(proj, vt, col0, vrow0, cum_col, cum_row, tq, tk):
    b, s, _ = proj.shape
    width = FOX_HEADS * HEAD_DIM
    nvt = vt.shape[1]
    kern = functools.partial(_fox_kernel, tq=tq, tk=tk, nheads=FOX_HEADS, hd=HEAD_DIM, vrow0=vrow0)
    return pl.pallas_call(
        kern,
        grid=(b, s // tq),
        in_specs=[
            pl.BlockSpec((1, tq, width), lambda bi, i: (bi, i, col0)),
            pl.BlockSpec((1, s, width), lambda bi, i: (bi, 0, col0 + 1)),
            pl.BlockSpec((1, nvt, s), lambda bi, i: (bi, 0, 0)),
            pl.BlockSpec((1, s, LANES), lambda bi, i: (bi, 0, 0)),
            pl.BlockSpec((1, SUBLANES, tq), lambda bi, i: (bi, 0, i)),
        ],
        out_specs=pl.BlockSpec((1, tq, width), lambda bi, i: (bi, i, 0)),
        out_shape=jax.ShapeDtypeStruct((b, s, width), BF16),
        scratch_shapes=[pltpu.VMEM((FOX_HEADS, tq, width), BF16),
                        pltpu.VMEM((FOX_HEADS, 1, tq), F32),
                        pltpu.VMEM((FOX_HEADS, 1, tq), F32),
                        pltpu.VMEM((FOX_HEADS, HEAD_DIM, tq), F32)],
        compiler_params=_cparams("parallel", "arbitrary"),
        name="forgetting_attention",
    )(proj, proj, vt, cum_col, cum_row)


def _hgrn_kernel(q_ref, f_ref, i_ref, g_ref, lb_ref, ng_ref, lt_ref, o_ref,
                 lf_ref, kk_ref, qq_ref, b_ref, st_ref, *, ts):
    @pl.when(pl.program_id(2) == 0)
    def _():
        st_ref[...] = jnp.zeros_like(st_ref)

    lb = lb_ref[...]
    z = f_ref[0]
    lsz = _log_sigmoid(z)
    a = jnp.log(lb)
    c = jnp.log(1.0 - lb) + lsz
    lf_ref[...] = jnp.maximum(a, c) + jnp.log(1.0 + jnp.exp(-jnp.abs(a - c)))
    kk_ref[...] = (1.0 - lb) * jnp.exp(lsz - z)
    qin = q_ref[0].astype(F32)
    qq_ref[...] = qin / (1.0 + jnp.exp(-qin))

    lt = lt_ref[...]
    nsub = CHUNK // SUBCHUNK
    rr = lax.broadcasted_iota(I32, (SUBCHUNK, SUBCHUNK), 0)
    cc = lax.broadcasted_iota(I32, (SUBCHUNK, SUBCHUNK), 1)
    causal = cc <= rr
    sub = lax.broadcasted_iota(I32, (SUBLANES, SUBCHUNK), 0)
    ng = ng_ref[...]

    def chunk(ci, carry):
        r0 = pl.multiple_of(ci * CHUNK, CHUNK)
        rows = pl.ds(r0, CHUNK)
        lf = lf_ref[rows, :]
        h_hi, h_mid, h_lo = _split3(lf)
        bcum = _dot(lt, h_hi) + _dot(lt, h_mid) + _dot(lt, h_lo)
        b_ref[...] = bcum
        kk = kk_ref[rows, :]
        qq = qq_ref[rows, :]
        vv = i_ref[0, rows, :]

        outs = []
        for bi in range(nsub):
            lo = bi * SUBCHUNK
            hi = lo + SUBCHUNK
            b_i = bcum[lo:hi, :]
            kk_i = kk[lo:hi, :]
            groups = []
            for gi in range(SUBCHUNK // SUBLANES):
                g0 = lo + gi * SUBLANES
                qg = qq[g0:g0 + SUBLANES, :].astype(BF16)
                dg = jnp.zeros((SUBLANES, SUBCHUNK), F32)
                for r in range(SUBLANES):
                    t = g0 + r
                    bt = b_ref[t:t + 1, :]
                    ke = kk_i * jnp.exp(jnp.minimum(bt - b_i, 0.0))
                    res = _dot_nt(qg, ke.astype(BF16))
                    dg = jnp.where(sub == r, res, dg)
                groups.append(dg)
            dblk = jnp.where(causal, jnp.concatenate(groups, axis=0), 0.0)
            o_i = _dot(dblk.astype(BF16), vv[lo:hi, :])
            if bi > 0:
                bref = b_ref[lo - 1:lo, :]
                q_i = (qq[lo:hi, :] * jnp.exp(b_i - bref)).astype(BF16)
                k_p = (kk[0:lo, :] * jnp.exp(bref - bcum[0:lo, :])).astype(BF16)
                s_i = _dot_nt(q_i, k_p)
                o_i = o_i + _dot(s_i.astype(BF16), vv[0:lo, :])
            outs.append(o_i)

        st = st_ref[...]
        qe = (qq * jnp.exp(bcum)).astype(BF16)
        o = jnp.concatenate(outs, axis=0) + _dot_nt(qe, st.astype(BF16))
        b_last = bcum[CHUNK - 1:CHUNK, :]
        kd = (kk * jnp.exp(b_last - bcum)).astype(BF16)
        st_ref[...] = st * jnp.exp(b_last) + _dot_tn(vv, kd)

        gate = g_ref[0, rows, :].astype(F32)
        on = _rmsnorm(o, ng) * (gate / (1.0 + jnp.exp(-gate)))
        o_ref[0, rows, :] = on.astype(o_ref.dtype)
        return carry

    lax.fori_loop(0, ts // CHUNK, chunk, 0)


def _hgrn(proj, f_hgrn, lb, ng, colq, coli, colg, ts):
    b, s, _ = proj.shape
    dk = HGRN_DIM
    lt = _tri(CHUNK, lambda r, c: c <= r)
    kern = functools.partial(_hgrn_kernel, ts=ts)
    return pl.pallas_call(
        kern,
        grid=(b, HGRN_HEADS, s // ts),
        in_specs=[
            pl.BlockSpec((1, ts, dk), lambda bi, h, si: (bi, si, colq + h)),
            pl.BlockSpec((1, ts, dk), lambda bi, h, si: (bi, si, h)),
            pl.BlockSpec((1, ts, dk), lambda bi, h, si: (bi, si, coli + h)),
            pl.BlockSpec((1, ts, dk), lambda bi, h, si: (bi, si, colg + h)),
            pl.BlockSpec((1, dk), lambda bi, h, si: (0, h)),
            pl.BlockSpec((1, dk), lambda bi, h, si: (0, 0)),
            pl.BlockSpec((CHUNK, CHUNK), lambda bi, h, si: (0, 0)),
        ],
        out_specs=pl.BlockSpec((1, ts, dk), lambda bi, h, si: (bi, si, h)),
        out_shape=jax.ShapeDtypeStruct((b, s, HGRN_HEADS * dk), BF16),
        scratch_shapes=[
            pltpu.VMEM((ts, dk), F32), pltpu.VMEM((ts, dk), F32), pltpu.VMEM((ts, dk), F32),
            pltpu.VMEM((CHUNK, dk), F32), pltpu.VMEM((dk, dk), F32),
        ],
        compiler_params=_cparams("parallel", "parallel", "arbitrary"),
        name="hgrn2_recurrence",
    )(proj, f_hgrn, proj, proj, lb, ng, lt)


def _merge_kernel(h_ref, oa_ref, ob_ref, oc_ref, g0_ref, g1_ref, g2_ref, bg_ref,
                  wa_ref, wb_ref, wc_ref, wo_ref, ng_ref, hout_ref, hn_ref):
    def gate(gref, k):
        x = gref[...].astype(F32) + bg_ref[k:k + 1, :]
        return 1.0 / (1.0 + jnp.exp(-x))

    merged = (gate(g0_ref, 0) * _dot(oa_ref[...], wa_ref[...])
              + gate(g1_ref, 1) * _dot(ob_ref[...], wb_ref[...])
              + gate(g2_ref, 2) * _dot(oc_ref[...], wc_ref[...]))
    hnew = h_ref[...] + _dot(merged.astype(BF16), wo_ref[...])
    hout_ref[...] = hnew
    hn_ref[...] = _rmsnorm(hnew, ng_ref[...]).astype(hn_ref.dtype)


def _merge_out(h, oa, ob, oc, proj, gcol0, bg, wa, wb, wc, wo, ng, hn_dtype, tm):
    n, d = h.shape
    full = lambda shape: pl.BlockSpec(shape, lambda i: (0,) * len(shape))
    rows = lambda w: pl.BlockSpec((tm, w), lambda i: (i, 0))
    return pl.pallas_call(
        _merge_kernel,
        grid=(n // tm,),
        in_specs=[
            rows(d), rows(oa.shape[1]), rows(ob.shape[1]), rows(oc.shape[1]),
            pl.BlockSpec((tm, d), lambda i: (i, gcol0)),
            pl.BlockSpec((tm, d), lambda i: (i, gcol0 + 1)),
            pl.BlockSpec((tm, d), lambda i: (i, gcol0 + 2)),
            full(bg.shape), full(wa.shape), full(wb.shape), full(wc.shape), full(wo.shape),
            full((1, d)),
        ],
        out_specs=[rows(d), rows(d)],
        out_shape=[jax.ShapeDtypeStruct((n, d), F32), jax.ShapeDtypeStruct((n, d), hn_dtype)],
        compiler_params=_cparams("parallel"),
        name="merge_out_proj",
    )(h, oa, ob, oc, proj, proj, proj, bg, wa, wb, wc, wo, ng)


def _ffn_kernel(hn_ref, h_ref, wg_ref, wu_ref, wd_ref, o_ref, acc_ref):
    j = pl.program_id(1)

    @pl.when(j == 0)
    def _():
        acc_ref[...] = jnp.zeros_like(acc_ref)

    x = hn_ref[...]
    gt = _dot(x, wg_ref[...])
    up = _dot(x, wu_ref[...])
    act = (gt / (1.0 + jnp.exp(-gt))) * up
    acc_ref[...] += _dot(act.astype(BF16), wd_ref[...])

    @pl.when(j == pl.num_programs(1) - 1)
    def _():
        o_ref[...] = h_ref[...] + acc_ref[...]


def _dense_ffn(hn, h, wg, wu, wd, tm, tf):
    n, d = h.shape
    ff = wg.shape[1]
    return pl.pallas_call(
        _ffn_kernel,
        grid=(n // tm, ff // tf),
        in_specs=[
            pl.BlockSpec((tm, d), lambda i, j: (i, 0)),
            pl.BlockSpec((tm, d), lambda i, j: (i, 0)),
            pl.BlockSpec((d, tf), lambda i, j: (0, j)),
            pl.BlockSpec((d, tf), lambda i, j: (0, j)),
            pl.BlockSpec((tf, d), lambda i, j: (j, 0)),
        ],
        out_specs=pl.BlockSpec((tm, d), lambda i, j: (i, 0)),
        out_shape=jax.ShapeDtypeStruct((n, d), F32),
        scratch_shapes=[pltpu.VMEM((tm, d), F32)],
        compiler_params=_cparams("parallel", "arbitrary"),
        name="dense_swiglu",
    )(hn, h, wg, wu, wd)


def _router_kernel(hn_ref, wr_ref, u_ref, idx_ref, wt_ref, cnt_ref, car_ref):
    t = hn_ref.shape[0]

    @pl.when(pl.program_id(0) == 0)
    def _():
        car_ref[...] = jnp.zeros_like(car_ref)

    logits = _dot_nt(wr_ref[...], hn_ref[...].astype(BF16))
    eidx = lax.broadcasted_iota(I32, (N_EXPERTS, t), 0)
    m1 = jnp.max(logits, axis=0, keepdims=True)
    i1 = jnp.min(jnp.where(logits == m1, eidx, N_EXPERTS), axis=0, keepdims=True)
    sel1 = eidx == i1
    rest = jnp.where(sel1, -jnp.inf, logits)
    m2 = jnp.max(rest, axis=0, keepdims=True)
    i2 = jnp.min(jnp.where(rest == m2, eidx, N_EXPERTS), axis=0, keepdims=True)
    sel2 = eidx == i2
    e21 = jnp.exp(m2 - m1)
    w1 = 1.0 / (1.0 + e21)
    w2 = e21 / (1.0 + e21)

    cnt = jnp.where(jnp.logical_or(sel1, sel2), 1.0, 0.0)
    excl = _dot(cnt.astype(BF16), u_ref[...])
    rank = excl + car_ref[:, 0:1]
    p1 = jnp.sum(jnp.where(sel1, rank, 0.0), axis=0, keepdims=True)
    p2 = jnp.sum(jnp.where(sel2, rank, 0.0), axis=0, keepdims=True)
    total = rank[:, t - 1:t] + cnt[:, t - 1:t]
    car_ref[...] = jnp.broadcast_to(total, car_ref.shape)
    cnt_ref[...] = jnp.broadcast_to(total, cnt_ref.shape)

    zi = jnp.zeros((1, t), I32)
    idx_ref[...] = jnp.concatenate(
        [i1, i2, p1.astype(I32), p2.astype(I32), zi, zi, zi, zi], axis=0)
    zf = jnp.zeros((1, t), F32)
    wt_ref[...] = jnp.concatenate([w1, w2, zf, zf, zf, zf, zf, zf], axis=0)


def _router(hn, wr_t, t):
    n, d = hn.shape
    u = _tri(t, lambda r, c: r < c)
    return pl.pallas_call(
        _router_kernel,
        grid=(n // t,),
        in_specs=[
            pl.BlockSpec((t, d), lambda i: (i, 0)),
            pl.BlockSpec((N_EXPERTS, d), lambda i: (0, 0)),
            pl.BlockSpec((t, t), lambda i: (0, 0)),
        ],
        out_specs=[
            pl.BlockSpec((SUBLANES, t), lambda i: (0, i)),
            pl.BlockSpec((SUBLANES, t), lambda i: (0, i)),
            pl.BlockSpec((N_EXPERTS, LANES), lambda i: (0, 0)),
        ],
        out_shape=[
            jax.ShapeDtypeStruct((SUBLANES, n), I32),
            jax.ShapeDtypeStruct((SUBLANES, n), F32),
            jax.ShapeDtypeStruct((N_EXPERTS, LANES), F32),
        ],
        scratch_shapes=[pltpu.VMEM((N_EXPERTS, LANES), F32)],
        compiler_params=_cparams("arbitrary"),
        name="moe_router_top2",
    )(hn, wr_t, u)


def _dest_kernel(idx_ref, cnt_ref, dest_ref, blk_ref, *, rblk):
    n = idx_ref.shape[1]
    nb = blk_ref.shape[1]
    shift = rblk.bit_length() - 1
    assert rblk == 1 << shift
    e1 = idx_ref[0:1, :]
    e2 = idx_ref[1:2, :]
    d1 = idx_ref[2:3, :]
    d2 = idx_ref[3:4, :]
    bstart = lax.broadcasted_iota(I32, (1, nb), 1) * rblk
    bexp = jnp.zeros((1, nb), I32)
    pend = jnp.zeros((1, 1), I32)
    for e in range(N_EXPERTS):
        c = cnt_ref[e:e + 1, 0:1].astype(I32)
        padded = lax.shift_left(lax.shift_right_logical(c + (rblk - 1), shift), shift)
        pstart = pend
        pend = pend + padded
        d1 = d1 + jnp.where(e1 == e, pstart, 0)
        d2 = d2 + jnp.where(e2 == e, pstart, 0)
        bexp = bexp + jnp.where(bstart >= pend, 1, 0)
    zi = jnp.zeros((1, n), I32)
    dest_ref[...] = jnp.concatenate([d1, d2, zi, zi, zi, zi, zi, zi], axis=0)
    zb = jnp.zeros((1, nb), I32)
    nvalid = jnp.broadcast_to(lax.shift_right_logical(pend, shift), (1, nb))
    blk_ref[...] = jnp.concatenate(
        [jnp.minimum(bexp, N_EXPERTS - 1), nvalid, zb, zb, zb, zb, zb, zb], axis=0)


def _dest(idx, cnt, rblk, nb_pad):
    n = idx.shape[1]
    return pl.pallas_call(
        functools.partial(_dest_kernel, rblk=rblk),
        out_shape=[
            jax.ShapeDtypeStruct((SUBLANES, n), I32),
            jax.ShapeDtypeStruct((SUBLANES, nb_pad), I32),
        ],
        compiler_params=pltpu.CompilerParams(vmem_limit_bytes=VMEM_LIMIT),
        name="moe_slot_assign",
    )(idx, cnt)


def _dispatch_kernel(d1_ref, d2_ref, x_ref, buf_in, buf_hbm, sem, *, t):
    del buf_in

    def row_copy(src_row, dst_row):
        return pltpu.make_async_copy(x_ref.at[pl.ds(src_row, 1)], buf_hbm.at[pl.ds(dst_row, 1)], sem)

    def start(r, c):
        row_copy(r, d1_ref[r]).start()
        row_copy(r, d2_ref[r]).start()
        return c

    lax.fori_loop(0, t, start, 0)

    def wait(r, c):
        row_copy(r, d1_ref[r]).wait()
        row_copy(r, d2_ref[r]).wait()
        return c

    lax.fori_loop(0, t, wait, 0)


def _dispatch(x, d1, d2, nrows, t):
    n, d = x.shape
    buf0 = jnp.zeros((nrows, d), x.dtype)
    smem = lambda: pl.BlockSpec((t,), lambda i: (i,), memory_space=pltpu.SMEM)
    return pl.pallas_call(
        functools.partial(_dispatch_kernel, t=t),
        grid=(n // t,),
        in_specs=[smem(), smem(),
                  pl.BlockSpec((t, d), lambda i: (i, 0)), pl.BlockSpec(memory_space=pl.ANY)],
        out_specs=pl.BlockSpec(memory_space=pl.ANY),
        out_shape=jax.ShapeDtypeStruct((nrows, d), x.dtype),
        scratch_shapes=[pltpu.SemaphoreType.DMA(())],
        input_output_aliases={3: 0},
        compiler_params=_cparams("arbitrary"),
        name="moe_dispatch_rows",
    )(d1, d2, x, buf0)


def _expert_kernel(be_ref, nv_ref, x_ref, wg_ref, wu_ref, wd_ref, o_ref, xb_ref):
    i = pl.program_id(0)
    j = pl.program_id(1)

    @pl.when(j == 0)
    def _():
        o_ref[...] = jnp.zeros_like(o_ref)
        xb_ref[...] = x_ref[...].astype(BF16)

    @pl.when(i < nv_ref[0])
    def _():
        x = xb_ref[...]
        gt = _dot(x, wg_ref[0])
        up = _dot(x, wu_ref[0])
        act = (gt / (1.0 + jnp.exp(-gt))) * up
        o_ref[...] += _dot(act.astype(BF16), wd_ref[0])


def _experts(buf, be, nv, wg, wu, wd, rblk, tf):
    nrows, d = buf.shape
    ff = wg.shape[2]
    nj = ff // tf

    def wcol(i, j, be_ref, nv_ref):
        return (be_ref[i], 0, jnp.where(i < nv_ref[0], j, nj - 1))

    def wrow(i, j, be_ref, nv_ref):
        return (be_ref[i], jnp.where(i < nv_ref[0], j, nj - 1), 0)

    grid_spec = pltpu.PrefetchScalarGridSpec(
        num_scalar_prefetch=2,
        grid=(nrows // rblk, nj),
        in_specs=[
            pl.BlockSpec((rblk, d), lambda i, j, be_ref, nv_ref: (i, 0)),
            pl.BlockSpec((1, d, tf), wcol),
            pl.BlockSpec((1, d, tf), wcol),
            pl.BlockSpec((1, tf, d), wrow),
        ],
        out_specs=pl.BlockSpec((rblk, d), lambda i, j, be_ref, nv_ref: (i, 0)),
        scratch_shapes=[pltpu.VMEM((rblk, d), BF16)],
    )
    return pl.pallas_call(
        _expert_kernel,
        grid_spec=grid_spec,
        out_shape=jax.ShapeDtypeStruct((nrows, d), F32),
        compiler_params=_cparams("parallel", "arbitrary"),
        name="moe_grouped_swiglu",
    )(be, nv, buf, wg, wu, wd)


def _combine_kernel(d1_ref, d2_ref, yb_hbm, h_ref, wt_ref, ng_ref, o_ref, y_ref, sem, *, t, final):
    def row_copy(slot, dst_row, src_row):
        return pltpu.make_async_copy(yb_hbm.at[pl.ds(src_row, 1)],
                                     y_ref.at[slot, pl.ds(dst_row, 1)], sem)

    def start(r, c):
        row_copy(0, r, d1_ref[r]).start()
        row_copy(1, r, d2_ref[r]).start()
        return c

    lax.fori_loop(0, t, start, 0)

    def wait(r, c):
        row_copy(0, r, d1_ref[r]).wait()
        row_copy(1, r, d2_ref[r]).wait()
        return c

    lax.fori_loop(0, t, wait, 0)

    rr = lax.broadcasted_iota(I32, (t, t), 0)
    cc = lax.broadcasted_iota(I32, (t, t), 1)
    eye = rr == cc
    w1 = jnp.sum(jnp.where(eye, wt_ref[0:1, :], 0.0), axis=1, keepdims=True)
    w2 = jnp.sum(jnp.where(eye, wt_ref[1:2, :], 0.0), axis=1, keepdims=True)
    out = h_ref[...] + w1 * y_ref[0] + w2 * y_ref[1]
    if final:
        out = _rmsnorm(out, ng_ref[...])
    o_ref[...] = out


def _combine(yb, d1, d2, h, wt, ng, t, final):
    n, d = h.shape
    smem = lambda: pl.BlockSpec((t,), lambda i: (i,), memory_space=pltpu.SMEM)
    return pl.pallas_call(
        functools.partial(_combine_kernel, t=t, final=final),
        grid=(n // t,),
        in_specs=[
            smem(), smem(),
            pl.BlockSpec(memory_space=pl.ANY),
            pl.BlockSpec((t, d), lambda i: (i, 0)),
            pl.BlockSpec((SUBLANES, t), lambda i: (0, i)),
            pl.BlockSpec((1, d), lambda i: (0, 0)),
        ],
        out_specs=pl.BlockSpec((t, d), lambda i: (i, 0)),
        out_shape=jax.ShapeDtypeStruct((n, d), F32),
        scratch_shapes=[pltpu.VMEM((2, t, d), F32), pltpu.SemaphoreType.DMA(())],
        compiler_params=_cparams("arbitrary"),
        name="moe_combine",
    )(d1, d2, yb, h, wt, ng)


def _final_norm_kernel(x_ref, g_ref, o_ref):
    o_ref[...] = _rmsnorm(x_ref[...], g_ref[...])


def _final_norm(h, g, tm):
    n, d = h.shape
    return pl.pallas_call(
        _final_norm_kernel,
        grid=(n // tm,),
        in_specs=[pl.BlockSpec((tm, d), lambda i: (i, 0)), pl.BlockSpec((1, d), lambda i: (0, 0))],
        out_specs=pl.BlockSpec((tm, d), lambda i: (i, 0)),
        out_shape=jax.ShapeDtypeStruct((n, d), F32),
        compiler_params=_cparams("parallel"),
        name="final_rmsnorm",
    )(h, g)


def _tile(n, pref):
    t = min(n, pref)
    assert n % t == 0, (n, pref)
    return t


def _mixer(h, g_mix, w_in, b_fox, b_gate, lb, hgrn_ng, w_sb, w_fox, w_hgrn, w_out, g_ffn, hn_dtype):
    b, s, d = h.shape
    n = b * s
    sbw = SB_HEADS * HEAD_DIM
    foxw = FOX_HEADS * HEAD_DIM
    hw = HGRN_HEADS * HGRN_DIM
    o_fox = 3 * sbw
    o_fb = o_fox + 3 * foxw
    o_hq = o_fb + FOX_HEADS
    o_hf, o_hi, o_hg = o_hq + hw, o_hq + 2 * hw, o_hq + 3 * hw
    o_gate = o_hq + 4 * hw
    w_main = jnp.concatenate(
        [w_in[:, :o_fb], w_in[:, o_hq:o_hf], w_in[:, o_hi:o_gate], w_in[:, o_gate:]], axis=1).astype(BF16)
    w_vt = jnp.concatenate([w_in[:, 2 * sbw:3 * sbw], w_in[:, o_fox + 2 * foxw:o_fb]], axis=1).T.astype(BF16)
    w_f = w_in[:, o_hf:o_hi].astype(BF16)
    w_fb = w_in[:, o_fb:o_hq]
    w_fbc = jnp.pad(w_fb, ((0, 0), (0, LANES - FOX_HEADS))).astype(BF16)
    w_fbr = jnp.pad(w_fb.T, ((0, SUBLANES - FOX_HEADS), (0, 0))).astype(BF16)
    bc = jnp.pad(b_fox, (0, LANES - FOX_HEADS)).reshape(1, LANES)
    br = jnp.broadcast_to(jnp.pad(b_fox, (0, SUBLANES - FOX_HEADS)).reshape(SUBLANES, 1), (SUBLANES, LANES))

    g_mix = g_mix.reshape(1, d)
    tm = _tile(s, 1024)
    proj, vt = _norm_matmul(h.reshape(n, d), g_mix, w_main, w_vt, s, tm, _tile(w_main.shape[1], 2048))
    proj = proj.reshape(b, s, -1)
    f_hgrn, cum_col, cum_row = _gate_cum(h, g_mix, w_f, w_fbc, w_fbr, bc, br, _tile(s, 1024))

    tq = _tile(s, 256)
    oa = _sb_attention(proj, vt, 0, 0, tq, tq)
    ob = _fox_attention(proj, vt, o_fox // foxw, sbw, cum_col, cum_row, tq, _tile(s, 512))
    c_hq = (o_fox + 3 * foxw) // HGRN_DIM
    oc = _hgrn(proj, f_hgrn, lb.reshape(1, hw), hgrn_ng.reshape(1, HGRN_DIM),
               c_hq, c_hq + HGRN_HEADS, c_hq + 2 * HGRN_HEADS, _tile(s, 512))

    gcol0 = (o_fox + 3 * foxw + 3 * hw) // d
    hnew, hn = _merge_out(
        h.reshape(n, d), oa.reshape(n, sbw), ob.reshape(n, foxw), oc.reshape(n, hw),
        proj.reshape(n, -1), gcol0, b_gate,
        w_sb.astype(BF16), w_fox.astype(BF16), w_hgrn.astype(BF16), w_out.astype(BF16),
        g_ffn.reshape(1, d), hn_dtype, _tile(n, 512))
    return hnew, hn


def _moe(h, hn, w_router, w_gate, w_up, w_down, final_g, final):
    n, d = h.shape
    rblk = 512
    nrows = n * TOP_K + N_EXPERTS * rblk
    nb = nrows // rblk
    nb_pad = -(-nb // LANES) * LANES
    idx, wt, cnt = _router(hn, w_router.T.astype(BF16), _tile(n, 1024))
    dest, blk = _dest(idx, cnt, rblk, nb_pad)
    d1, d2 = dest[0], dest[1]
    buf = _dispatch(hn, d1, d2, nrows, _tile(n, 256))
    yb = _experts(buf, blk[0, :nb], blk[1, :1], w_gate.astype(BF16), w_up.astype(BF16),
                  w_down.astype(BF16), rblk, 512)
    return _combine(yb, d1, d2, h, wt, final_g.reshape(1, d), _tile(n, 256), final)


def kernel(x, norm_mix_g, w_in, b_fox, b_gate, hgrn_lb_logits, hgrn_norm_g, w_branch_sb, w_branch_fox, w_branch_hgrn, w_out, norm_ffn_g, w_ffn_gate, w_ffn_up, w_ffn_down, w_router, w_exp_gate, w_exp_up, w_exp_down, final_norm_g):
    b, s, d = x.shape
    n = b * s
    depth = w_in.shape[0]
    lbs = _lower_bounds(hgrn_lb_logits)
    h = x
    for l in range(depth):
        dense = l % 2 == 0
        h2, hn = _mixer(h, norm_mix_g[l], w_in[l], b_fox[l], b_gate[l], lbs[l], hgrn_norm_g[l],
                        w_branch_sb[l], w_branch_fox[l], w_branch_hgrn[l], w_out[l], norm_ffn_g[l],
                        BF16 if dense else F32)
        last = l == depth - 1
        i = l // 2
        if dense:
            h2 = _dense_ffn(hn, h2, w_ffn_gate[i].astype(BF16), w_ffn_up[i].astype(BF16),
                            w_ffn_down[i].astype(BF16), _tile(n, 1024), 768)
            if last:
                h2 = _final_norm(h2, final_norm_g.reshape(1, d), _tile(n, 1024))
        else:
            h2 = _moe(h2, hn, w_router[i], w_exp_gate[i], w_exp_up[i], w_exp_down[i],
                      final_norm_g, last)
        h = h2.reshape(b, s, d)
    return h
```

```python
import functools

import jax
import jax.numpy as jnp
from jax import lax
from jax.experimental import pallas as pl
from jax.experimental.pallas import tpu as pltpu

F32 = jnp.float32
BF16 = jnp.bfloat16
I32 = jnp.int32

RMS_EPS = 1e-6
CHUNK = 64
SUBCHUNK = 16
SB_HEADS = 4
FOX_HEADS = 4
HGRN_HEADS = 4
HEAD_DIM = 64
HGRN_DIM = 128
N_EXPERTS = 8
TOP_K = 2

LANES = 128
SUBLANES = 8
VMEM_LIMIT = 56 * 1024 * 1024

EXP_ZERO_BELOW = -105.0
NEG_BIG = -1e30


def _cparams(*sem):
    return pltpu.CompilerParams(dimension_semantics=sem, vmem_limit_bytes=VMEM_LIMIT)


def _dot(a, b):
    return jnp.dot(a, b, preferred_element_type=F32)


def _dot_nt(a, b):
    return lax.dot_general(a, b, (((1,), (1,)), ((), ())), preferred_element_type=F32)


def _dot_tn(a, b):
    return lax.dot_general(a, b, (((0,), (0,)), ((), ())), preferred_element_type=F32)


def _split2(x):
    hi = x.astype(BF16)
    lo = (x - hi.astype(F32)).astype(BF16)
    return hi, lo


def _split3(x):
    hi = x.astype(BF16)
    r = x - hi.astype(F32)
    mid = r.astype(BF16)
    lo = (r - mid.astype(F32)).astype(BF16)
    return hi, mid, lo


def _log_sigmoid(x):
    return jnp.minimum(x, 0.0) - jnp.log(1.0 + jnp.exp(-jnp.abs(x)))


def _rmsnorm(x, g):
    ms = jnp.mean(x * x, axis=-1, keepdims=True)
    return x * lax.rsqrt(ms + RMS_EPS) * g


def _tri(n, fn, dtype=BF16):
    r = lax.broadcasted_iota(I32, (n, n), 0)
    c = lax.broadcasted_iota(I32, (n, n), 1)
    return fn(r, c).astype(dtype)


def _lb_kernel(x_ref, o_ref):
    x = x_ref[...]
    depth = x.shape[0]
    m = jnp.max(x, axis=0, keepdims=True)
    e = jnp.exp(x - m)
    sm = e / jnp.sum(e, axis=0, keepdims=True)
    run = sm[0:1]
    first = run
    o_ref[0:1, :] = run - first
    for l in range(1, depth):
        run = run + sm[l:l + 1]
        o_ref[l:l + 1, :] = run - first


def _lower_bounds(logits):
    return pl.pallas_call(
        _lb_kernel,
        out_shape=jax.ShapeDtypeStruct(logits.shape, F32),
        name="hgrn_lower_bounds",
    )(logits.astype(F32))


def _norm_matmul_kernel(x_ref, g_ref, w_ref, wvt_ref, o_ref, vt_ref, xn_ref):
    @pl.when(pl.program_id(1) == 0)
    def _():
        xn = _rmsnorm(x_ref[...], g_ref[...]).astype(BF16)
        xn_ref[...] = xn
        vt_ref[0] = _dot_nt(wvt_ref[...], xn).astype(vt_ref.dtype)

    o_ref[...] = _dot(xn_ref[...], w_ref[...]).astype(o_ref.dtype)


def _norm_matmul(x, g, w, wvt, seq, tm, tn):
    n, d = x.shape
    nout = w.shape[1]
    nv = wvt.shape[0]
    per = seq // tm
    return pl.pallas_call(
        _norm_matmul_kernel,
        grid=(n // tm, nout // tn),
        in_specs=[
            pl.BlockSpec((tm, d), lambda i, j: (i, 0)),
            pl.BlockSpec((1, d), lambda i, j: (0, 0)),
            pl.BlockSpec((d, tn), lambda i, j: (0, j)),
            pl.BlockSpec((nv, d), lambda i, j: (0, 0)),
        ],
        out_specs=[
            pl.BlockSpec((tm, tn), lambda i, j: (i, j)),
            pl.BlockSpec((1, nv, tm), lambda i, j: (i // per, 0, i % per)),
        ],
        out_shape=[
            jax.ShapeDtypeStruct((n, nout), BF16),
            jax.ShapeDtypeStruct((n // seq, nv, seq), BF16),
        ],
        scratch_shapes=[pltpu.VMEM((tm, d), BF16)],
        compiler_params=_cparams("parallel", "arbitrary"),
        name="norm_in_proj",
    )(x, g, w, wvt)


def _gate_cum_kernel(x_ref, g_ref, wf_ref, wc_ref, wr_ref, bc_ref, br_ref, lt_ref, ut_ref,
                     f_ref, cc_ref, cr_ref, carc_ref, carr_ref):
    tm = x_ref.shape[1]

    @pl.when(pl.program_id(1) == 0)
    def _():
        carc_ref[...] = jnp.zeros_like(carc_ref)
        carr_ref[...] = jnp.zeros_like(carr_ref)

    xn = _rmsnorm(x_ref[0], g_ref[...]).astype(BF16)
    f_ref[0] = _dot(xn, wf_ref[...])

    lc = _log_sigmoid(_dot(xn, wc_ref[...]) + bc_ref[...])
    lt = lt_ref[...]
    c_hi, c_mid, c_lo = _split3(lc)
    cum_c = _dot(lt, c_hi) + _dot(lt, c_mid) + _dot(lt, c_lo) + carc_ref[0:1, :]
    cc_ref[0] = cum_c
    carc_ref[...] = jnp.broadcast_to(cum_c[tm - 1:tm, :], carc_ref.shape)

    lr = _log_sigmoid(_dot_nt(wr_ref[...], xn) + br_ref[:, 0:1])
    ut = ut_ref[...]
    r_hi, r_mid, r_lo = _split3(lr)
    cum_r = _dot(r_hi, ut) + _dot(r_mid, ut) + _dot(r_lo, ut) + carr_ref[:, 0:1]
    cr_ref[0] = cum_r
    carr_ref[...] = jnp.broadcast_to(cum_r[:, tm - 1:tm], carr_ref.shape)


def _gate_cum(h, g, wf, wc, wr, bc, br, tm):
    b, s, d = h.shape
    nf = wf.shape[1]
    lt = _tri(tm, lambda r, c: c <= r)
    ut = _tri(tm, lambda r, c: r <= c)
    full = lambda shape: pl.BlockSpec(shape, lambda bi, si: (0,) * len(shape))
    return pl.pallas_call(
        _gate_cum_kernel,
        grid=(b, s // tm),
        in_specs=[
            pl.BlockSpec((1, tm, d), lambda bi, si: (bi, si, 0)),
            full((1, d)), full((d, nf)), full((d, LANES)), full((SUBLANES, d)),
            full((1, LANES)), full((SUBLANES, LANES)), full((tm, tm)), full((tm, tm)),
        ],
        out_specs=[
            pl.BlockSpec((1, tm, nf), lambda bi, si: (bi, si, 0)),
            pl.BlockSpec((1, tm, LANES), lambda bi, si: (bi, si, 0)),
            pl.BlockSpec((1, SUBLANES, tm), lambda bi, si: (bi, 0, si)),
        ],
        out_shape=[
            jax.ShapeDtypeStruct((b, s, nf), F32),
            jax.ShapeDtypeStruct((b, s, LANES), F32),
            jax.ShapeDtypeStruct((b, SUBLANES, s), F32),
        ],
        scratch_shapes=[pltpu.VMEM((SUBLANES, LANES), F32), pltpu.VMEM((SUBLANES, LANES), F32)],
        compiler_params=_cparams("parallel", "arbitrary"),
        name="gate_proj_cum",
    )(h, g, wf, wc, wr, bc, br, lt, ut)


def _masked_queries(q_ref, qm_ref, nheads, hd, scale):
    q = q_ref[0]
    lane = lax.broadcasted_iota(I32, q.shape, 1)
    qs = q * jnp.asarray(scale, q.dtype)
    for h in range(nheads):
        sel = jnp.logical_and(lane >= h * hd, lane < (h + 1) * hd)
        qm_ref[h] = jnp.where(sel, qs, jnp.zeros_like(qs))


def _sb_kernel(q_ref, k_ref, vt_ref, u_ref, o_ref, qm_ref, car_ref, acc_ref, *, tq, tk, nheads, hd, vrow0):
    i = pl.program_id(1)
    _masked_queries(q_ref, qm_ref, nheads, hd, hd ** -0.5)
    car_ref[...] = jnp.zeros_like(car_ref)
    acc_ref[...] = jnp.zeros_like(acc_ref)
    u = u_ref[...]

    def tile(j, diag):
        start = pl.multiple_of(j * tk, tk)
        kb = k_ref[0, pl.ds(start, tk), :]
        if diag:
            srow = start + lax.broadcasted_iota(I32, (tk, tq), 0)
            tcol = i * tq + lax.broadcasted_iota(I32, (tk, tq), 1)
            strict = srow < tcol
        alive = None
        for h in range(nheads):
            z = _dot_nt(kb, qm_ref[h])
            lg = -(jnp.maximum(z, 0.0) + jnp.log(1.0 + jnp.exp(-jnp.abs(z))))
            if diag:
                lg = jnp.where(strict, lg, 0.0)
            l_hi, l_lo = _split2(lg)
            suf = _dot(u, l_hi) + _dot(u, l_lo)
            carry = car_ref[h]
            w = jnp.exp(z + lg + suf + carry)
            if diag:
                w = jnp.where(strict, w, 0.0)
            vt = vt_ref[0, vrow0 + h * hd:vrow0 + (h + 1) * hd, pl.ds(start, tk)]
            acc_ref[h] += _dot(vt, w.astype(BF16))
            carry = carry + (suf[0:1, :] + lg[0:1, :])
            car_ref[h] = carry
            top = jnp.max(carry)
            alive = top if alive is None else jnp.maximum(alive, top)
        return alive

    j0 = (i * tq) // tk
    alive0 = tile(j0, True)

    def cond(c):
        j, alive = c
        return jnp.logical_and(j >= 0, alive > EXP_ZERO_BELOW)

    def body(c):
        j, _ = c
        return j - 1, tile(j, False)

    lax.while_loop(cond, body, (j0 - 1, alive0))
    o_ref[0] = acc_ref[...].reshape(nheads * hd, tq).T.astype(o_ref.dtype)


def _sb_attention(proj, vt, col0, vrow0, tq, tk):
    b, s, _ = proj.shape
    width = SB_HEADS * HEAD_DIM
    nvt = vt.shape[1]
    u = _tri(tk, lambda r, c: c > r)
    kern = functools.partial(_sb_kernel, tq=tq, tk=tk, nheads=SB_HEADS, hd=HEAD_DIM, vrow0=vrow0)
    return pl.pallas_call(
        kern,
        grid=(b, s // tq),
        in_specs=[
            pl.BlockSpec((1, tq, width), lambda bi, i: (bi, i, col0)),
            pl.BlockSpec((1, s, width), lambda bi, i: (bi, 0, col0 + 1)),
            pl.BlockSpec((1, nvt, s), lambda bi, i: (bi, 0, 0)),
            pl.BlockSpec((tk, tk), lambda bi, i: (0, 0)),
        ],
        out_specs=pl.BlockSpec((1, tq, width), lambda bi, i: (bi, i, 0)),
        out_shape=jax.ShapeDtypeStruct((b, s, width), BF16),
        scratch_shapes=[pltpu.VMEM((SB_HEADS, tq, width), BF16),
                        pltpu.VMEM((SB_HEADS, 1, tq), F32),
                        pltpu.VMEM((SB_HEADS, HEAD_DIM, tq), F32)],
        compiler_params=_cparams("parallel", "arbitrary"),
        name="stick_breaking_attention",
    )(proj, proj, vt, u)


def _fox_kernel(q_ref, k_ref, vt_ref, cc_ref, cr_ref, o_ref, qm_ref, s_ref, mx_ref, m_ref, l_ref, acc_ref,
                *, tq, tk, nheads, hd, vrow0):
    i = pl.program_id(1)
    _masked_queries(q_ref, qm_ref, nheads, hd, hd ** -0.5)
    m_ref[...] = jnp.full(m_ref.shape, NEG_BIG, F32)
    l_ref[...] = jnp.zeros_like(l_ref)
    acc_ref[...] = jnp.zeros_like(acc_ref)

    def score_pass(j, slot, diag):
        start = pl.multiple_of(j * tk, tk)
        kb = k_ref[0, pl.ds(start, tk), :]
        if diag:
            srow = start + lax.broadcasted_iota(I32, (tk, tq), 0)
            tcol = i * tq + lax.broadcasted_iota(I32, (tk, tq), 1)
            causal = srow <= tcol
        for h in range(nheads):
            cs = cc_ref[0, pl.ds(start, tk), h:h + 1]
            s = _dot_nt(kb, qm_ref[h]) - cs
            if diag:
                s = jnp.where(causal, s, NEG_BIG)
            s_ref[slot * nheads + h] = s
            mx_ref[slot * nheads + h] = jnp.max(s, axis=0, keepdims=True)

    def value_pass(j, slot):
        start = pl.multiple_of(j * tk, tk)
        for h in range(nheads):
            ct = cr_ref[0, h:h + 1, :]
            m = m_ref[h]
            m_new = jnp.maximum(m, mx_ref[slot * nheads + h] + ct)
            alpha = jnp.exp(m - m_new)
            p = jnp.exp(s_ref[slot * nheads + h] - (m_new - ct))
            l_ref[h] = alpha * l_ref[h] + jnp.sum(p, axis=0, keepdims=True)
            vt = vt_ref[0, vrow0 + h * hd:vrow0 + (h + 1) * hd, pl.ds(start, tk)]
            acc_ref[h] = alpha * acc_ref[h] + _dot(vt, p.astype(BF16))
            m_ref[h] = m_new

    j0 = (i * tq) // tk
    score_pass(j0, 0, True)

    def body(it, c):
        slot = lax.rem(it, 2)
        score_pass(j0 - 1 - it, 1 - slot, False)
        value_pass(j0 - it, slot)
        return c

    lax.fori_loop(0, j0, body, 0)
    value_pass(0, lax.rem(j0, 2))
    out = acc_ref[...] / l_ref[...]
    o_ref[0] = out.reshape(nheads * hd, tq).T.astype(o_ref.dtype)


def _fox_attention(proj, vt, col0, vrow0, cum_col, cum_row, tq, tk):
    b, s, _ = proj.shape
    width = FOX_HEADS * HEAD_DIM
    nvt = vt.shape[1]
    kern = functools.partial(_fox_kernel, tq=tq, tk=tk, nheads=FOX_HEADS, hd=HEAD_DIM, vrow0=vrow0)
    return pl.pallas_call(
        kern,
        grid=(b, s // tq),
        in_specs=[
            pl.BlockSpec((1, tq, width), lambda bi, i: (bi, i, col0)),
            pl.BlockSpec((1, s, width), lambda bi, i: (bi, 0, col0 + 1)),
            pl.BlockSpec((1, nvt, s), lambda bi, i: (bi, 0, 0)),
            pl.BlockSpec((1, s, LANES), lambda bi, i: (bi, 0, 0)),
            pl.BlockSpec((1, SUBLANES, tq), lambda bi, i: (bi, 0, i)),
        ],
        out_specs=pl.BlockSpec((1, tq, width), lambda bi, i: (bi, i, 0)),
        out_shape=jax.ShapeDtypeStruct((b, s, width), BF16),
        scratch_shapes=[pltpu.VMEM((FOX_HEADS, tq, width), BF16),
                        pltpu.VMEM((2 * FOX_HEADS, tk, tq), F32),
                        pltpu.VMEM((2 * FOX_HEADS, 1, tq), F32),
                        pltpu.VMEM((FOX_HEADS, 1, tq), F32),
                        pltpu.VMEM((FOX_HEADS, 1, tq), F32),
                        pltpu.VMEM((FOX_HEADS, HEAD_DIM, tq), F32)],
        compiler_params=_cparams("parallel", "arbitrary"),
        name="forgetting_attention",
    )(proj, proj, vt, cum_col, cum_row)


HGRN_SAFE_SPAN = 40.0


def _hgrn_kernel(q_ref, f_ref, i_ref, g_ref, lb_ref, ng_ref, lt_ref, o_ref,
                 lf_ref, kk_ref, qq_ref, b_ref, oi_ref, st_ref, *, ts, nheads, dk):
    @pl.when(pl.program_id(1) == 0)
    def _():
        st_ref[...] = jnp.zeros_like(st_ref)

    lb = lb_ref[...]
    z = f_ref[0]
    lsz = _log_sigmoid(z)
    a = jnp.log(lb)
    c = jnp.log(1.0 - lb) + lsz
    lf_ref[...] = jnp.maximum(a, c) + jnp.log(1.0 + jnp.exp(-jnp.abs(a - c)))
    kk_ref[...] = (1.0 - lb) * jnp.exp(lsz - z)
    qin = q_ref[0].astype(F32)
    qq_ref[...] = qin / (1.0 + jnp.exp(-qin))

    lt = lt_ref[...]
    nsub = CHUNK // SUBCHUNK
    rr = lax.broadcasted_iota(I32, (SUBCHUNK, SUBCHUNK), 0)
    cc = lax.broadcasted_iota(I32, (SUBCHUNK, SUBCHUNK), 1)
    causal = cc <= rr
    sub = lax.broadcasted_iota(I32, (SUBLANES, SUBCHUNK), 0)
    causal_c = (lax.broadcasted_iota(I32, (CHUNK, CHUNK), 1)
                <= lax.broadcasted_iota(I32, (CHUNK, CHUNK), 0))
    row_blk = lax.broadcasted_iota(I32, (CHUNK, dk), 0) // SUBCHUNK
    ng = ng_ref[...]

    def chunk(ci, carry):
        r0 = pl.multiple_of(ci * CHUNK, CHUNK)
        rows = pl.ds(r0, CHUNK)
        lf = lf_ref[rows, :]
        h_hi, h_mid, h_lo = _split3(lf)
        bcum = _dot(lt, h_hi) + _dot(lt, h_mid) + _dot(lt, h_lo)
        b_ref[...] = bcum
        kk = kk_ref[rows, :]
        qq = qq_ref[rows, :]
        vv = i_ref[0, rows, :]

        span = -bcum[SUBCHUNK - 1:SUBCHUNK, :]
        for bi in range(1, nsub):
            lo = bi * SUBCHUNK
            span = jnp.maximum(span, bcum[lo - 1:lo, :] - bcum[lo + SUBCHUNK - 1:lo + SUBCHUNK, :])
        safe = jnp.max(span) < HGRN_SAFE_SPAN

        def intra(h, bi):
            hs = slice(h * dk, (h + 1) * dk)
            lo = bi * SUBCHUNK
            hi = lo + SUBCHUNK
            b_i = bcum[lo:hi, hs]
            kk_i = kk[lo:hi, hs]
            groups = []
            for gi in range(SUBCHUNK // SUBLANES):
                g0 = lo + gi * SUBLANES
                qg = qq[g0:g0 + SUBLANES, hs].astype(BF16)
                dg = jnp.zeros((SUBLANES, SUBCHUNK), F32)
                for r in range(SUBLANES):
                    t = g0 + r
                    bt = b_ref[t:t + 1, hs]
                    ke = kk_i * jnp.exp(jnp.minimum(bt - b_i, 0.0))
                    res = _dot_nt(qg, ke.astype(BF16))
                    dg = jnp.where(sub == r, res, dg)
                groups.append(dg)
            dblk = jnp.where(causal, jnp.concatenate(groups, axis=0), 0.0)
            o_i = _dot(dblk.astype(BF16), vv[lo:hi, hs])
            if bi > 0:
                bref = b_ref[lo - 1:lo, hs]
                q_i = (qq[lo:hi, hs] * jnp.exp(b_i - bref)).astype(BF16)
                k_p = (kk[0:lo, hs] * jnp.exp(bref - bcum[0:lo, hs])).astype(BF16)
                o_i = o_i + _dot(_dot_nt(q_i, k_p).astype(BF16), vv[0:lo, hs])
            return o_i

        @pl.when(safe)
        def _():
            for h in range(nheads):
                hs = slice(h * dk, (h + 1) * dk)
                bh = bcum[:, hs]
                refs = [jnp.zeros((1, dk), F32)] + [b_ref[bi * SUBCHUNK - 1:bi * SUBCHUNK, hs]
                                                    for bi in range(1, nsub)]
                ref_rows = jnp.concatenate([jnp.broadcast_to(r, (SUBCHUNK, dk)) for r in refs], axis=0)
                q_stack = qq[:, hs] * jnp.exp(bh - ref_rows)
                q_big = jnp.concatenate(
                    [jnp.where(row_blk == bi, q_stack, 0.0) for bi in range(nsub)], axis=1).astype(BF16)
                k_big = jnp.concatenate(
                    [kk[:, hs] * jnp.exp(jnp.minimum(r - bh, HGRN_SAFE_SPAN)) for r in refs],
                    axis=1).astype(BF16)
                s_all = jnp.where(causal_c, _dot_nt(q_big, k_big), 0.0)
                oi_ref[:, hs] = _dot(s_all.astype(BF16), vv[:, hs])

        @pl.when(jnp.logical_not(safe))
        def _():
            for h in range(nheads):
                for bi in range(nsub):
                    oi_ref[bi * SUBCHUNK:(bi + 1) * SUBCHUNK, h * dk:(h + 1) * dk] = intra(h, bi)

        for h in range(nheads):
            hs = slice(h * dk, (h + 1) * dk)
            st = st_ref[h]
            bh = bcum[:, hs]
            qe = (qq[:, hs] * jnp.exp(bh)).astype(BF16)
            o = oi_ref[:, hs] + _dot_nt(qe, st.astype(BF16))
            b_last = bh[CHUNK - 1:CHUNK, :]
            kd = (kk[:, hs] * jnp.exp(b_last - bh)).astype(BF16)
            st_ref[h] = st * jnp.exp(b_last) + _dot_tn(vv[:, hs], kd)
            gate = g_ref[0, rows, hs].astype(F32)
            on = _rmsnorm(o, ng) * (gate / (1.0 + jnp.exp(-gate)))
            o_ref[0, rows, hs] = on.astype(o_ref.dtype)
        return carry

    lax.fori_loop(0, ts // CHUNK, chunk, 0)


def _hgrn(proj, f_hgrn, lb, ng, colq, coli, colg, ts):
    b, s, _ = proj.shape
    dk = HGRN_DIM
    hw = HGRN_HEADS * dk
    lt = _tri(CHUNK, lambda r, c: c <= r)
    kern = functools.partial(_hgrn_kernel, ts=ts, nheads=HGRN_HEADS, dk=dk)
    return pl.pallas_call(
        kern,
        grid=(b, s // ts),
        in_specs=[
            pl.BlockSpec((1, ts, hw), lambda bi, si: (bi, si, colq)),
            pl.BlockSpec((1, ts, hw), lambda bi, si: (bi, si, 0)),
            pl.BlockSpec((1, ts, hw), lambda bi, si: (bi, si, coli)),
            pl.BlockSpec((1, ts, hw), lambda bi, si: (bi, si, colg)),
            pl.BlockSpec((1, hw), lambda bi, si: (0, 0)),
            pl.BlockSpec((1, dk), lambda bi, si: (0, 0)),
            pl.BlockSpec((CHUNK, CHUNK), lambda bi, si: (0, 0)),
        ],
        out_specs=pl.BlockSpec((1, ts, hw), lambda bi, si: (bi, si, 0)),
        out_shape=jax.ShapeDtypeStruct((b, s, hw), BF16),
        scratch_shapes=[
            pltpu.VMEM((ts, hw), F32), pltpu.VMEM((ts, hw), F32), pltpu.VMEM((ts, hw), F32),
            pltpu.VMEM((CHUNK, hw), F32), pltpu.VMEM((CHUNK, hw), F32),
            pltpu.VMEM((HGRN_HEADS, dk, dk), F32),
        ],
        compiler_params=_cparams("parallel", "arbitrary"),
        name="hgrn2_recurrence",
    )(proj, f_hgrn, proj, proj, lb, ng, lt)


def _merge_kernel(h_ref, oa_ref, ob_ref, oc_ref, g0_ref, g1_ref, g2_ref, bg_ref,
                  wa_ref, wb_ref, wc_ref, wo_ref, ng_ref, hout_ref, hn_ref):
    def gate(gref, k):
        x = gref[...].astype(F32) + bg_ref[k:k + 1, :]
        return 1.0 / (1.0 + jnp.exp(-x))

    merged = (gate(g0_ref, 0) * _dot(oa_ref[...], wa_ref[...])
              + gate(g1_ref, 1) * _dot(ob_ref[...], wb_ref[...])
              + gate(g2_ref, 2) * _dot(oc_ref[...], wc_ref[...]))
    hnew = h_ref[...] + _dot(merged.astype(BF16), wo_ref[...])
    hout_ref[...] = hnew
    hn_ref[...] = _rmsnorm(hnew, ng_ref[...]).astype(hn_ref.dtype)


def _merge_out(h, oa, ob, oc, proj, gcol0, bg, wa, wb, wc, wo, ng, hn_dtype, tm):
    n, d = h.shape
    full = lambda shape: pl.BlockSpec(shape, lambda i: (0,) * len(shape))
    rows = lambda w: pl.BlockSpec((tm, w), lambda i: (i, 0))
    return pl.pallas_call(
        _merge_kernel,
        grid=(n // tm,),
        in_specs=[
            rows(d), rows(oa.shape[1]), rows(ob.shape[1]), rows(oc.shape[1]),
            pl.BlockSpec((tm, d), lambda i: (i, gcol0)),
            pl.BlockSpec((tm, d), lambda i: (i, gcol0 + 1)),
            pl.BlockSpec((tm, d), lambda i: (i, gcol0 + 2)),
            full(bg.shape), full(wa.shape), full(wb.shape), full(wc.shape), full(wo.shape),
            full((1, d)),
        ],
        out_specs=[rows(d), rows(d)],
        out_shape=[jax.ShapeDtypeStruct((n, d), F32), jax.ShapeDtypeStruct((n, d), hn_dtype)],
        compiler_params=_cparams("parallel"),
        name="merge_out_proj",
    )(h, oa, ob, oc, proj, proj, proj, bg, wa, wb, wc, wo, ng)


def _ffn_kernel(hn_ref, h_ref, wg_ref, wu_ref, wd_ref, o_ref, acc_ref):
    j = pl.program_id(1)

    @pl.when(j == 0)
    def _():
        acc_ref[...] = jnp.zeros_like(acc_ref)

    x = hn_ref[...]
    gt = _dot(x, wg_ref[...])
    up = _dot(x, wu_ref[...])
    act = (gt / (1.0 + jnp.exp(-gt))) * up
    acc_ref[...] += _dot(act.astype(BF16), wd_ref[...])

    @pl.when(j == pl.num_programs(1) - 1)
    def _():
        o_ref[...] = h_ref[...] + acc_ref[...]


def _dense_ffn(hn, h, wg, wu, wd, tm, tf):
    n, d = h.shape
    ff = wg.shape[1]
    return pl.pallas_call(
        _ffn_kernel,
        grid=(n // tm, ff // tf),
        in_specs=[
            pl.BlockSpec((tm, d), lambda i, j: (i, 0)),
            pl.BlockSpec((tm, d), lambda i, j: (i, 0)),
            pl.BlockSpec((d, tf), lambda i, j: (0, j)),
            pl.BlockSpec((d, tf), lambda i, j: (0, j)),
            pl.BlockSpec((tf, d), lambda i, j: (j, 0)),
        ],
        out_specs=pl.BlockSpec((tm, d), lambda i, j: (i, 0)),
        out_shape=jax.ShapeDtypeStruct((n, d), F32),
        scratch_shapes=[pltpu.VMEM((tm, d), F32)],
        compiler_params=_cparams("parallel", "arbitrary"),
        name="dense_swiglu",
    )(hn, h, wg, wu, wd)


def _router_kernel(hn_ref, wr_ref, u_ref, idx_ref, wt_ref, cnt_ref, car_ref):
    t = hn_ref.shape[0]

    @pl.when(pl.program_id(0) == 0)
    def _():
        car_ref[...] = jnp.zeros_like(car_ref)

    logits = _dot_nt(wr_ref[...], hn_ref[...].astype(BF16))
    eidx = lax.broadcasted_iota(I32, (N_EXPERTS, t), 0)
    m1 = jnp.max(logits, axis=0, keepdims=True)
    i1 = jnp.min(jnp.where(logits == m1, eidx, N_EXPERTS), axis=0, keepdims=True)
    sel1 = eidx == i1
    rest = jnp.where(sel1, -jnp.inf, logits)
    m2 = jnp.max(rest, axis=0, keepdims=True)
    i2 = jnp.min(jnp.where(rest == m2, eidx, N_EXPERTS), axis=0, keepdims=True)
    sel2 = eidx == i2
    e21 = jnp.exp(m2 - m1)
    w1 = 1.0 / (1.0 + e21)
    w2 = e21 / (1.0 + e21)

    cnt = jnp.where(jnp.logical_or(sel1, sel2), 1.0, 0.0)
    excl = _dot(cnt.astype(BF16), u_ref[...])
    rank = excl + car_ref[:, 0:1]
    p1 = jnp.sum(jnp.where(sel1, rank, 0.0), axis=0, keepdims=True)
    p2 = jnp.sum(jnp.where(sel2, rank, 0.0), axis=0, keepdims=True)
    total = rank[:, t - 1:t] + cnt[:, t - 1:t]
    car_ref[...] = jnp.broadcast_to(total, car_ref.shape)
    cnt_ref[...] = jnp.broadcast_to(total, cnt_ref.shape)

    zi = jnp.zeros((1, t), I32)
    idx_ref[...] = jnp.concatenate(
        [i1, i2, p1.astype(I32), p2.astype(I32), zi, zi, zi, zi], axis=0)
    zf = jnp.zeros((1, t), F32)
    wt_ref[...] = jnp.concatenate([w1, w2, zf, zf, zf, zf, zf, zf], axis=0)


def _router(hn, wr_t, t):
    n, d = hn.shape
    u = _tri(t, lambda r, c: r < c)
    return pl.pallas_call(
        _router_kernel,
        grid=(n // t,),
        in_specs=[
            pl.BlockSpec((t, d), lambda i: (i, 0)),
            pl.BlockSpec((N_EXPERTS, d), lambda i: (0, 0)),
            pl.BlockSpec((t, t), lambda i: (0, 0)),
        ],
        out_specs=[
            pl.BlockSpec((SUBLANES, t), lambda i: (0, i)),
            pl.BlockSpec((SUBLANES, t), lambda i: (0, i)),
            pl.BlockSpec((N_EXPERTS, LANES), lambda i: (0, 0)),
        ],
        out_shape=[
            jax.ShapeDtypeStruct((SUBLANES, n), I32),
            jax.ShapeDtypeStruct((SUBLANES, n), F32),
            jax.ShapeDtypeStruct((N_EXPERTS, LANES), F32),
        ],
        scratch_shapes=[pltpu.VMEM((N_EXPERTS, LANES), F32)],
        compiler_params=_cparams("arbitrary"),
        name="moe_router_top2",
    )(hn, wr_t, u)


def _dest_kernel(idx_ref, cnt_ref, dest_ref, blk_ref, *, rblk):
    n = idx_ref.shape[1]
    nb = blk_ref.shape[1]
    shift = rblk.bit_length() - 1
    assert rblk == 1 << shift
    e1 = idx_ref[0:1, :]
    e2 = idx_ref[1:2, :]
    d1 = idx_ref[2:3, :]
    d2 = idx_ref[3:4, :]
    bstart = lax.broadcasted_iota(I32, (1, nb), 1) * rblk
    bexp = jnp.zeros((1, nb), I32)
    pend = jnp.zeros((1, 1), I32)
    for e in range(N_EXPERTS):
        c = cnt_ref[e:e + 1, 0:1].astype(I32)
        padded = lax.shift_left(lax.shift_right_logical(c + (rblk - 1), shift), shift)
        pstart = pend
        pend = pend + padded
        d1 = d1 + jnp.where(e1 == e, pstart, 0)
        d2 = d2 + jnp.where(e2 == e, pstart, 0)
        bexp = bexp + jnp.where(bstart >= pend, 1, 0)
    zi = jnp.zeros((1, n), I32)
    dest_ref[...] = jnp.concatenate([d1, d2, zi, zi, zi, zi, zi, zi], axis=0)
    zb = jnp.zeros((1, nb), I32)
    nvalid = jnp.broadcast_to(lax.shift_right_logical(pend, shift), (1, nb))
    blk_ref[...] = jnp.concatenate(
        [jnp.minimum(bexp, N_EXPERTS - 1), nvalid, zb, zb, zb, zb, zb, zb], axis=0)


def _dest(idx, cnt, rblk, nb_pad):
    n = idx.shape[1]
    return pl.pallas_call(
        functools.partial(_dest_kernel, rblk=rblk),
        out_shape=[
            jax.ShapeDtypeStruct((SUBLANES, n), I32),
            jax.ShapeDtypeStruct((SUBLANES, nb_pad), I32),
        ],
        compiler_params=pltpu.CompilerParams(vmem_limit_bytes=VMEM_LIMIT),
        name="moe_slot_assign",
    )(idx, cnt)


def _dispatch_kernel(d1_ref, d2_ref, x_ref, buf_in, buf_hbm, sem, *, t):
    del buf_in

    def row_copy(src_row, dst_row):
        return pltpu.make_async_copy(x_ref.at[pl.ds(src_row, 1)], buf_hbm.at[pl.ds(dst_row, 1)], sem)

    def start(r, c):
        row_copy(r, d1_ref[r]).start()
        row_copy(r, d2_ref[r]).start()
        return c

    lax.fori_loop(0, t, start, 0)

    def wait(r, c):
        row_copy(r, d1_ref[r]).wait()
        row_copy(r, d2_ref[r]).wait()
        return c

    lax.fori_loop(0, t, wait, 0)


def _dispatch(x, d1, d2, nrows, t):
    n, d = x.shape
    buf0 = jnp.zeros((nrows, d), x.dtype)
    smem = lambda: pl.BlockSpec((t,), lambda i: (i,), memory_space=pltpu.SMEM)
    return pl.pallas_call(
        functools.partial(_dispatch_kernel, t=t),
        grid=(n // t,),
        in_specs=[smem(), smem(),
                  pl.BlockSpec((t, d), lambda i: (i, 0)), pl.BlockSpec(memory_space=pl.ANY)],
        out_specs=pl.BlockSpec(memory_space=pl.ANY),
        out_shape=jax.ShapeDtypeStruct((nrows, d), x.dtype),
        scratch_shapes=[pltpu.SemaphoreType.DMA(())],
        input_output_aliases={3: 0},
        compiler_params=_cparams("arbitrary"),
        name="moe_dispatch_rows",
    )(d1, d2, x, buf0)


def _expert_kernel(be_ref, nv_ref, x_ref, wg_ref, wu_ref, wd_ref, o_ref, xb_ref):
    i = pl.program_id(0)
    j = pl.program_id(1)

    @pl.when(j == 0)
    def _():
        o_ref[...] = jnp.zeros_like(o_ref)
        xb_ref[...] = x_ref[...].astype(BF16)

    @pl.when(i < nv_ref[0])
    def _():
        x = xb_ref[...]
        gt = _dot(x, wg_ref[0])
        up = _dot(x, wu_ref[0])
        act = (gt / (1.0 + jnp.exp(-gt))) * up
        o_ref[...] += _dot(act.astype(BF16), wd_ref[0])


def _experts(buf, be, nv, wg, wu, wd, rblk, tf):
    nrows, d = buf.shape
    ff = wg.shape[2]
    nj = ff // tf

    def wcol(i, j, be_ref, nv_ref):
        return (be_ref[i], 0, jnp.where(i < nv_ref[0], j, nj - 1))

    def wrow(i, j, be_ref, nv_ref):
        return (be_ref[i], jnp.where(i < nv_ref[0], j, nj - 1), 0)

    grid_spec = pltpu.PrefetchScalarGridSpec(
        num_scalar_prefetch=2,
        grid=(nrows // rblk, nj),
        in_specs=[
            pl.BlockSpec((rblk, d), lambda i, j, be_ref, nv_ref: (i, 0)),
            pl.BlockSpec((1, d, tf), wcol),
            pl.BlockSpec((1, d, tf), wcol),
            pl.BlockSpec((1, tf, d), wrow),
        ],
        out_specs=pl.BlockSpec((rblk, d), lambda i, j, be_ref, nv_ref: (i, 0)),
        scratch_shapes=[pltpu.VMEM((rblk, d), BF16)],
    )
    return pl.pallas_call(
        _expert_kernel,
        grid_spec=grid_spec,
        out_shape=jax.ShapeDtypeStruct((nrows, d), F32),
        compiler_params=_cparams("parallel", "arbitrary"),
        name="moe_grouped_swiglu",
    )(be, nv, buf, wg, wu, wd)


def _combine_kernel(d1_ref, d2_ref, yb_hbm, h_ref, wt_ref, ng_ref, o_ref, y_ref, sem, *, t, final):
    def row_copy(slot, dst_row, src_row):
        return pltpu.make_async_copy(yb_hbm.at[pl.ds(src_row, 1)],
                                     y_ref.at[slot, pl.ds(dst_row, 1)], sem)

    def start(r, c):
        row_copy(0, r, d1_ref[r]).start()
        row_copy(1, r, d2_ref[r]).start()
        return c

    lax.fori_loop(0, t, start, 0)

    def wait(r, c):
        row_copy(0, r, d1_ref[r]).wait()
        row_copy(1, r, d2_ref[r]).wait()
        return c

    lax.fori_loop(0, t, wait, 0)

    rr = lax.broadcasted_iota(I32, (t, t), 0)
    cc = lax.broadcasted_iota(I32, (t, t), 1)
    eye = rr == cc
    w1 = jnp.sum(jnp.where(eye, wt_ref[0:1, :], 0.0), axis=1, keepdims=True)
    w2 = jnp.sum(jnp.where(eye, wt_ref[1:2, :], 0.0), axis=1, keepdims=True)
    out = h_ref[...] + w1 * y_ref[0] + w2 * y_ref[1]
    if final:
        out = _rmsnorm(out, ng_ref[...])
    o_ref[...] = out


def _combine(yb, d1, d2, h, wt, ng, t, final):
    n, d = h.shape
    smem = lambda: pl.BlockSpec((t,), lambda i: (i,), memory_space=pltpu.SMEM)
    return pl.pallas_call(
        functools.partial(_combine_kernel, t=t, final=final),
        grid=(n // t,),
        in_specs=[
            smem(), smem(),
            pl.BlockSpec(memory_space=pl.ANY),
            pl.BlockSpec((t, d), lambda i: (i, 0)),
            pl.BlockSpec((SUBLANES, t), lambda i: (0, i)),
            pl.BlockSpec((1, d), lambda i: (0, 0)),
        ],
        out_specs=pl.BlockSpec((t, d), lambda i: (i, 0)),
        out_shape=jax.ShapeDtypeStruct((n, d), F32),
        scratch_shapes=[pltpu.VMEM((2, t, d), F32), pltpu.SemaphoreType.DMA(())],
        compiler_params=_cparams("arbitrary"),
        name="moe_combine",
    )(d1, d2, yb, h, wt, ng)


def _final_norm_kernel(x_ref, g_ref, o_ref):
    o_ref[...] = _rmsnorm(x_ref[...], g_ref[...])


def _final_norm(h, g, tm):
    n, d = h.shape
    return pl.pallas_call(
        _final_norm_kernel,
        grid=(n // tm,),
        in_specs=[pl.BlockSpec((tm, d), lambda i: (i, 0)), pl.BlockSpec((1, d), lambda i: (0, 0))],
        out_specs=pl.BlockSpec((tm, d), lambda i: (i, 0)),
        out_shape=jax.ShapeDtypeStruct((n, d), F32),
        compiler_params=_cparams("parallel"),
        name="final_rmsnorm",
    )(h, g)


def _tile(n, pref):
    t = min(n, pref)
    assert n % t == 0, (n, pref)
    return t


def _mixer(h, g_mix, w_in, b_fox, b_gate, lb, hgrn_ng, w_sb, w_fox, w_hgrn, w_out, g_ffn, hn_dtype):
    b, s, d = h.shape
    n = b * s
    sbw = SB_HEADS * HEAD_DIM
    foxw = FOX_HEADS * HEAD_DIM
    hw = HGRN_HEADS * HGRN_DIM
    o_fox = 3 * sbw
    o_fb = o_fox + 3 * foxw
    o_hq = o_fb + FOX_HEADS
    o_hf, o_hi, o_hg = o_hq + hw, o_hq + 2 * hw, o_hq + 3 * hw
    o_gate = o_hq + 4 * hw
    w_main = jnp.concatenate(
        [w_in[:, :o_fb], w_in[:, o_hq:o_hf], w_in[:, o_hi:o_gate], w_in[:, o_gate:]], axis=1).astype(BF16)
    w_vt = jnp.concatenate([w_in[:, 2 * sbw:3 * sbw], w_in[:, o_fox + 2 * foxw:o_fb]], axis=1).T.astype(BF16)
    w_f = w_in[:, o_hf:o_hi].astype(BF16)
    w_fb = w_in[:, o_fb:o_hq]
    w_fbc = jnp.pad(w_fb, ((0, 0), (0, LANES - FOX_HEADS))).astype(BF16)
    w_fbr = jnp.pad(w_fb.T, ((0, SUBLANES - FOX_HEADS), (0, 0))).astype(BF16)
    bc = jnp.pad(b_fox, (0, LANES - FOX_HEADS)).reshape(1, LANES)
    br = jnp.broadcast_to(jnp.pad(b_fox, (0, SUBLANES - FOX_HEADS)).reshape(SUBLANES, 1), (SUBLANES, LANES))

    g_mix = g_mix.reshape(1, d)
    tm = _tile(s, 1024)
    proj, vt = _norm_matmul(h.reshape(n, d), g_mix, w_main, w_vt, s, tm, _tile(w_main.shape[1], 2048))
    proj = proj.reshape(b, s, -1)
    f_hgrn, cum_col, cum_row = _gate_cum(h, g_mix, w_f, w_fbc, w_fbr, bc, br, _tile(s, 1024))

    tq = _tile(s, 256)
    oa = _sb_attention(proj, vt, 0, 0, tq, tq)
    ob = _fox_attention(proj, vt, o_fox // foxw, sbw, cum_col, cum_row, tq, _tile(s, 512))
    c_hq = (o_fox + 3 * foxw) // hw
    oc = _hgrn(proj, f_hgrn, lb.reshape(1, hw), hgrn_ng.reshape(1, HGRN_DIM),
               c_hq, c_hq + 1, c_hq + 2, _tile(s, 512))

    gcol0 = (o_fox + 3 * foxw + 3 * hw) // d
    hnew, hn = _merge_out(
        h.reshape(n, d), oa.reshape(n, sbw), ob.reshape(n, foxw), oc.reshape(n, hw),
        proj.reshape(n, -1), gcol0, b_gate,
        w_sb.astype(BF16), w_fox.astype(BF16), w_hgrn.astype(BF16), w_out.astype(BF16),
        g_ffn.reshape(1, d), hn_dtype, _tile(n, 512))
    return hnew, hn


def _moe(h, hn, w_router, w_gate, w_up, w_down, final_g, final):
    n, d = h.shape
    rblk = 512
    nrows = n * TOP_K + N_EXPERTS * rblk
    nb = nrows // rblk
    nb_pad = -(-nb // LANES) * LANES
    idx, wt, cnt = _router(hn, w_router.T.astype(BF16), _tile(n, 1024))
    dest, blk = _dest(idx, cnt, rblk, nb_pad)
    d1, d2 = dest[0], dest[1]
    buf = _dispatch(hn, d1, d2, nrows, _tile(n, 256))
    yb = _experts(buf, blk[0, :nb], blk[1, :1], w_gate.astype(BF16), w_up.astype(BF16),
                  w_down.astype(BF16), rblk, w_gate.shape[2] // 2)
    return _combine(yb, d1, d2, h, wt, final_g.reshape(1, d), _tile(n, 256), final)


def kernel(x, norm_mix_g, w_in, b_fox, b_gate, hgrn_lb_logits, hgrn_norm_g, w_branch_sb, w_branch_fox, w_branch_hgrn, w_out, norm_ffn_g, w_ffn_gate, w_ffn_up, w_ffn_down, w_router, w_exp_gate, w_exp_up, w_exp_down, final_norm_g):
    b, s, d = x.shape
    n = b * s
    depth = w_in.shape[0]
    lbs = _lower_bounds(hgrn_lb_logits)
    h = x
    for l in range(depth):
        dense = l % 2 == 0
        h2, hn = _mixer(h, norm_mix_g[l], w_in[l], b_fox[l], b_gate[l], lbs[l], hgrn_norm_g[l],
                        w_branch_sb[l], w_branch_fox[l], w_branch_hgrn[l], w_out[l], norm_ffn_g[l],
                        BF16 if dense else F32)
        last = l == depth - 1
        i = l // 2
        if dense:
            h2 = _dense_ffn(hn, h2, w_ffn_gate[i].astype(BF16), w_ffn_up[i].astype(BF16),
                            w_ffn_down[i].astype(BF16), _tile(n, 1024), 768)
            if last:
                h2 = _final_norm(h2, final_norm_g.reshape(1, d), _tile(n, 1024))
        else:
            h2 = _moe(h2, hn, w_router[i], w_exp_gate[i], w_exp_up[i], w_exp_down[i],
                      final_norm_g, last)
        h = h2.reshape(b, s, d)
    return h
```

```python
import functools

import jax
import jax.numpy as jnp
from jax import lax
from jax.experimental import pallas as pl
from jax.experimental.pallas import tpu as pltpu

F32 = jnp.float32
BF16 = jnp.bfloat16
I32 = jnp.int32

RMS_EPS = 1e-6
CHUNK = 64
SUBCHUNK = 16
SB_HEADS = 4
FOX_HEADS = 4
HGRN_HEADS = 4
HEAD_DIM = 64
HGRN_DIM = 128
N_EXPERTS = 8
TOP_K = 2

LANES = 128
SUBLANES = 8
VMEM_LIMIT = 56 * 1024 * 1024

EXP_ZERO_BELOW = -105.0
NEG_BIG = -1e30
NORM_SLACK = 1.0 + 2.0 ** -7
DMA_LOOP_UNROLL = 8


def _cparams(*sem):
    return pltpu.CompilerParams(dimension_semantics=sem, vmem_limit_bytes=VMEM_LIMIT)


def _dot(a, b):
    return jnp.dot(a, b, preferred_element_type=F32)


def _dot_nt(a, b):
    return lax.dot_general(a, b, (((1,), (1,)), ((), ())), preferred_element_type=F32)


def _dot_tn(a, b):
    return lax.dot_general(a, b, (((0,), (0,)), ((), ())), preferred_element_type=F32)


def _split2(x):
    hi = x.astype(BF16)
    lo = (x - hi.astype(F32)).astype(BF16)
    return hi, lo


def _split3(x):
    hi = x.astype(BF16)
    r = x - hi.astype(F32)
    mid = r.astype(BF16)
    lo = (r - mid.astype(F32)).astype(BF16)
    return hi, mid, lo


def _log_sigmoid(x):
    return jnp.minimum(x, 0.0) - jnp.log(1.0 + jnp.exp(-jnp.abs(x)))


def _rmsnorm(x, g):
    ms = jnp.mean(x * x, axis=-1, keepdims=True)
    return x * lax.rsqrt(ms + RMS_EPS) * g


def _tri(n, fn, dtype=BF16):
    r = lax.broadcasted_iota(I32, (n, n), 0)
    c = lax.broadcasted_iota(I32, (n, n), 1)
    return fn(r, c).astype(dtype)


def _lb_kernel(x_ref, o_ref):
    x = x_ref[...]
    depth = x.shape[0]
    m = jnp.max(x, axis=0, keepdims=True)
    e = jnp.exp(x - m)
    sm = e / jnp.sum(e, axis=0, keepdims=True)
    run = sm[0:1]
    first = run
    o_ref[0:1, :] = run - first
    for l in range(1, depth):
        run = run + sm[l:l + 1]
        o_ref[l:l + 1, :] = run - first


def _lower_bounds(logits):
    return pl.pallas_call(
        _lb_kernel,
        out_shape=jax.ShapeDtypeStruct(logits.shape, F32),
        name="hgrn_lower_bounds",
    )(logits.astype(F32))


def _norm_matmul_kernel(x_ref, g_ref, w_ref, wvt_ref, o_ref, vt_ref, xn_ref):
    @pl.when(pl.program_id(1) == 0)
    def _():
        xn = _rmsnorm(x_ref[...], g_ref[...]).astype(BF16)
        xn_ref[...] = xn
        vt_ref[0] = _dot_nt(wvt_ref[...], xn).astype(vt_ref.dtype)

    o_ref[...] = _dot(xn_ref[...], w_ref[...]).astype(o_ref.dtype)


def _norm_matmul(x, g, w, wvt, seq, tm, tn):
    n, d = x.shape
    nout = w.shape[1]
    nv = wvt.shape[0]
    per = seq // tm
    return pl.pallas_call(
        _norm_matmul_kernel,
        grid=(n // tm, nout // tn),
        in_specs=[
            pl.BlockSpec((tm, d), lambda i, j: (i, 0)),
            pl.BlockSpec((1, d), lambda i, j: (0, 0)),
            pl.BlockSpec((d, tn), lambda i, j: (0, j)),
            pl.BlockSpec((nv, d), lambda i, j: (0, 0)),
        ],
        out_specs=[
            pl.BlockSpec((tm, tn), lambda i, j: (i, j)),
            pl.BlockSpec((1, nv, tm), lambda i, j: (i // per, 0, i % per)),
        ],
        out_shape=[
            jax.ShapeDtypeStruct((n, nout), BF16),
            jax.ShapeDtypeStruct((n // seq, nv, seq), BF16),
        ],
        scratch_shapes=[pltpu.VMEM((tm, d), BF16)],
        compiler_params=_cparams("parallel", "arbitrary"),
        name="norm_in_proj",
    )(x, g, w, wvt)


def _gate_cum_kernel(x_ref, g_ref, wf_ref, wc_ref, wr_ref, bc_ref, br_ref, lt_ref, ut_ref,
                     f_ref, cc_ref, cr_ref, carc_ref, carr_ref):
    tm = x_ref.shape[1]

    @pl.when(pl.program_id(1) == 0)
    def _():
        carc_ref[...] = jnp.zeros_like(carc_ref)
        carr_ref[...] = jnp.zeros_like(carr_ref)

    xn = _rmsnorm(x_ref[0], g_ref[...]).astype(BF16)
    f_ref[0] = _dot(xn, wf_ref[...])

    lc = _log_sigmoid(_dot(xn, wc_ref[...]) + bc_ref[...])
    lt = lt_ref[...]
    c_hi, c_mid, c_lo = _split3(lc)
    cum_c = _dot(lt, c_hi) + _dot(lt, c_mid) + _dot(lt, c_lo) + carc_ref[0:1, :]
    cc_ref[0] = cum_c
    carc_ref[...] = jnp.broadcast_to(cum_c[tm - 1:tm, :], carc_ref.shape)

    lr = _log_sigmoid(_dot_nt(wr_ref[...], xn) + br_ref[:, 0:1])
    ut = ut_ref[...]
    r_hi, r_mid, r_lo = _split3(lr)
    cum_r = _dot(r_hi, ut) + _dot(r_mid, ut) + _dot(r_lo, ut) + carr_ref[:, 0:1]
    cr_ref[0] = cum_r
    carr_ref[...] = jnp.broadcast_to(cum_r[:, tm - 1:tm], carr_ref.shape)


def _gate_cum(h, g, wf, wc, wr, bc, br, tm):
    b, s, d = h.shape
    nf = wf.shape[1]
    lt = _tri(tm, lambda r, c: c <= r)
    ut = _tri(tm, lambda r, c: r <= c)
    full = lambda shape: pl.BlockSpec(shape, lambda bi, si: (0,) * len(shape))
    return pl.pallas_call(
        _gate_cum_kernel,
        grid=(b, s // tm),
        in_specs=[
            pl.BlockSpec((1, tm, d), lambda bi, si: (bi, si, 0)),
            full((1, d)), full((d, nf)), full((d, LANES)), full((SUBLANES, d)),
            full((1, LANES)), full((SUBLANES, LANES)), full((tm, tm)), full((tm, tm)),
        ],
        out_specs=[
            pl.BlockSpec((1, tm, nf), lambda bi, si: (bi, si, 0)),
            pl.BlockSpec((1, tm, LANES), lambda bi, si: (bi, si, 0)),
            pl.BlockSpec((1, SUBLANES, tm), lambda bi, si: (bi, 0, si)),
        ],
        out_shape=[
            jax.ShapeDtypeStruct((b, s, nf), F32),
            jax.ShapeDtypeStruct((b, s, LANES), F32),
            jax.ShapeDtypeStruct((b, SUBLANES, s), F32),
        ],
        scratch_shapes=[pltpu.VMEM((SUBLANES, LANES), F32), pltpu.VMEM((SUBLANES, LANES), F32)],
        compiler_params=_cparams("parallel", "arbitrary"),
        name="gate_proj_cum",
    )(h, g, wf, wc, wr, bc, br, lt, ut)


def _masked_queries(q_ref, qm_ref, nheads, hd, scale):
    q = q_ref[0]
    lane = lax.broadcasted_iota(I32, q.shape, 1)
    qs = q * jnp.asarray(scale, q.dtype)
    for h in range(nheads):
        sel = jnp.logical_and(lane >= h * hd, lane < (h + 1) * hd)
        qm_ref[h] = jnp.where(sel, qs, jnp.zeros_like(qs))


def _sb_kernel(q_ref, k_ref, vt_ref, u_ref, o_ref, qm_ref, car_ref, acc_ref, *, tq, tk, nheads, hd, vrow0):
    i = pl.program_id(1)
    _masked_queries(q_ref, qm_ref, nheads, hd, hd ** -0.5)
    car_ref[...] = jnp.zeros_like(car_ref)
    acc_ref[...] = jnp.zeros_like(acc_ref)
    u = u_ref[...]

    def tile(j, diag):
        start = pl.multiple_of(j * tk, tk)
        kb = k_ref[0, pl.ds(start, tk), :]
        if diag:
            srow = start + lax.broadcasted_iota(I32, (tk, tq), 0)
            tcol = i * tq + lax.broadcasted_iota(I32, (tk, tq), 1)
            strict = srow < tcol
        alive = None
        for h in range(nheads):
            z = _dot_nt(kb, qm_ref[h])
            lg = -(jnp.maximum(z, 0.0) + jnp.log(1.0 + jnp.exp(-jnp.abs(z))))
            if diag:
                lg = jnp.where(strict, lg, 0.0)
            l_hi, l_lo = _split2(lg)
            suf = _dot(u, l_hi) + _dot(u, l_lo)
            carry = car_ref[h]
            w = jnp.exp(z + lg + suf + carry)
            if diag:
                w = jnp.where(strict, w, 0.0)
            vt = vt_ref[0, vrow0 + h * hd:vrow0 + (h + 1) * hd, pl.ds(start, tk)]
            acc_ref[h] += _dot(vt, w.astype(BF16))
            carry = carry + (suf[0:1, :] + lg[0:1, :])
            car_ref[h] = carry
            top = jnp.max(carry)
            alive = top if alive is None else jnp.maximum(alive, top)
        return alive

    j0 = (i * tq) // tk
    alive0 = tile(j0, True)

    def cond(c):
        j, alive = c
        return jnp.logical_and(j >= 0, alive > EXP_ZERO_BELOW)

    def body(c):
        j, _ = c
        return j - 1, tile(j, False)

    lax.while_loop(cond, body, (j0 - 1, alive0))
    o_ref[0] = acc_ref[...].reshape(nheads * hd, tq).T.astype(o_ref.dtype)


def _sb_attention(proj, vt, col0, vrow0, tq, tk):
    b, s, _ = proj.shape
    width = SB_HEADS * HEAD_DIM
    nvt = vt.shape[1]
    u = _tri(tk, lambda r, c: c > r)
    kern = functools.partial(_sb_kernel, tq=tq, tk=tk, nheads=SB_HEADS, hd=HEAD_DIM, vrow0=vrow0)
    return pl.pallas_call(
        kern,
        grid=(b, s // tq),
        in_specs=[
            pl.BlockSpec((1, tq, width), lambda bi, i: (bi, i, col0)),
            pl.BlockSpec((1, s, width), lambda bi, i: (bi, 0, col0 + 1)),
            pl.BlockSpec((1, nvt, s), lambda bi, i: (bi, 0, 0)),
            pl.BlockSpec((tk, tk), lambda bi, i: (0, 0)),
        ],
        out_specs=pl.BlockSpec((1, tq, width), lambda bi, i: (bi, i, 0)),
        out_shape=jax.ShapeDtypeStruct((b, s, width), BF16),
        scratch_shapes=[pltpu.VMEM((SB_HEADS, tq, width), BF16),
                        pltpu.VMEM((SB_HEADS, 1, tq), F32),
                        pltpu.VMEM((SB_HEADS, HEAD_DIM, tq), F32)],
        compiler_params=_cparams("parallel", "arbitrary"),
        name="stick_breaking_attention",
    )(proj, proj, vt, u)


def _fox_kernel(q_ref, k_ref, vt_ref, cc_ref, cr_ref, kmax_ref, cend_ref, et_ref, o_ref,
                qm_ref, s_ref, mx_ref, m_ref, l_ref, acc_ref, qn_ref, *, tq, tk, nheads, hd, vrow0):
    i = pl.program_id(1)
    _masked_queries(q_ref, qm_ref, nheads, hd, hd ** -0.5)
    m_ref[...] = jnp.full(m_ref.shape, NEG_BIG, F32)
    l_ref[...] = jnp.zeros_like(l_ref)
    acc_ref[...] = jnp.zeros_like(acc_ref)

    def score_pass(j, slot, diag):
        start = pl.multiple_of(j * tk, tk)
        kb = k_ref[0, pl.ds(start, tk), :]
        if diag:
            srow = start + lax.broadcasted_iota(I32, (tk, tq), 0)
            tcol = i * tq + lax.broadcasted_iota(I32, (tk, tq), 1)
            causal = srow <= tcol
        for h in range(nheads):
            cs = cc_ref[0, pl.ds(start, tk), h:h + 1]
            s = _dot_nt(kb, qm_ref[h]) - cs
            if diag:
                s = jnp.where(causal, s, NEG_BIG)
            s_ref[slot * nheads + h] = s
            mx_ref[slot * nheads + h] = jnp.max(s, axis=0, keepdims=True)

    def value_pass(j, slot):
        start = pl.multiple_of(j * tk, tk)
        for h in range(nheads):
            ct = cr_ref[0, h:h + 1, :]
            m = m_ref[h]
            m_new = jnp.maximum(m, mx_ref[slot * nheads + h] + ct)
            alpha = jnp.exp(m - m_new)
            p = jnp.exp(s_ref[slot * nheads + h] - (m_new - ct))
            l_ref[h] = alpha * l_ref[h] + jnp.sum(p, axis=0, keepdims=True)
            vt = vt_ref[0, vrow0 + h * hd:vrow0 + (h + 1) * hd, pl.ds(start, tk)]
            acc_ref[h] = alpha * acc_ref[h] + _dot(vt, p.astype(BF16))
            m_ref[h] = m_new

    qf = q_ref[0].astype(F32) * (hd ** -0.5)
    sq_hi, sq_lo = _split2(qf * qf)
    et = et_ref[...]
    qn_ref[...] = jnp.sqrt(_dot_nt(et, sq_hi) + _dot_nt(et, sq_lo)) * NORM_SLACK

    def bound_below(j):
        sel = lax.broadcasted_iota(I32, (SUBLANES, LANES), 1) == j
        kmax = jnp.max(jnp.where(sel, kmax_ref[0], 0.0), axis=1, keepdims=True)
        cend = jnp.sum(jnp.where(sel, cend_ref[0], 0.0), axis=1, keepdims=True)
        best = None
        for h in range(nheads):
            b = (qn_ref[h:h + 1, :] * kmax[h:h + 1, :] + cr_ref[0, h:h + 1, :]
                 - cend[h:h + 1, :] - m_ref[h])
            best = b if best is None else jnp.maximum(best, b)
        return jnp.max(best)

    j0 = (i * tq) // tk
    score_pass(j0, 0, True)

    def cond(c):
        it, alive = c
        return jnp.logical_and(it < j0, alive > EXP_ZERO_BELOW)

    def body(c):
        it, _ = c
        slot = lax.rem(it, 2)
        score_pass(j0 - 1 - it, 1 - slot, False)
        value_pass(j0 - it, slot)
        return it + 1, bound_below(j0 - 2 - it)

    done, _ = lax.while_loop(cond, body, (jnp.int32(0), jnp.float32(0.0)))
    value_pass(j0 - done, lax.rem(done, 2))
    out = acc_ref[...] / l_ref[...]
    o_ref[0] = out.reshape(nheads * hd, tq).T.astype(o_ref.dtype)


def _head_selector(nheads, hd):
    r = lax.broadcasted_iota(I32, (SUBLANES, nheads * hd), 0)
    c = lax.broadcasted_iota(I32, (SUBLANES, nheads * hd), 1)
    return (c // hd == r).astype(BF16)


def _key_bounds_kernel(k_ref, et_ref, o_ref, run_ref):
    j = pl.program_id(1)

    @pl.when(j == 0)
    def _():
        run_ref[...] = jnp.zeros_like(run_ref)
        o_ref[...] = jnp.zeros_like(o_ref)

    kf = k_ref[0].astype(F32)
    sq_hi, sq_lo = _split2(kf * kf)
    et = et_ref[...]
    n2 = _dot_nt(et, sq_hi) + _dot_nt(et, sq_lo)
    tile_max = jnp.sqrt(jnp.max(n2, axis=1, keepdims=True)) * NORM_SLACK
    run = jnp.maximum(run_ref[...], tile_max)
    run_ref[...] = run
    lane = lax.broadcasted_iota(I32, (SUBLANES, LANES), 1)
    o_ref[0] = jnp.where(lane == j, run, o_ref[0])


def _key_bounds(proj, colk, tk):
    b, s, _ = proj.shape
    width = FOX_HEADS * HEAD_DIM
    assert s // tk <= LANES
    return pl.pallas_call(
        _key_bounds_kernel,
        grid=(b, s // tk),
        in_specs=[
            pl.BlockSpec((1, tk, width), lambda bi, j: (bi, j, colk)),
            pl.BlockSpec((SUBLANES, width), lambda bi, j: (0, 0)),
        ],
        out_specs=pl.BlockSpec((1, SUBLANES, LANES), lambda bi, j: (bi, 0, 0)),
        out_shape=jax.ShapeDtypeStruct((b, SUBLANES, LANES), F32),
        scratch_shapes=[pltpu.VMEM((SUBLANES, LANES), F32)],
        compiler_params=_cparams("parallel", "arbitrary"),
        name="fox_key_norm_bounds",
    )(proj, _head_selector(FOX_HEADS, HEAD_DIM))


def _fox_attention(proj, vt, col0, vrow0, cum_col, cum_row, tq, tk):
    b, s, _ = proj.shape
    width = FOX_HEADS * HEAD_DIM
    nvt = vt.shape[1]
    kmax = _key_bounds(proj, col0 + 1, tk)
    cend = jnp.pad(cum_row[:, :, tk - 1::tk], ((0, 0), (0, 0), (0, LANES - s // tk)))
    kern = functools.partial(_fox_kernel, tq=tq, tk=tk, nheads=FOX_HEADS, hd=HEAD_DIM, vrow0=vrow0)
    return pl.pallas_call(
        kern,
        grid=(b, s // tq),
        in_specs=[
            pl.BlockSpec((1, tq, width), lambda bi, i: (bi, i, col0)),
            pl.BlockSpec((1, s, width), lambda bi, i: (bi, 0, col0 + 1)),
            pl.BlockSpec((1, nvt, s), lambda bi, i: (bi, 0, 0)),
            pl.BlockSpec((1, s, LANES), lambda bi, i: (bi, 0, 0)),
            pl.BlockSpec((1, SUBLANES, tq), lambda bi, i: (bi, 0, i)),
            pl.BlockSpec((1, SUBLANES, LANES), lambda bi, i: (bi, 0, 0)),
            pl.BlockSpec((1, SUBLANES, LANES), lambda bi, i: (bi, 0, 0)),
            pl.BlockSpec((SUBLANES, width), lambda bi, i: (0, 0)),
        ],
        out_specs=pl.BlockSpec((1, tq, width), lambda bi, i: (bi, i, 0)),
        out_shape=jax.ShapeDtypeStruct((b, s, width), BF16),
        scratch_shapes=[pltpu.VMEM((FOX_HEADS, tq, width), BF16),
                        pltpu.VMEM((2 * FOX_HEADS, tk, tq), F32),
                        pltpu.VMEM((2 * FOX_HEADS, 1, tq), F32),
                        pltpu.VMEM((FOX_HEADS, 1, tq), F32),
                        pltpu.VMEM((FOX_HEADS, 1, tq), F32),
                        pltpu.VMEM((FOX_HEADS, HEAD_DIM, tq), F32),
                        pltpu.VMEM((SUBLANES, tq), F32)],
        compiler_params=_cparams("parallel", "arbitrary"),
        name="forgetting_attention",
    )(proj, proj, vt, cum_col, cum_row, kmax, cend, _head_selector(FOX_HEADS, HEAD_DIM))


HGRN_SAFE_SPAN = 40.0


def _hgrn_kernel(q_ref, f_ref, i_ref, g_ref, lb_ref, ng_ref, lt_ref, o_ref,
                 lf_ref, kk_ref, qq_ref, b_ref, oi_ref, st_ref, *, ts, nheads, dk):
    @pl.when(pl.program_id(1) == 0)
    def _():
        st_ref[...] = jnp.zeros_like(st_ref)

    lb = lb_ref[...]
    z = f_ref[0]
    lsz = _log_sigmoid(z)
    a = jnp.log(lb)
    c = jnp.log(1.0 - lb) + lsz
    lf_ref[...] = jnp.maximum(a, c) + jnp.log(1.0 + jnp.exp(-jnp.abs(a - c)))
    kk_ref[...] = (1.0 - lb) * jnp.exp(lsz - z)
    qin = q_ref[0].astype(F32)
    qq_ref[...] = qin / (1.0 + jnp.exp(-qin))

    lt = lt_ref[...]
    nsub = CHUNK // SUBCHUNK
    rr = lax.broadcasted_iota(I32, (SUBCHUNK, SUBCHUNK), 0)
    cc = lax.broadcasted_iota(I32, (SUBCHUNK, SUBCHUNK), 1)
    causal = cc <= rr
    sub = lax.broadcasted_iota(I32, (SUBLANES, SUBCHUNK), 0)
    causal_c = (lax.broadcasted_iota(I32, (CHUNK, CHUNK), 1)
                <= lax.broadcasted_iota(I32, (CHUNK, CHUNK), 0))
    row_blk = lax.broadcasted_iota(I32, (CHUNK, dk), 0) // SUBCHUNK
    ng = ng_ref[...]

    def chunk(ci, carry):
        r0 = pl.multiple_of(ci * CHUNK, CHUNK)
        rows = pl.ds(r0, CHUNK)
        lf = lf_ref[rows, :]
        h_hi, h_mid, h_lo = _split3(lf)
        bcum = _dot(lt, h_hi) + _dot(lt, h_mid) + _dot(lt, h_lo)
        b_ref[...] = bcum
        kk = kk_ref[rows, :]
        qq = qq_ref[rows, :]
        vv = i_ref[0, rows, :]

        span = -bcum[SUBCHUNK - 1:SUBCHUNK, :]
        for bi in range(1, nsub):
            lo = bi * SUBCHUNK
            span = jnp.maximum(span, bcum[lo - 1:lo, :] - bcum[lo + SUBCHUNK - 1:lo + SUBCHUNK, :])
        safe = jnp.max(span) < HGRN_SAFE_SPAN

        def intra(h, bi):
            hs = slice(h * dk, (h + 1) * dk)
            lo = bi * SUBCHUNK
            hi = lo + SUBCHUNK
            b_i = bcum[lo:hi, hs]
            kk_i = kk[lo:hi, hs]
            groups = []
            for gi in range(SUBCHUNK // SUBLANES):
                g0 = lo + gi * SUBLANES
                qg = qq[g0:g0 + SUBLANES, hs].astype(BF16)
                dg = jnp.zeros((SUBLANES, SUBCHUNK), F32)
                for r in range(SUBLANES):
                    t = g0 + r
                    bt = b_ref[t:t + 1, hs]
                    ke = kk_i * jnp.exp(jnp.minimum(bt - b_i, 0.0))
                    res = _dot_nt(qg, ke.astype(BF16))
                    dg = jnp.where(sub == r, res, dg)
                groups.append(dg)
            dblk = jnp.where(causal, jnp.concatenate(groups, axis=0), 0.0)
            o_i = _dot(dblk.astype(BF16), vv[lo:hi, hs])
            if bi > 0:
                bref = b_ref[lo - 1:lo, hs]
                q_i = (qq[lo:hi, hs] * jnp.exp(b_i - bref)).astype(BF16)
                k_p = (kk[0:lo, hs] * jnp.exp(bref - bcum[0:lo, hs])).astype(BF16)
                o_i = o_i + _dot(_dot_nt(q_i, k_p).astype(BF16), vv[0:lo, hs])
            return o_i

        @pl.when(safe)
        def _():
            for h in range(nheads):
                hs = slice(h * dk, (h + 1) * dk)
                bh = bcum[:, hs]
                refs = [jnp.zeros((1, dk), F32)] + [b_ref[bi * SUBCHUNK - 1:bi * SUBCHUNK, hs]
                                                    for bi in range(1, nsub)]
                ref_rows = jnp.concatenate([jnp.broadcast_to(r, (SUBCHUNK, dk)) for r in refs], axis=0)
                q_stack = qq[:, hs] * jnp.exp(bh - ref_rows)
                q_big = jnp.concatenate(
                    [jnp.where(row_blk == bi, q_stack, 0.0) for bi in range(nsub)], axis=1).astype(BF16)
                k_big = jnp.concatenate(
                    [kk[:, hs] * jnp.exp(jnp.minimum(r - bh, HGRN_SAFE_SPAN)) for r in refs],
                    axis=1).astype(BF16)
                s_all = jnp.where(causal_c, _dot_nt(q_big, k_big), 0.0)
                oi_ref[:, hs] = _dot(s_all.astype(BF16), vv[:, hs])

        @pl.when(jnp.logical_not(safe))
        def _():
            for h in range(nheads):
                for bi in range(nsub):
                    oi_ref[bi * SUBCHUNK:(bi + 1) * SUBCHUNK, h * dk:(h + 1) * dk] = intra(h, bi)

        for h in range(nheads):
            hs = slice(h * dk, (h + 1) * dk)
            st = st_ref[h]
            bh = bcum[:, hs]
            qe = (qq[:, hs] * jnp.exp(bh)).astype(BF16)
            o = oi_ref[:, hs] + _dot_nt(qe, st.astype(BF16))
            b_last = bh[CHUNK - 1:CHUNK, :]
            kd = (kk[:, hs] * jnp.exp(b_last - bh)).astype(BF16)
            st_ref[h] = st * jnp.exp(b_last) + _dot_tn(vv[:, hs], kd)
            gate = g_ref[0, rows, hs].astype(F32)
            on = _rmsnorm(o, ng) * (gate / (1.0 + jnp.exp(-gate)))
            o_ref[0, rows, hs] = on.astype(o_ref.dtype)
        return carry

    lax.fori_loop(0, ts // CHUNK, chunk, 0)


def _hgrn(proj, f_hgrn, lb, ng, colq, coli, colg, ts):
    b, s, _ = proj.shape
    dk = HGRN_DIM
    hw = HGRN_HEADS * dk
    lt = _tri(CHUNK, lambda r, c: c <= r)
    kern = functools.partial(_hgrn_kernel, ts=ts, nheads=HGRN_HEADS, dk=dk)
    return pl.pallas_call(
        kern,
        grid=(b, s // ts),
        in_specs=[
            pl.BlockSpec((1, ts, hw), lambda bi, si: (bi, si, colq)),
            pl.BlockSpec((1, ts, hw), lambda bi, si: (bi, si, 0)),
            pl.BlockSpec((1, ts, hw), lambda bi, si: (bi, si, coli)),
            pl.BlockSpec((1, ts, hw), lambda bi, si: (bi, si, colg)),
            pl.BlockSpec((1, hw), lambda bi, si: (0, 0)),
            pl.BlockSpec((1, dk), lambda bi, si: (0, 0)),
            pl.BlockSpec((CHUNK, CHUNK), lambda bi, si: (0, 0)),
        ],
        out_specs=pl.BlockSpec((1, ts, hw), lambda bi, si: (bi, si, 0)),
        out_shape=jax.ShapeDtypeStruct((b, s, hw), BF16),
        scratch_shapes=[
            pltpu.VMEM((ts, hw), F32), pltpu.VMEM((ts, hw), F32), pltpu.VMEM((ts, hw), F32),
            pltpu.VMEM((CHUNK, hw), F32), pltpu.VMEM((CHUNK, hw), F32),
            pltpu.VMEM((HGRN_HEADS, dk, dk), F32),
        ],
        compiler_params=_cparams("parallel", "arbitrary"),
        name="hgrn2_recurrence",
    )(proj, f_hgrn, proj, proj, lb, ng, lt)


def _merge_kernel(h_ref, oa_ref, ob_ref, oc_ref, g0_ref, g1_ref, g2_ref, bg_ref,
                  wa_ref, wb_ref, wc_ref, wo_ref, ng_ref, hout_ref, hn_ref):
    def gate(gref, k):
        x = gref[...].astype(F32) + bg_ref[k:k + 1, :]
        return 1.0 / (1.0 + jnp.exp(-x))

    merged = (gate(g0_ref, 0) * _dot(oa_ref[...], wa_ref[...])
              + gate(g1_ref, 1) * _dot(ob_ref[...], wb_ref[...])
              + gate(g2_ref, 2) * _dot(oc_ref[...], wc_ref[...]))
    hnew = h_ref[...] + _dot(merged.astype(BF16), wo_ref[...])
    hout_ref[...] = hnew
    hn_ref[...] = _rmsnorm(hnew, ng_ref[...]).astype(hn_ref.dtype)


def _merge_out(h, oa, ob, oc, proj, gcol0, bg, wa, wb, wc, wo, ng, hn_dtype, tm):
    n, d = h.shape
    full = lambda shape: pl.BlockSpec(shape, lambda i: (0,) * len(shape))
    rows = lambda w: pl.BlockSpec((tm, w), lambda i: (i, 0))
    return pl.pallas_call(
        _merge_kernel,
        grid=(n // tm,),
        in_specs=[
            rows(d), rows(oa.shape[1]), rows(ob.shape[1]), rows(oc.shape[1]),
            pl.BlockSpec((tm, d), lambda i: (i, gcol0)),
            pl.BlockSpec((tm, d), lambda i: (i, gcol0 + 1)),
            pl.BlockSpec((tm, d), lambda i: (i, gcol0 + 2)),
            full(bg.shape), full(wa.shape), full(wb.shape), full(wc.shape), full(wo.shape),
            full((1, d)),
        ],
        out_specs=[rows(d), rows(d)],
        out_shape=[jax.ShapeDtypeStruct((n, d), F32), jax.ShapeDtypeStruct((n, d), hn_dtype)],
        compiler_params=_cparams("parallel"),
        name="merge_out_proj",
    )(h, oa, ob, oc, proj, proj, proj, bg, wa, wb, wc, wo, ng)


def _ffn_kernel(hn_ref, h_ref, wg_ref, wu_ref, wd_ref, o_ref, acc_ref):
    j = pl.program_id(1)

    @pl.when(j == 0)
    def _():
        acc_ref[...] = jnp.zeros_like(acc_ref)

    x = hn_ref[...]
    gt = _dot(x, wg_ref[...])
    up = _dot(x, wu_ref[...])
    act = (gt / (1.0 + jnp.exp(-gt))) * up
    acc_ref[...] += _dot(act.astype(BF16), wd_ref[...])

    @pl.when(j == pl.num_programs(1) - 1)
    def _():
        o_ref[...] = h_ref[...] + acc_ref[...]


def _dense_ffn(hn, h, wg, wu, wd, tm, tf):
    n, d = h.shape
    ff = wg.shape[1]
    return pl.pallas_call(
        _ffn_kernel,
        grid=(n // tm, ff // tf),
        in_specs=[
            pl.BlockSpec((tm, d), lambda i, j: (i, 0)),
            pl.BlockSpec((tm, d), lambda i, j: (i, 0)),
            pl.BlockSpec((d, tf), lambda i, j: (0, j)),
            pl.BlockSpec((d, tf), lambda i, j: (0, j)),
            pl.BlockSpec((tf, d), lambda i, j: (j, 0)),
        ],
        out_specs=pl.BlockSpec((tm, d), lambda i, j: (i, 0)),
        out_shape=jax.ShapeDtypeStruct((n, d), F32),
        scratch_shapes=[pltpu.VMEM((tm, d), F32)],
        compiler_params=_cparams("parallel", "arbitrary"),
        name="dense_swiglu",
    )(hn, h, wg, wu, wd)


def _router_kernel(hn_ref, wr_ref, u_ref, idx_ref, wt_ref, cnt_ref, car_ref):
    t = hn_ref.shape[0]

    @pl.when(pl.program_id(0) == 0)
    def _():
        car_ref[...] = jnp.zeros_like(car_ref)

    logits = _dot_nt(wr_ref[...], hn_ref[...].astype(BF16))
    eidx = lax.broadcasted_iota(I32, (N_EXPERTS, t), 0)
    m1 = jnp.max(logits, axis=0, keepdims=True)
    i1 = jnp.min(jnp.where(logits == m1, eidx, N_EXPERTS), axis=0, keepdims=True)
    sel1 = eidx == i1
    rest = jnp.where(sel1, -jnp.inf, logits)
    m2 = jnp.max(rest, axis=0, keepdims=True)
    i2 = jnp.min(jnp.where(rest == m2, eidx, N_EXPERTS), axis=0, keepdims=True)
    sel2 = eidx == i2
    e21 = jnp.exp(m2 - m1)
    w1 = 1.0 / (1.0 + e21)
    w2 = e21 / (1.0 + e21)

    cnt = jnp.where(jnp.logical_or(sel1, sel2), 1.0, 0.0)
    excl = _dot(cnt.astype(BF16), u_ref[...])
    rank = excl + car_ref[:, 0:1]
    p1 = jnp.sum(jnp.where(sel1, rank, 0.0), axis=0, keepdims=True)
    p2 = jnp.sum(jnp.where(sel2, rank, 0.0), axis=0, keepdims=True)
    total = rank[:, t - 1:t] + cnt[:, t - 1:t]
    car_ref[...] = jnp.broadcast_to(total, car_ref.shape)
    cnt_ref[...] = jnp.broadcast_to(total, cnt_ref.shape)

    zi = jnp.zeros((1, t), I32)
    idx_ref[...] = jnp.concatenate(
        [i1, i2, p1.astype(I32), p2.astype(I32), zi, zi, zi, zi], axis=0)
    zf = jnp.zeros((1, t), F32)
    wt_ref[...] = jnp.concatenate([w1, w2, zf, zf, zf, zf, zf, zf], axis=0)


def _router(hn, wr_t, t):
    n, d = hn.shape
    u = _tri(t, lambda r, c: r < c)
    return pl.pallas_call(
        _router_kernel,
        grid=(n // t,),
        in_specs=[
            pl.BlockSpec((t, d), lambda i: (i, 0)),
            pl.BlockSpec((N_EXPERTS, d), lambda i: (0, 0)),
            pl.BlockSpec((t, t), lambda i: (0, 0)),
        ],
        out_specs=[
            pl.BlockSpec((SUBLANES, t), lambda i: (0, i)),
            pl.BlockSpec((SUBLANES, t), lambda i: (0, i)),
            pl.BlockSpec((N_EXPERTS, LANES), lambda i: (0, 0)),
        ],
        out_shape=[
            jax.ShapeDtypeStruct((SUBLANES, n), I32),
            jax.ShapeDtypeStruct((SUBLANES, n), F32),
            jax.ShapeDtypeStruct((N_EXPERTS, LANES), F32),
        ],
        scratch_shapes=[pltpu.VMEM((N_EXPERTS, LANES), F32)],
        compiler_params=_cparams("arbitrary"),
        name="moe_router_top2",
    )(hn, wr_t, u)


def _dest_kernel(idx_ref, cnt_ref, dest_ref, blk_ref, *, rblk):
    n = idx_ref.shape[1]
    nb = blk_ref.shape[1]
    shift = rblk.bit_length() - 1
    assert rblk == 1 << shift
    e1 = idx_ref[0:1, :]
    e2 = idx_ref[1:2, :]
    d1 = idx_ref[2:3, :]
    d2 = idx_ref[3:4, :]
    bstart = lax.broadcasted_iota(I32, (1, nb), 1) * rblk
    bexp = jnp.zeros((1, nb), I32)
    pend = jnp.zeros((1, 1), I32)
    for e in range(N_EXPERTS):
        c = cnt_ref[e:e + 1, 0:1].astype(I32)
        padded = lax.shift_left(lax.shift_right_logical(c + (rblk - 1), shift), shift)
        pstart = pend
        pend = pend + padded
        d1 = d1 + jnp.where(e1 == e, pstart, 0)
        d2 = d2 + jnp.where(e2 == e, pstart, 0)
        bexp = bexp + jnp.where(bstart >= pend, 1, 0)
    zi = jnp.zeros((1, n), I32)
    dest_ref[...] = jnp.concatenate([d1, d2, zi, zi, zi, zi, zi, zi], axis=0)
    zb = jnp.zeros((1, nb), I32)
    nvalid = jnp.broadcast_to(lax.shift_right_logical(pend, shift), (1, nb))
    blk_ref[...] = jnp.concatenate(
        [jnp.minimum(bexp, N_EXPERTS - 1), nvalid, zb, zb, zb, zb, zb, zb], axis=0)


def _dest(idx, cnt, rblk, nb_pad):
    n = idx.shape[1]
    return pl.pallas_call(
        functools.partial(_dest_kernel, rblk=rblk),
        out_shape=[
            jax.ShapeDtypeStruct((SUBLANES, n), I32),
            jax.ShapeDtypeStruct((SUBLANES, nb_pad), I32),
        ],
        compiler_params=pltpu.CompilerParams(vmem_limit_bytes=VMEM_LIMIT),
        name="moe_slot_assign",
    )(idx, cnt)


def _dispatch_kernel(d1_ref, d2_ref, x_ref, buf_in, buf_hbm, sem, *, t):
    del buf_in

    def row_copy(src_row, dst_row):
        return pltpu.make_async_copy(x_ref.at[pl.ds(src_row, 1)], buf_hbm.at[pl.ds(dst_row, 1)], sem)

    def start(r, c):
        row_copy(r, d1_ref[r]).start()
        row_copy(r, d2_ref[r]).start(priority=1)
        return c

    lax.fori_loop(0, t, start, 0, unroll=DMA_LOOP_UNROLL)

    def wait(r, c):
        row_copy(r, d1_ref[r]).wait()
        row_copy(r, d2_ref[r]).wait()
        return c

    lax.fori_loop(0, t, wait, 0, unroll=DMA_LOOP_UNROLL)


def _dispatch(x, d1, d2, nrows, t):
    n, d = x.shape
    buf0 = jnp.zeros((nrows, d), x.dtype)
    smem = lambda: pl.BlockSpec((t,), lambda i: (i,), memory_space=pltpu.SMEM)
    return pl.pallas_call(
        functools.partial(_dispatch_kernel, t=t),
        grid=(n // t,),
        in_specs=[smem(), smem(),
                  pl.BlockSpec((t, d), lambda i: (i, 0)), pl.BlockSpec(memory_space=pl.ANY)],
        out_specs=pl.BlockSpec(memory_space=pl.ANY),
        out_shape=jax.ShapeDtypeStruct((nrows, d), x.dtype),
        scratch_shapes=[pltpu.SemaphoreType.DMA(())],
        input_output_aliases={3: 0},
        compiler_params=_cparams("arbitrary"),
        name="moe_dispatch_rows",
    )(d1, d2, x, buf0)


def _expert_kernel(be_ref, nv_ref, x_ref, wg_ref, wu_ref, wd_ref, o_ref, xb_ref):
    i = pl.program_id(0)
    j = pl.program_id(1)

    @pl.when(j == 0)
    def _():
        o_ref[...] = jnp.zeros_like(o_ref)
        xb_ref[...] = x_ref[...].astype(BF16)

    @pl.when(i < nv_ref[0])
    def _():
        x = xb_ref[...]
        gt = _dot(x, wg_ref[0])
        up = _dot(x, wu_ref[0])
        act = (gt / (1.0 + jnp.exp(-gt))) * up
        o_ref[...] += _dot(act.astype(BF16), wd_ref[0])


def _experts(buf, be, nv, wg, wu, wd, rblk, tf):
    nrows, d = buf.shape
    ff = wg.shape[2]
    nj = ff // tf

    def wcol(i, j, be_ref, nv_ref):
        return (be_ref[i], 0, jnp.where(i < nv_ref[0], j, nj - 1))

    def wrow(i, j, be_ref, nv_ref):
        return (be_ref[i], jnp.where(i < nv_ref[0], j, nj - 1), 0)

    grid_spec = pltpu.PrefetchScalarGridSpec(
        num_scalar_prefetch=2,
        grid=(nrows // rblk, nj),
        in_specs=[
            pl.BlockSpec((rblk, d), lambda i, j, be_ref, nv_ref: (i, 0)),
            pl.BlockSpec((1, d, tf), wcol),
            pl.BlockSpec((1, d, tf), wcol),
            pl.BlockSpec((1, tf, d), wrow),
        ],
        out_specs=pl.BlockSpec((rblk, d), lambda i, j, be_ref, nv_ref: (i, 0)),
        scratch_shapes=[pltpu.VMEM((rblk, d), BF16)],
    )
    return pl.pallas_call(
        _expert_kernel,
        grid_spec=grid_spec,
        out_shape=jax.ShapeDtypeStruct((nrows, d), F32),
        compiler_params=_cparams("parallel", "arbitrary"),
        name="moe_grouped_swiglu",
    )(be, nv, buf, wg, wu, wd)


def _combine_kernel(d1_ref, d2_ref, yb_hbm, h_ref, wt_ref, ng_ref, o_ref, y_ref, sem, *, t, final):
    def row_copy(slot, dst_row, src_row):
        return pltpu.make_async_copy(yb_hbm.at[pl.ds(src_row, 1)],
                                     y_ref.at[slot, pl.ds(dst_row, 1)], sem)

    def start(r, c):
        row_copy(0, r, d1_ref[r]).start()
        row_copy(1, r, d2_ref[r]).start(priority=1)
        return c

    lax.fori_loop(0, t, start, 0, unroll=DMA_LOOP_UNROLL)

    def wait(r, c):
        row_copy(0, r, d1_ref[r]).wait()
        row_copy(1, r, d2_ref[r]).wait()
        return c

    lax.fori_loop(0, t, wait, 0, unroll=DMA_LOOP_UNROLL)

    rr = lax.broadcasted_iota(I32, (t, t), 0)
    cc = lax.broadcasted_iota(I32, (t, t), 1)
    eye = rr == cc
    w1 = jnp.sum(jnp.where(eye, wt_ref[0:1, :], 0.0), axis=1, keepdims=True)
    w2 = jnp.sum(jnp.where(eye, wt_ref[1:2, :], 0.0), axis=1, keepdims=True)
    out = h_ref[...] + w1 * y_ref[0] + w2 * y_ref[1]
    if final:
        out = _rmsnorm(out, ng_ref[...])
    o_ref[...] = out


def _combine(yb, d1, d2, h, wt, ng, t, final):
    n, d = h.shape
    smem = lambda: pl.BlockSpec((t,), lambda i: (i,), memory_space=pltpu.SMEM)
    return pl.pallas_call(
        functools.partial(_combine_kernel, t=t, final=final),
        grid=(n // t,),
        in_specs=[
            smem(), smem(),
            pl.BlockSpec(memory_space=pl.ANY),
            pl.BlockSpec((t, d), lambda i: (i, 0)),
            pl.BlockSpec((SUBLANES, t), lambda i: (0, i)),
            pl.BlockSpec((1, d), lambda i: (0, 0)),
        ],
        out_specs=pl.BlockSpec((t, d), lambda i: (i, 0)),
        out_shape=jax.ShapeDtypeStruct((n, d), F32),
        scratch_shapes=[pltpu.VMEM((2, t, d), F32), pltpu.SemaphoreType.DMA(())],
        compiler_params=_cparams("arbitrary"),
        name="moe_combine",
    )(d1, d2, yb, h, wt, ng)


def _final_norm_kernel(x_ref, g_ref, o_ref):
    o_ref[...] = _rmsnorm(x_ref[...], g_ref[...])


def _final_norm(h, g, tm):
    n, d = h.shape
    return pl.pallas_call(
        _final_norm_kernel,
        grid=(n // tm,),
        in_specs=[pl.BlockSpec((tm, d), lambda i: (i, 0)), pl.BlockSpec((1, d), lambda i: (0, 0))],
        out_specs=pl.BlockSpec((tm, d), lambda i: (i, 0)),
        out_shape=jax.ShapeDtypeStruct((n, d), F32),
        compiler_params=_cparams("parallel"),
        name="final_rmsnorm",
    )(h, g)


def _tile(n, pref):
    t = min(n, pref)
    assert n % t == 0, (n, pref)
    return t


def _mixer(h, g_mix, w_in, b_fox, b_gate, lb, hgrn_ng, w_sb, w_fox, w_hgrn, w_out, g_ffn, hn_dtype):
    b, s, d = h.shape
    n = b * s
    sbw = SB_HEADS * HEAD_DIM
    foxw = FOX_HEADS * HEAD_DIM
    hw = HGRN_HEADS * HGRN_DIM
    o_fox = 3 * sbw
    o_fb = o_fox + 3 * foxw
    o_hq = o_fb + FOX_HEADS
    o_hf, o_hi, o_hg = o_hq + hw, o_hq + 2 * hw, o_hq + 3 * hw
    o_gate = o_hq + 4 * hw
    w_main = jnp.concatenate(
        [w_in[:, :o_fb], w_in[:, o_hq:o_hf], w_in[:, o_hi:o_gate], w_in[:, o_gate:]], axis=1).astype(BF16)
    w_vt = jnp.concatenate([w_in[:, 2 * sbw:3 * sbw], w_in[:, o_fox + 2 * foxw:o_fb]], axis=1).T.astype(BF16)
    w_f = w_in[:, o_hf:o_hi].astype(BF16)
    w_fb = w_in[:, o_fb:o_hq]
    w_fbc = jnp.pad(w_fb, ((0, 0), (0, LANES - FOX_HEADS))).astype(BF16)
    w_fbr = jnp.pad(w_fb.T, ((0, SUBLANES - FOX_HEADS), (0, 0))).astype(BF16)
    bc = jnp.pad(b_fox, (0, LANES - FOX_HEADS)).reshape(1, LANES)
    br = jnp.broadcast_to(jnp.pad(b_fox, (0, SUBLANES - FOX_HEADS)).reshape(SUBLANES, 1), (SUBLANES, LANES))

    g_mix = g_mix.reshape(1, d)
    tm = _tile(s, 1024)
    proj, vt = _norm_matmul(h.reshape(n, d), g_mix, w_main, w_vt, s, tm, _tile(w_main.shape[1], 2048))
    proj = proj.reshape(b, s, -1)
    f_hgrn, cum_col, cum_row = _gate_cum(h, g_mix, w_f, w_fbc, w_fbr, bc, br, _tile(s, 512))

    tq = _tile(s, 256)
    oa = _sb_attention(proj, vt, 0, 0, tq, tq)
    ob = _fox_attention(proj, vt, o_fox // foxw, sbw, cum_col, cum_row, tq, _tile(s, 512))
    c_hq = (o_fox + 3 * foxw) // hw
    oc = _hgrn(proj, f_hgrn, lb.reshape(1, hw), hgrn_ng.reshape(1, HGRN_DIM),
               c_hq, c_hq + 1, c_hq + 2, _tile(s, 512))

    gcol0 = (o_fox + 3 * foxw + 3 * hw) // d
    hnew, hn = _merge_out(
        h.reshape(n, d), oa.reshape(n, sbw), ob.reshape(n, foxw), oc.reshape(n, hw),
        proj.reshape(n, -1), gcol0, b_gate,
        w_sb.astype(BF16), w_fox.astype(BF16), w_hgrn.astype(BF16), w_out.astype(BF16),
        g_ffn.reshape(1, d), hn_dtype, _tile(n, 512))
    return hnew, hn


def _moe(h, hn, w_router, w_gate, w_up, w_down, final_g, final):
    n, d = h.shape
    rblk = 512
    nrows = n * TOP_K + N_EXPERTS * rblk
    nb = nrows // rblk
    nb_pad = -(-nb // LANES) * LANES
    idx, wt, cnt = _router(hn, w_router.T.astype(BF16), _tile(n, 1024))
    dest, blk = _dest(idx, cnt, rblk, nb_pad)
    d1, d2 = dest[0], dest[1]
    buf = _dispatch(hn, d1, d2, nrows, _tile(n, 256))
    yb = _experts(buf, blk[0, :nb], blk[1, :1], w_gate.astype(BF16), w_up.astype(BF16),
                  w_down.astype(BF16), rblk, w_gate.shape[2] // 2)
    return _combine(yb, d1, d2, h, wt, final_g.reshape(1, d), _tile(n, 256), final)


def kernel(x, norm_mix_g, w_in, b_fox, b_gate, hgrn_lb_logits, hgrn_norm_g, w_branch_sb, w_branch_fox, w_branch_hgrn, w_out, norm_ffn_g, w_ffn_gate, w_ffn_up, w_ffn_down, w_router, w_exp_gate, w_exp_up, w_exp_down, final_norm_g):
    b, s, d = x.shape
    n = b * s
    depth = w_in.shape[0]
    lbs = _lower_bounds(hgrn_lb_logits)
    h = x
    for l in range(depth):
        dense = l % 2 == 0
        h2, hn = _mixer(h, norm_mix_g[l], w_in[l], b_fox[l], b_gate[l], lbs[l], hgrn_norm_g[l],
                        w_branch_sb[l], w_branch_fox[l], w_branch_hgrn[l], w_out[l], norm_ffn_g[l],
                        BF16 if dense else F32)
        last = l == depth - 1
        i = l // 2
        if dense:
            h2 = _dense_ffn(hn, h2, w_ffn_gate[i].astype(BF16), w_ffn_up[i].astype(BF16),
                            w_ffn_down[i].astype(BF16), _tile(n, 1024), 768)
            if last:
                h2 = _final_norm(h2, final_norm_g.reshape(1, d), _tile(n, 1024))
        else:
            h2 = _moe(h2, hn, w_router[i], w_exp_gate[i], w_exp_up[i], w_exp_down[i],
                      final_norm_g, last)
        h = h2.reshape(b, s, d)
    return h
```

```python
import functools

import jax
import jax.numpy as jnp
from jax import lax
from jax.experimental import pallas as pl
from jax.experimental.pallas import tpu as pltpu

F32 = jnp.float32
BF16 = jnp.bfloat16
I32 = jnp.int32

RMS_EPS = 1e-6
CHUNK = 64
SUBCHUNK = 16
SB_HEADS = 4
FOX_HEADS = 4
HGRN_HEADS = 4
HEAD_DIM = 64
HGRN_DIM = 128
N_EXPERTS = 8
TOP_K = 2

LANES = 128
SUBLANES = 8
VMEM_LIMIT = 56 * 1024 * 1024

EXP_ZERO_BELOW = -105.0
NEG_BIG = -1e30
NORM_SLACK = 1.0 + 2.0 ** -7
DMA_LOOP_UNROLL = 8


def _cparams(*sem):
    return pltpu.CompilerParams(dimension_semantics=sem, vmem_limit_bytes=VMEM_LIMIT)


def _dot(a, b):
    return jnp.dot(a, b, preferred_element_type=F32)


def _dot_nt(a, b):
    return lax.dot_general(a, b, (((1,), (1,)), ((), ())), preferred_element_type=F32)


def _dot_tn(a, b):
    return lax.dot_general(a, b, (((0,), (0,)), ((), ())), preferred_element_type=F32)


def _split2(x):
    hi = x.astype(BF16)
    lo = (x - hi.astype(F32)).astype(BF16)
    return hi, lo


def _split3(x):
    hi = x.astype(BF16)
    r = x - hi.astype(F32)
    mid = r.astype(BF16)
    lo = (r - mid.astype(F32)).astype(BF16)
    return hi, mid, lo


def _log_sigmoid(x):
    return jnp.minimum(x, 0.0) - jnp.log(1.0 + jnp.exp(-jnp.abs(x)))


def _rmsnorm(x, g):
    ms = jnp.mean(x * x, axis=-1, keepdims=True)
    return x * lax.rsqrt(ms + RMS_EPS) * g


def _tri(n, fn, dtype=BF16):
    r = lax.broadcasted_iota(I32, (n, n), 0)
    c = lax.broadcasted_iota(I32, (n, n), 1)
    return fn(r, c).astype(dtype)


def _lb_kernel(x_ref, o_ref):
    x = x_ref[...]
    depth = x.shape[0]
    m = jnp.max(x, axis=0, keepdims=True)
    e = jnp.exp(x - m)
    sm = e / jnp.sum(e, axis=0, keepdims=True)
    run = sm[0:1]
    first = run
    o_ref[0:1, :] = run - first
    for l in range(1, depth):
        run = run + sm[l:l + 1]
        o_ref[l:l + 1, :] = run - first


def _lower_bounds(logits):
    return pl.pallas_call(
        _lb_kernel,
        out_shape=jax.ShapeDtypeStruct(logits.shape, F32),
        name="hgrn_lower_bounds",
    )(logits.astype(F32))


def _norm_matmul_kernel(x_ref, g_ref, w_ref, wvt_ref, o_ref, vt_ref, xn_ref):
    @pl.when(pl.program_id(1) == 0)
    def _():
        xn = _rmsnorm(x_ref[...], g_ref[...]).astype(BF16)
        xn_ref[...] = xn
        vt_ref[0] = _dot_nt(wvt_ref[...], xn).astype(vt_ref.dtype)

    o_ref[...] = _dot(xn_ref[...], w_ref[...]).astype(o_ref.dtype)


def _norm_matmul(x, g, w, wvt, seq, tm, tn):
    n, d = x.shape
    nout = w.shape[1]
    nv = wvt.shape[0]
    per = seq // tm
    return pl.pallas_call(
        _norm_matmul_kernel,
        grid=(n // tm, nout // tn),
        in_specs=[
            pl.BlockSpec((tm, d), lambda i, j: (i, 0)),
            pl.BlockSpec((1, d), lambda i, j: (0, 0)),
            pl.BlockSpec((d, tn), lambda i, j: (0, j)),
            pl.BlockSpec((nv, d), lambda i, j: (0, 0)),
        ],
        out_specs=[
            pl.BlockSpec((tm, tn), lambda i, j: (i, j)),
            pl.BlockSpec((1, nv, tm), lambda i, j: (i // per, 0, i % per)),
        ],
        out_shape=[
            jax.ShapeDtypeStruct((n, nout), BF16),
            jax.ShapeDtypeStruct((n // seq, nv, seq), BF16),
        ],
        scratch_shapes=[pltpu.VMEM((tm, d), BF16)],
        compiler_params=_cparams("parallel", "arbitrary"),
        name="norm_in_proj",
    )(x, g, w, wvt)


def _gate_cum_kernel(x_ref, g_ref, wf_ref, wc_ref, wr_ref, bc_ref, br_ref, lt_ref, ut_ref,
                     f_ref, cc_ref, cr_ref, carc_ref, carr_ref):
    tm = x_ref.shape[1]

    @pl.when(pl.program_id(1) == 0)
    def _():
        carc_ref[...] = jnp.zeros_like(carc_ref)
        carr_ref[...] = jnp.zeros_like(carr_ref)

    xn = _rmsnorm(x_ref[0], g_ref[...]).astype(BF16)
    f_ref[0] = _dot(xn, wf_ref[...])

    lc = _log_sigmoid(_dot(xn, wc_ref[...]) + bc_ref[...])
    lt = lt_ref[...]
    c_hi, c_mid, c_lo = _split3(lc)
    cum_c = _dot(lt, c_hi) + _dot(lt, c_mid) + _dot(lt, c_lo) + carc_ref[0:1, :]
    cc_ref[0] = cum_c
    carc_ref[...] = jnp.broadcast_to(cum_c[tm - 1:tm, :], carc_ref.shape)

    lr = _log_sigmoid(_dot_nt(wr_ref[...], xn) + br_ref[:, 0:1])
    ut = ut_ref[...]
    r_hi, r_mid, r_lo = _split3(lr)
    cum_r = _dot(r_hi, ut) + _dot(r_mid, ut) + _dot(r_lo, ut) + carr_ref[:, 0:1]
    cr_ref[0] = cum_r
    carr_ref[...] = jnp.broadcast_to(cum_r[:, tm - 1:tm], carr_ref.shape)


def _gate_cum(h, g, wf, wc, wr, bc, br, tm):
    b, s, d = h.shape
    nf = wf.shape[1]
    lt = _tri(tm, lambda r, c: c <= r)
    ut = _tri(tm, lambda r, c: r <= c)
    full = lambda shape: pl.BlockSpec(shape, lambda bi, si: (0,) * len(shape))
    return pl.pallas_call(
        _gate_cum_kernel,
        grid=(b, s // tm),
        in_specs=[
            pl.BlockSpec((1, tm, d), lambda bi, si: (bi, si, 0)),
            full((1, d)), full((d, nf)), full((d, LANES)), full((SUBLANES, d)),
            full((1, LANES)), full((SUBLANES, LANES)), full((tm, tm)), full((tm, tm)),
        ],
        out_specs=[
            pl.BlockSpec((1, tm, nf), lambda bi, si: (bi, si, 0)),
            pl.BlockSpec((1, tm, LANES), lambda bi, si: (bi, si, 0)),
            pl.BlockSpec((1, SUBLANES, tm), lambda bi, si: (bi, 0, si)),
        ],
        out_shape=[
            jax.ShapeDtypeStruct((b, s, nf), F32),
            jax.ShapeDtypeStruct((b, s, LANES), F32),
            jax.ShapeDtypeStruct((b, SUBLANES, s), F32),
        ],
        scratch_shapes=[pltpu.VMEM((SUBLANES, LANES), F32), pltpu.VMEM((SUBLANES, LANES), F32)],
        compiler_params=_cparams("parallel", "arbitrary"),
        name="gate_proj_cum",
    )(h, g, wf, wc, wr, bc, br, lt, ut)


def _masked_queries(q_ref, qm_ref, nheads, hd, scale):
    q = q_ref[0]
    lane = lax.broadcasted_iota(I32, q.shape, 1)
    qs = q * jnp.asarray(scale, q.dtype)
    for h in range(nheads):
        sel = jnp.logical_and(lane >= h * hd, lane < (h + 1) * hd)
        qm_ref[h] = jnp.where(sel, qs, jnp.zeros_like(qs))


def _sb_kernel(q_ref, k_ref, vt_ref, u_ref, o_ref, qm_ref, car_ref, acc_ref, *, tq, tk, nheads, hd, vrow0):
    i = pl.program_id(1)
    _masked_queries(q_ref, qm_ref, nheads, hd, hd ** -0.5)
    car_ref[...] = jnp.zeros_like(car_ref)
    acc_ref[...] = jnp.zeros_like(acc_ref)
    u = u_ref[...]

    def tile(j, diag):
        start = pl.multiple_of(j * tk, tk)
        kb = k_ref[0, pl.ds(start, tk), :]
        if diag:
            srow = start + lax.broadcasted_iota(I32, (tk, tq), 0)
            tcol = i * tq + lax.broadcasted_iota(I32, (tk, tq), 1)
            strict = srow < tcol
        alive = None
        for h in range(nheads):
            z = _dot_nt(kb, qm_ref[h])
            lg = -(jnp.maximum(z, 0.0) + jnp.log(1.0 + jnp.exp(-jnp.abs(z))))
            if diag:
                lg = jnp.where(strict, lg, 0.0)
            l_hi, l_lo = _split2(lg)
            suf = _dot(u, l_hi) + _dot(u, l_lo)
            carry = car_ref[h]
            w = jnp.exp(z + lg + suf + carry)
            if diag:
                w = jnp.where(strict, w, 0.0)
            vt = vt_ref[0, vrow0 + h * hd:vrow0 + (h + 1) * hd, pl.ds(start, tk)]
            acc_ref[h] += _dot(vt, w.astype(BF16))
            carry = carry + (suf[0:1, :] + lg[0:1, :])
            car_ref[h] = carry
            top = jnp.max(carry)
            alive = top if alive is None else jnp.maximum(alive, top)
        return alive

    j0 = (i * tq) // tk
    alive0 = tile(j0, True)

    def cond(c):
        j, alive = c
        return jnp.logical_and(j >= 0, alive > EXP_ZERO_BELOW)

    def body(c):
        j, _ = c
        return j - 1, tile(j, False)

    lax.while_loop(cond, body, (j0 - 1, alive0))
    o_ref[0] = acc_ref[...].reshape(nheads * hd, tq).T.astype(o_ref.dtype)


def _sb_attention(proj, vt, col0, vrow0, tq, tk):
    b, s, _ = proj.shape
    width = SB_HEADS * HEAD_DIM
    nvt = vt.shape[1]
    u = _tri(tk, lambda r, c: c > r)
    kern = functools.partial(_sb_kernel, tq=tq, tk=tk, nheads=SB_HEADS, hd=HEAD_DIM, vrow0=vrow0)
    return pl.pallas_call(
        kern,
        grid=(b, s // tq),
        in_specs=[
            pl.BlockSpec((1, tq, width), lambda bi, i: (bi, i, col0)),
            pl.BlockSpec((1, s, width), lambda bi, i: (bi, 0, col0 + 1)),
            pl.BlockSpec((1, nvt, s), lambda bi, i: (bi, 0, 0)),
            pl.BlockSpec((tk, tk), lambda bi, i: (0, 0)),
        ],
        out_specs=pl.BlockSpec((1, tq, width), lambda bi, i: (bi, i, 0)),
        out_shape=jax.ShapeDtypeStruct((b, s, width), BF16),
        scratch_shapes=[pltpu.VMEM((SB_HEADS, tq, width), BF16),
                        pltpu.VMEM((SB_HEADS, 1, tq), F32),
                        pltpu.VMEM((SB_HEADS, HEAD_DIM, tq), F32)],
        compiler_params=_cparams("parallel", "arbitrary"),
        name="stick_breaking_attention",
    )(proj, proj, vt, u)


def _fox_kernel(q_ref, k_ref, vt_ref, cc_ref, cr_ref, kmax_ref, cend_ref, et_ref, o_ref,
                qm_ref, s_ref, mx_ref, m_ref, l_ref, acc_ref, qn_ref, *, tq, tk, nheads, hd, vrow0):
    i = pl.program_id(1)
    _masked_queries(q_ref, qm_ref, nheads, hd, hd ** -0.5)
    m_ref[...] = jnp.full(m_ref.shape, NEG_BIG, F32)
    l_ref[...] = jnp.zeros_like(l_ref)
    acc_ref[...] = jnp.zeros_like(acc_ref)

    def score_pass(j, slot, diag):
        start = pl.multiple_of(j * tk, tk)
        kb = k_ref[0, pl.ds(start, tk), :]
        if diag:
            srow = start + lax.broadcasted_iota(I32, (tk, tq), 0)
            tcol = i * tq + lax.broadcasted_iota(I32, (tk, tq), 1)
            causal = srow <= tcol
        for h in range(nheads):
            cs = cc_ref[0, pl.ds(start, tk), h:h + 1]
            s = _dot_nt(kb, qm_ref[h]) - cs
            if diag:
                s = jnp.where(causal, s, NEG_BIG)
            s_ref[slot * nheads + h] = s
            mx_ref[slot * nheads + h] = jnp.max(s, axis=0, keepdims=True)

    def value_pass(j, slot):
        start = pl.multiple_of(j * tk, tk)
        for h in range(nheads):
            ct = cr_ref[0, h:h + 1, :]
            m = m_ref[h]
            m_new = jnp.maximum(m, mx_ref[slot * nheads + h] + ct)
            alpha = jnp.exp(m - m_new)
            p = jnp.exp(s_ref[slot * nheads + h] - (m_new - ct))
            l_ref[h] = alpha * l_ref[h] + jnp.sum(p, axis=0, keepdims=True)
            vt = vt_ref[0, vrow0 + h * hd:vrow0 + (h + 1) * hd, pl.ds(start, tk)]
            acc_ref[h] = alpha * acc_ref[h] + _dot(vt, p.astype(BF16))
            m_ref[h] = m_new

    qf = q_ref[0].astype(F32) * (hd ** -0.5)
    sq_hi, sq_lo = _split2(qf * qf)
    et = et_ref[...]
    qn_ref[...] = jnp.sqrt(_dot_nt(et, sq_hi) + _dot_nt(et, sq_lo)) * NORM_SLACK

    def bound_below(j):
        sel = lax.broadcasted_iota(I32, (SUBLANES, LANES), 1) == j
        kmax = jnp.max(jnp.where(sel, kmax_ref[0], 0.0), axis=1, keepdims=True)
        cend = jnp.sum(jnp.where(sel, cend_ref[0], 0.0), axis=1, keepdims=True)
        best = None
        for h in range(nheads):
            b = (qn_ref[h:h + 1, :] * kmax[h:h + 1, :] + cr_ref[0, h:h + 1, :]
                 - cend[h:h + 1, :] - m_ref[h])
            best = b if best is None else jnp.maximum(best, b)
        return jnp.max(best)

    j0 = (i * tq) // tk
    score_pass(j0, 0, True)

    def cond(c):
        it, alive = c
        return jnp.logical_and(it < j0, alive > EXP_ZERO_BELOW)

    def body(c):
        it, _ = c
        slot = lax.rem(it, 2)
        score_pass(j0 - 1 - it, 1 - slot, False)
        value_pass(j0 - it, slot)
        return it + 1, bound_below(j0 - 2 - it)

    done, _ = lax.while_loop(cond, body, (jnp.int32(0), jnp.float32(0.0)))
    value_pass(j0 - done, lax.rem(done, 2))
    out = acc_ref[...] / l_ref[...]
    o_ref[0] = out.reshape(nheads * hd, tq).T.astype(o_ref.dtype)


def _head_selector(nheads, hd):
    r = lax.broadcasted_iota(I32, (SUBLANES, nheads * hd), 0)
    c = lax.broadcasted_iota(I32, (SUBLANES, nheads * hd), 1)
    return (c // hd == r).astype(BF16)


def _key_bounds_kernel(k_ref, et_ref, o_ref, run_ref):
    j = pl.program_id(1)

    @pl.when(j == 0)
    def _():
        run_ref[...] = jnp.zeros_like(run_ref)
        o_ref[...] = jnp.zeros_like(o_ref)

    kf = k_ref[0].astype(F32)
    sq_hi, sq_lo = _split2(kf * kf)
    et = et_ref[...]
    n2 = _dot_nt(et, sq_hi) + _dot_nt(et, sq_lo)
    tile_max = jnp.sqrt(jnp.max(n2, axis=1, keepdims=True)) * NORM_SLACK
    run = jnp.maximum(run_ref[...], tile_max)
    run_ref[...] = run
    lane = lax.broadcasted_iota(I32, (SUBLANES, LANES), 1)
    o_ref[0] = jnp.where(lane == j, run, o_ref[0])


def _key_bounds(proj, colk, tk):
    b, s, _ = proj.shape
    width = FOX_HEADS * HEAD_DIM
    assert s // tk <= LANES
    return pl.pallas_call(
        _key_bounds_kernel,
        grid=(b, s // tk),
        in_specs=[
            pl.BlockSpec((1, tk, width), lambda bi, j: (bi, j, colk)),
            pl.BlockSpec((SUBLANES, width), lambda bi, j: (0, 0)),
        ],
        out_specs=pl.BlockSpec((1, SUBLANES, LANES), lambda bi, j: (bi, 0, 0)),
        out_shape=jax.ShapeDtypeStruct((b, SUBLANES, LANES), F32),
        scratch_shapes=[pltpu.VMEM((SUBLANES, LANES), F32)],
        compiler_params=_cparams("parallel", "arbitrary"),
        name="fox_key_norm_bounds",
    )(proj, _head_selector(FOX_HEADS, HEAD_DIM))


def _fox_attention(proj, vt, col0, vrow0, cum_col, cum_row, tq, tk):
    b, s, _ = proj.shape
    width = FOX_HEADS * HEAD_DIM
    nvt = vt.shape[1]
    kmax = _key_bounds(proj, col0 + 1, tk)
    cend = jnp.pad(cum_row[:, :, tk - 1::tk], ((0, 0), (0, 0), (0, LANES - s // tk)))
    kern = functools.partial(_fox_kernel, tq=tq, tk=tk, nheads=FOX_HEADS, hd=HEAD_DIM, vrow0=vrow0)
    return pl.pallas_call(
        kern,
        grid=(b, s // tq),
        in_specs=[
            pl.BlockSpec((1, tq, width), lambda bi, i: (bi, i, col0)),
            pl.BlockSpec((1, s, width), lambda bi, i: (bi, 0, col0 + 1)),
            pl.BlockSpec((1, nvt, s), lambda bi, i: (bi, 0, 0)),
            pl.BlockSpec((1, s, LANES), lambda bi, i: (bi, 0, 0)),
            pl.BlockSpec((1, SUBLANES, tq), lambda bi, i: (bi, 0, i)),
            pl.BlockSpec((1, SUBLANES, LANES), lambda bi, i: (bi, 0, 0)),
            pl.BlockSpec((1, SUBLANES, LANES), lambda bi, i: (bi, 0, 0)),
            pl.BlockSpec((SUBLANES, width), lambda bi, i: (0, 0)),
        ],
        out_specs=pl.BlockSpec((1, tq, width), lambda bi, i: (bi, i, 0)),
        out_shape=jax.ShapeDtypeStruct((b, s, width), BF16),
        scratch_shapes=[pltpu.VMEM((FOX_HEADS, tq, width), BF16),
                        pltpu.VMEM((2 * FOX_HEADS, tk, tq), F32),
                        pltpu.VMEM((2 * FOX_HEADS, 1, tq), F32),
                        pltpu.VMEM((FOX_HEADS, 1, tq), F32),
                        pltpu.VMEM((FOX_HEADS, 1, tq), F32),
                        pltpu.VMEM((FOX_HEADS, HEAD_DIM, tq), F32),
                        pltpu.VMEM((SUBLANES, tq), F32)],
        compiler_params=_cparams("parallel", "arbitrary"),
        name="forgetting_attention",
    )(proj, proj, vt, cum_col, cum_row, kmax, cend, _head_selector(FOX_HEADS, HEAD_DIM))


HGRN_SAFE_SPAN = 40.0
HGRN_GROUP = 4


def _hgrn_kernel(q_ref, f_ref, i_ref, g_ref, lb_ref, ng_ref, lt_ref, o_ref,
                 lf_ref, kk_ref, qq_ref, b_ref, oi_ref, st_ref, *, ts, nheads, dk):
    @pl.when(pl.program_id(1) == 0)
    def _():
        st_ref[...] = jnp.zeros_like(st_ref)

    lb = lb_ref[...]
    z = f_ref[0]
    lsz = _log_sigmoid(z)
    a = jnp.log(lb)
    c = jnp.log(1.0 - lb) + lsz
    lf_ref[...] = jnp.maximum(a, c) + jnp.log(1.0 + jnp.exp(-jnp.abs(a - c)))
    kk_ref[...] = (1.0 - lb) * jnp.exp(lsz - z)
    qin = q_ref[0].astype(F32)
    qq_ref[...] = qin / (1.0 + jnp.exp(-qin))

    lt = lt_ref[...]
    nsub = CHUNK // SUBCHUNK
    rr = lax.broadcasted_iota(I32, (SUBCHUNK, SUBCHUNK), 0)
    cc = lax.broadcasted_iota(I32, (SUBCHUNK, SUBCHUNK), 1)
    causal = cc <= rr
    sub = lax.broadcasted_iota(I32, (SUBLANES, SUBCHUNK), 0)
    causal_c = (lax.broadcasted_iota(I32, (CHUNK, CHUNK), 1)
                <= lax.broadcasted_iota(I32, (CHUNK, CHUNK), 0))
    row_blk = lax.broadcasted_iota(I32, (CHUNK, dk), 0) // SUBCHUNK
    ng = ng_ref[...]

    def prepare(r0):
        rows = pl.ds(r0, CHUNK)
        h_hi, h_mid, h_lo = _split3(lf_ref[rows, :])
        bcum = _dot(lt, h_hi) + _dot(lt, h_mid) + _dot(lt, h_lo)
        b_ref[rows, :] = bcum
        span = -bcum[SUBCHUNK - 1:SUBCHUNK, :]
        for bi in range(1, nsub):
            lo = bi * SUBCHUNK
            span = jnp.maximum(span, bcum[lo - 1:lo, :] - bcum[lo + SUBCHUNK - 1:lo + SUBCHUNK, :])
        return (r0, bcum, kk_ref[rows, :], qq_ref[rows, :], i_ref[0, rows, :]), span

    def intra_exact(ck, h, bi):
        r0, bcum, kk, qq, vv = ck
        hs = slice(h * dk, (h + 1) * dk)
        lo = bi * SUBCHUNK
        hi = lo + SUBCHUNK
        b_i = bcum[lo:hi, hs]
        kk_i = kk[lo:hi, hs]
        groups = []
        for gi in range(SUBCHUNK // SUBLANES):
            g0 = lo + gi * SUBLANES
            qg = qq[g0:g0 + SUBLANES, hs].astype(BF16)
            dg = jnp.zeros((SUBLANES, SUBCHUNK), F32)
            for r in range(SUBLANES):
                bt = bcum[g0 + r:g0 + r + 1, hs]
                ke = kk_i * jnp.exp(jnp.minimum(bt - b_i, 0.0))
                res = _dot_nt(qg, ke.astype(BF16))
                dg = jnp.where(sub == r, res, dg)
            groups.append(dg)
        dblk = jnp.where(causal, jnp.concatenate(groups, axis=0), 0.0)
        o_i = _dot(dblk.astype(BF16), vv[lo:hi, hs])
        if bi > 0:
            bref = bcum[lo - 1:lo, hs]
            q_i = (qq[lo:hi, hs] * jnp.exp(b_i - bref)).astype(BF16)
            k_p = (kk[0:lo, hs] * jnp.exp(bref - bcum[0:lo, hs])).astype(BF16)
            o_i = o_i + _dot(_dot_nt(q_i, k_p).astype(BF16), vv[0:lo, hs])
        return o_i

    def intra_fast(ck, h):
        r0, bcum, kk, qq, vv = ck
        hs = slice(h * dk, (h + 1) * dk)
        bh = bcum[:, hs]
        refs = [jnp.zeros((1, dk), F32)] + [bh[bi * SUBCHUNK - 1:bi * SUBCHUNK, :] for bi in range(1, nsub)]
        ref_rows = jnp.concatenate([jnp.broadcast_to(r, (SUBCHUNK, dk)) for r in refs], axis=0)
        q_stack = qq[:, hs] * jnp.exp(bh - ref_rows)
        q_big = jnp.concatenate(
            [jnp.where(row_blk == bi, q_stack, 0.0) for bi in range(nsub)], axis=1).astype(BF16)
        k_big = jnp.concatenate(
            [kk[:, hs] * jnp.exp(jnp.minimum(r - bh, HGRN_SAFE_SPAN)) for r in refs], axis=1).astype(BF16)
        s_all = jnp.where(causal_c, _dot_nt(q_big, k_big), 0.0)
        return _dot(s_all.astype(BF16), vv[:, hs])

    def phase1(gi, carry):
        cks, spans = [], None
        for c in range(HGRN_GROUP):
            ck, span = prepare(pl.multiple_of((gi * HGRN_GROUP + c) * CHUNK, CHUNK))
            cks.append(ck)
            spans = span if spans is None else jnp.maximum(spans, span)
        safe = jnp.max(spans) < HGRN_SAFE_SPAN

        @pl.when(safe)
        def _():
            for ck in cks:
                for h in range(nheads):
                    oi_ref[pl.ds(ck[0], CHUNK), h * dk:(h + 1) * dk] = intra_fast(ck, h)

        @pl.when(jnp.logical_not(safe))
        def _():
            for ck in cks:
                for h in range(nheads):
                    for bi in range(nsub):
                        oi_ref[pl.ds(ck[0] + bi * SUBCHUNK, SUBCHUNK), h * dk:(h + 1) * dk] = (
                            intra_exact(ck, h, bi))
        return carry

    lax.fori_loop(0, ts // (CHUNK * HGRN_GROUP), phase1, 0)

    def phase2(ci, carry):
        r0 = pl.multiple_of(ci * CHUNK, CHUNK)
        rows = pl.ds(r0, CHUNK)
        for h in range(nheads):
            hs = slice(h * dk, (h + 1) * dk)
            st = st_ref[h]
            bh = b_ref[rows, hs]
            qq = qq_ref[rows, hs]
            kk = kk_ref[rows, hs]
            vv = i_ref[0, rows, hs]
            qe = (qq * jnp.exp(bh)).astype(BF16)
            o = oi_ref[rows, hs] + _dot_nt(qe, st.astype(BF16))
            b_last = bh[CHUNK - 1:CHUNK, :]
            kd = (kk * jnp.exp(b_last - bh)).astype(BF16)
            st_ref[h] = st * jnp.exp(b_last) + _dot_tn(vv, kd)
            gate = g_ref[0, rows, hs].astype(F32)
            on = _rmsnorm(o, ng) * (gate / (1.0 + jnp.exp(-gate)))
            o_ref[0, rows, hs] = on.astype(o_ref.dtype)
        return carry

    lax.fori_loop(0, ts // CHUNK, phase2, 0, unroll=HGRN_GROUP)


def _hgrn(proj, f_hgrn, lb, ng, colq, coli, colg, ts):
    b, s, _ = proj.shape
    dk = HGRN_DIM
    hw = HGRN_HEADS * dk
    lt = _tri(CHUNK, lambda r, c: c <= r)
    kern = functools.partial(_hgrn_kernel, ts=ts, nheads=HGRN_HEADS, dk=dk)
    return pl.pallas_call(
        kern,
        grid=(b, s // ts),
        in_specs=[
            pl.BlockSpec((1, ts, hw), lambda bi, si: (bi, si, colq)),
            pl.BlockSpec((1, ts, hw), lambda bi, si: (bi, si, 0)),
            pl.BlockSpec((1, ts, hw), lambda bi, si: (bi, si, coli)),
            pl.BlockSpec((1, ts, hw), lambda bi, si: (bi, si, colg)),
            pl.BlockSpec((1, hw), lambda bi, si: (0, 0)),
            pl.BlockSpec((1, dk), lambda bi, si: (0, 0)),
            pl.BlockSpec((CHUNK, CHUNK), lambda bi, si: (0, 0)),
        ],
        out_specs=pl.BlockSpec((1, ts, hw), lambda bi, si: (bi, si, 0)),
        out_shape=jax.ShapeDtypeStruct((b, s, hw), BF16),
        scratch_shapes=[
            pltpu.VMEM((ts, hw), F32), pltpu.VMEM((ts, hw), F32), pltpu.VMEM((ts, hw), F32),
            pltpu.VMEM((ts, hw), F32), pltpu.VMEM((ts, hw), F32),
            pltpu.VMEM((HGRN_HEADS, dk, dk), F32),
        ],
        compiler_params=_cparams("parallel", "arbitrary"),
        name="hgrn2_recurrence",
    )(proj, f_hgrn, proj, proj, lb, ng, lt)


def _merge_kernel(h_ref, oa_ref, ob_ref, oc_ref, g0_ref, g1_ref, g2_ref, bg_ref,
                  wa_ref, wb_ref, wc_ref, wo_ref, ng_ref, hout_ref, hn_ref):
    def gate(gref, k):
        x = gref[...].astype(F32) + bg_ref[k:k + 1, :]
        return 1.0 / (1.0 + jnp.exp(-x))

    merged = (gate(g0_ref, 0) * _dot(oa_ref[...], wa_ref[...])
              + gate(g1_ref, 1) * _dot(ob_ref[...], wb_ref[...])
              + gate(g2_ref, 2) * _dot(oc_ref[...], wc_ref[...]))
    hnew = h_ref[...] + _dot(merged.astype(BF16), wo_ref[...])
    hout_ref[...] = hnew
    hn_ref[...] = _rmsnorm(hnew, ng_ref[...]).astype(hn_ref.dtype)


def _merge_out(h, oa, ob, oc, proj, gcol0, bg, wa, wb, wc, wo, ng, hn_dtype, tm):
    n, d = h.shape
    full = lambda shape: pl.BlockSpec(shape, lambda i: (0,) * len(shape))
    rows = lambda w: pl.BlockSpec((tm, w), lambda i: (i, 0))
    return pl.pallas_call(
        _merge_kernel,
        grid=(n // tm,),
        in_specs=[
            rows(d), rows(oa.shape[1]), rows(ob.shape[1]), rows(oc.shape[1]),
            pl.BlockSpec((tm, d), lambda i: (i, gcol0)),
            pl.BlockSpec((tm, d), lambda i: (i, gcol0 + 1)),
            pl.BlockSpec((tm, d), lambda i: (i, gcol0 + 2)),
            full(bg.shape), full(wa.shape), full(wb.shape), full(wc.shape), full(wo.shape),
            full((1, d)),
        ],
        out_specs=[rows(d), rows(d)],
        out_shape=[jax.ShapeDtypeStruct((n, d), F32), jax.ShapeDtypeStruct((n, d), hn_dtype)],
        compiler_params=_cparams("parallel"),
        name="merge_out_proj",
    )(h, oa, ob, oc, proj, proj, proj, bg, wa, wb, wc, wo, ng)


def _ffn_kernel(hn_ref, h_ref, wg_ref, wu_ref, wd_ref, o_ref, acc_ref):
    j = pl.program_id(1)

    @pl.when(j == 0)
    def _():
        acc_ref[...] = jnp.zeros_like(acc_ref)

    x = hn_ref[...]
    gt = _dot(x, wg_ref[...])
    up = _dot(x, wu_ref[...])
    act = (gt / (1.0 + jnp.exp(-gt))) * up
    acc_ref[...] += _dot(act.astype(BF16), wd_ref[...])

    @pl.when(j == pl.num_programs(1) - 1)
    def _():
        o_ref[...] = h_ref[...] + acc_ref[...]


def _dense_ffn(hn, h, wg, wu, wd, tm, tf):
    n, d = h.shape
    ff = wg.shape[1]
    return pl.pallas_call(
        _ffn_kernel,
        grid=(n // tm, ff // tf),
        in_specs=[
            pl.BlockSpec((tm, d), lambda i, j: (i, 0)),
            pl.BlockSpec((tm, d), lambda i, j: (i, 0)),
            pl.BlockSpec((d, tf), lambda i, j: (0, j)),
            pl.BlockSpec((d, tf), lambda i, j: (0, j)),
            pl.BlockSpec((tf, d), lambda i, j: (j, 0)),
        ],
        out_specs=pl.BlockSpec((tm, d), lambda i, j: (i, 0)),
        out_shape=jax.ShapeDtypeStruct((n, d), F32),
        scratch_shapes=[pltpu.VMEM((tm, d), F32)],
        compiler_params=_cparams("parallel", "arbitrary"),
        name="dense_swiglu",
    )(hn, h, wg, wu, wd)


def _router_kernel(hn_ref, wr_ref, u_ref, idx_ref, wt_ref, cnt_ref, car_ref):
    t = hn_ref.shape[0]

    @pl.when(pl.program_id(0) == 0)
    def _():
        car_ref[...] = jnp.zeros_like(car_ref)

    logits = _dot_nt(wr_ref[...], hn_ref[...].astype(BF16))
    eidx = lax.broadcasted_iota(I32, (N_EXPERTS, t), 0)
    m1 = jnp.max(logits, axis=0, keepdims=True)
    i1 = jnp.min(jnp.where(logits == m1, eidx, N_EXPERTS), axis=0, keepdims=True)
    sel1 = eidx == i1
    rest = jnp.where(sel1, -jnp.inf, logits)
    m2 = jnp.max(rest, axis=0, keepdims=True)
    i2 = jnp.min(jnp.where(rest == m2, eidx, N_EXPERTS), axis=0, keepdims=True)
    sel2 = eidx == i2
    e21 = jnp.exp(m2 - m1)
    w1 = 1.0 / (1.0 + e21)
    w2 = e21 / (1.0 + e21)

    cnt = jnp.where(jnp.logical_or(sel1, sel2), 1.0, 0.0)
    excl = _dot(cnt.astype(BF16), u_ref[...])
    rank = excl + car_ref[:, 0:1]
    p1 = jnp.sum(jnp.where(sel1, rank, 0.0), axis=0, keepdims=True)
    p2 = jnp.sum(jnp.where(sel2, rank, 0.0), axis=0, keepdims=True)
    total = rank[:, t - 1:t] + cnt[:, t - 1:t]
    car_ref[...] = jnp.broadcast_to(total, car_ref.shape)
    cnt_ref[...] = jnp.broadcast_to(total, cnt_ref.shape)

    zi = jnp.zeros((1, t), I32)
    idx_ref[...] = jnp.concatenate(
        [i1, i2, p1.astype(I32), p2.astype(I32), zi, zi, zi, zi], axis=0)
    zf = jnp.zeros((1, t), F32)
    wt_ref[...] = jnp.concatenate([w1, w2, zf, zf, zf, zf, zf, zf], axis=0)


def _router(hn, wr_t, t):
    n, d = hn.shape
    u = _tri(t, lambda r, c: r < c)
    return pl.pallas_call(
        _router_kernel,
        grid=(n // t,),
        in_specs=[
            pl.BlockSpec((t, d), lambda i: (i, 0)),
            pl.BlockSpec((N_EXPERTS, d), lambda i: (0, 0)),
            pl.BlockSpec((t, t), lambda i: (0, 0)),
        ],
        out_specs=[
            pl.BlockSpec((SUBLANES, t), lambda i: (0, i)),
            pl.BlockSpec((SUBLANES, t), lambda i: (0, i)),
            pl.BlockSpec((N_EXPERTS, LANES), lambda i: (0, 0)),
        ],
        out_shape=[
            jax.ShapeDtypeStruct((SUBLANES, n), I32),
            jax.ShapeDtypeStruct((SUBLANES, n), F32),
            jax.ShapeDtypeStruct((N_EXPERTS, LANES), F32),
        ],
        scratch_shapes=[pltpu.VMEM((N_EXPERTS, LANES), F32)],
        compiler_params=_cparams("arbitrary"),
        name="moe_router_top2",
    )(hn, wr_t, u)


def _dest_kernel(idx_ref, cnt_ref, dest_ref, blk_ref, *, rblk):
    n = idx_ref.shape[1]
    nb = blk_ref.shape[1]
    shift = rblk.bit_length() - 1
    assert rblk == 1 << shift
    e1 = idx_ref[0:1, :]
    e2 = idx_ref[1:2, :]
    d1 = idx_ref[2:3, :]
    d2 = idx_ref[3:4, :]
    bstart = lax.broadcasted_iota(I32, (1, nb), 1) * rblk
    bexp = jnp.zeros((1, nb), I32)
    pend = jnp.zeros((1, 1), I32)
    for e in range(N_EXPERTS):
        c = cnt_ref[e:e + 1, 0:1].astype(I32)
        padded = lax.shift_left(lax.shift_right_logical(c + (rblk - 1), shift), shift)
        pstart = pend
        pend = pend + padded
        d1 = d1 + jnp.where(e1 == e, pstart, 0)
        d2 = d2 + jnp.where(e2 == e, pstart, 0)
        bexp = bexp + jnp.where(bstart >= pend, 1, 0)
    zi = jnp.zeros((1, n), I32)
    dest_ref[...] = jnp.concatenate([d1, d2, zi, zi, zi, zi, zi, zi], axis=0)
    zb = jnp.zeros((1, nb), I32)
    nvalid = jnp.broadcast_to(lax.shift_right_logical(pend, shift), (1, nb))
    blk_ref[...] = jnp.concatenate(
        [jnp.minimum(bexp, N_EXPERTS - 1), nvalid, zb, zb, zb, zb, zb, zb], axis=0)


def _dest(idx, cnt, rblk, nb_pad):
    n = idx.shape[1]
    return pl.pallas_call(
        functools.partial(_dest_kernel, rblk=rblk),
        out_shape=[
            jax.ShapeDtypeStruct((SUBLANES, n), I32),
            jax.ShapeDtypeStruct((SUBLANES, nb_pad), I32),
        ],
        compiler_params=pltpu.CompilerParams(vmem_limit_bytes=VMEM_LIMIT),
        name="moe_slot_assign",
    )(idx, cnt)


def _dispatch_kernel(d1_ref, d2_ref, x_ref, buf_in, buf_hbm, sem, *, t):
    del buf_in

    def row_copy(src_row, dst_row):
        return pltpu.make_async_copy(x_ref.at[pl.ds(src_row, 1)], buf_hbm.at[pl.ds(dst_row, 1)], sem)

    def start(r, c):
        row_copy(r, d1_ref[r]).start()
        row_copy(r, d2_ref[r]).start(priority=1)
        return c

    lax.fori_loop(0, t, start, 0, unroll=DMA_LOOP_UNROLL)

    def wait(r, c):
        row_copy(r, d1_ref[r]).wait()
        row_copy(r, d2_ref[r]).wait()
        return c

    lax.fori_loop(0, t, wait, 0, unroll=DMA_LOOP_UNROLL)


def _dispatch(x, d1, d2, nrows, t):
    n, d = x.shape
    buf0 = jnp.zeros((nrows, d), x.dtype)
    smem = lambda: pl.BlockSpec((t,), lambda i: (i,), memory_space=pltpu.SMEM)
    return pl.pallas_call(
        functools.partial(_dispatch_kernel, t=t),
        grid=(n // t,),
        in_specs=[smem(), smem(),
                  pl.BlockSpec((t, d), lambda i: (i, 0)), pl.BlockSpec(memory_space=pl.ANY)],
        out_specs=pl.BlockSpec(memory_space=pl.ANY),
        out_shape=jax.ShapeDtypeStruct((nrows, d), x.dtype),
        scratch_shapes=[pltpu.SemaphoreType.DMA(())],
        input_output_aliases={3: 0},
        compiler_params=_cparams("arbitrary"),
        name="moe_dispatch_rows",
    )(d1, d2, x, buf0)


def _expert_kernel(be_ref, nv_ref, x_ref, wg_ref, wu_ref, wd_ref, o_ref, xb_ref):
    i = pl.program_id(0)
    j = pl.program_id(1)

    @pl.when(j == 0)
    def _():
        o_ref[...] = jnp.zeros_like(o_ref)
        xb_ref[...] = x_ref[...].astype(BF16)

    @pl.when(i < nv_ref[0])
    def _():
        x = xb_ref[...]
        gt = _dot(x, wg_ref[0])
        up = _dot(x, wu_ref[0])
        act = (gt / (1.0 + jnp.exp(-gt))) * up
        o_ref[...] += _dot(act.astype(BF16), wd_ref[0])


def _experts(buf, be, nv, wg, wu, wd, rblk, tf):
    nrows, d = buf.shape
    ff = wg.shape[2]
    nj = ff // tf

    def wcol(i, j, be_ref, nv_ref):
        return (be_ref[i], 0, jnp.where(i < nv_ref[0], j, nj - 1))

    def wrow(i, j, be_ref, nv_ref):
        return (be_ref[i], jnp.where(i < nv_ref[0], j, nj - 1), 0)

    grid_spec = pltpu.PrefetchScalarGridSpec(
        num_scalar_prefetch=2,
        grid=(nrows // rblk, nj),
        in_specs=[
            pl.BlockSpec((rblk, d), lambda i, j, be_ref, nv_ref: (i, 0)),
            pl.BlockSpec((1, d, tf), wcol),
            pl.BlockSpec((1, d, tf), wcol),
            pl.BlockSpec((1, tf, d), wrow),
        ],
        out_specs=pl.BlockSpec((rblk, d), lambda i, j, be_ref, nv_ref: (i, 0)),
        scratch_shapes=[pltpu.VMEM((rblk, d), BF16)],
    )
    return pl.pallas_call(
        _expert_kernel,
        grid_spec=grid_spec,
        out_shape=jax.ShapeDtypeStruct((nrows, d), F32),
        compiler_params=_cparams("parallel", "arbitrary"),
        name="moe_grouped_swiglu",
    )(be, nv, buf, wg, wu, wd)


def _combine_kernel(d1_ref, d2_ref, yb_hbm, h_ref, wt_ref, ng_ref, o_ref, y_ref, sem, *, t, final):
    def row_copy(slot, dst_row, src_row):
        return pltpu.make_async_copy(yb_hbm.at[pl.ds(src_row, 1)],
                                     y_ref.at[slot, pl.ds(dst_row, 1)], sem)

    def start(r, c):
        row_copy(0, r, d1_ref[r]).start()
        row_copy(1, r, d2_ref[r]).start(priority=1)
        return c

    lax.fori_loop(0, t, start, 0, unroll=DMA_LOOP_UNROLL)

    def wait(r, c):
        row_copy(0, r, d1_ref[r]).wait()
        row_copy(1, r, d2_ref[r]).wait()
        return c

    lax.fori_loop(0, t, wait, 0, unroll=DMA_LOOP_UNROLL)

    rr = lax.broadcasted_iota(I32, (t, t), 0)
    cc = lax.broadcasted_iota(I32, (t, t), 1)
    eye = rr == cc
    w1 = jnp.sum(jnp.where(eye, wt_ref[0:1, :], 0.0), axis=1, keepdims=True)
    w2 = jnp.sum(jnp.where(eye, wt_ref[1:2, :], 0.0), axis=1, keepdims=True)
    out = h_ref[...] + w1 * y_ref[0] + w2 * y_ref[1]
    if final:
        out = _rmsnorm(out, ng_ref[...])
    o_ref[...] = out


def _combine(yb, d1, d2, h, wt, ng, t, final):
    n, d = h.shape
    smem = lambda: pl.BlockSpec((t,), lambda i: (i,), memory_space=pltpu.SMEM)
    return pl.pallas_call(
        functools.partial(_combine_kernel, t=t, final=final),
        grid=(n // t,),
        in_specs=[
            smem(), smem(),
            pl.BlockSpec(memory_space=pl.ANY),
            pl.BlockSpec((t, d), lambda i: (i, 0)),
            pl.BlockSpec((SUBLANES, t), lambda i: (0, i)),
            pl.BlockSpec((1, d), lambda i: (0, 0)),
        ],
        out_specs=pl.BlockSpec((t, d), lambda i: (i, 0)),
        out_shape=jax.ShapeDtypeStruct((n, d), F32),
        scratch_shapes=[pltpu.VMEM((2, t, d), F32), pltpu.SemaphoreType.DMA(())],
        compiler_params=_cparams("arbitrary"),
        name="moe_combine",
    )(d1, d2, yb, h, wt, ng)


def _final_norm_kernel(x_ref, g_ref, o_ref):
    o_ref[...] = _rmsnorm(x_ref[...], g_ref[...])


def _final_norm(h, g, tm):
    n, d = h.shape
    return pl.pallas_call(
        _final_norm_kernel,
        grid=(n // tm,),
        in_specs=[pl.BlockSpec((tm, d), lambda i: (i, 0)), pl.BlockSpec((1, d), lambda i: (0, 0))],
        out_specs=pl.BlockSpec((tm, d), lambda i: (i, 0)),
        out_shape=jax.ShapeDtypeStruct((n, d), F32),
        compiler_params=_cparams("parallel"),
        name="final_rmsnorm",
    )(h, g)


def _tile(n, pref):
    t = min(n, pref)
    assert n % t == 0, (n, pref)
    return t


def _mixer(h, g_mix, w_in, b_fox, b_gate, lb, hgrn_ng, w_sb, w_fox, w_hgrn, w_out, g_ffn, hn_dtype):
    b, s, d = h.shape
    n = b * s
    sbw = SB_HEADS * HEAD_DIM
    foxw = FOX_HEADS * HEAD_DIM
    hw = HGRN_HEADS * HGRN_DIM
    o_fox = 3 * sbw
    o_fb = o_fox + 3 * foxw
    o_hq = o_fb + FOX_HEADS
    o_hf, o_hi, o_hg = o_hq + hw, o_hq + 2 * hw, o_hq + 3 * hw
    o_gate = o_hq + 4 * hw
    w_main = jnp.concatenate(
        [w_in[:, :o_fb], w_in[:, o_hq:o_hf], w_in[:, o_hi:o_gate], w_in[:, o_gate:]], axis=1).astype(BF16)
    w_vt = jnp.concatenate([w_in[:, 2 * sbw:3 * sbw], w_in[:, o_fox + 2 * foxw:o_fb]], axis=1).T.astype(BF16)
    w_f = w_in[:, o_hf:o_hi].astype(BF16)
    w_fb = w_in[:, o_fb:o_hq]
    w_fbc = jnp.pad(w_fb, ((0, 0), (0, LANES - FOX_HEADS))).astype(BF16)
    w_fbr = jnp.pad(w_fb.T, ((0, SUBLANES - FOX_HEADS), (0, 0))).astype(BF16)
    bc = jnp.pad(b_fox, (0, LANES - FOX_HEADS)).reshape(1, LANES)
    br = jnp.broadcast_to(jnp.pad(b_fox, (0, SUBLANES - FOX_HEADS)).reshape(SUBLANES, 1), (SUBLANES, LANES))

    g_mix = g_mix.reshape(1, d)
    tm = _tile(s, 1024)
    proj, vt = _norm_matmul(h.reshape(n, d), g_mix, w_main, w_vt, s, tm, _tile(w_main.shape[1], 2048))
    proj = proj.reshape(b, s, -1)
    f_hgrn, cum_col, cum_row = _gate_cum(h, g_mix, w_f, w_fbc, w_fbr, bc, br, _tile(s, 512))

    tq = _tile(s, 256)
    oa = _sb_attention(proj, vt, 0, 0, tq, tq)
    ob = _fox_attention(proj, vt, o_fox // foxw, sbw, cum_col, cum_row, tq, _tile(s, 512))
    c_hq = (o_fox + 3 * foxw) // hw
    oc = _hgrn(proj, f_hgrn, lb.reshape(1, hw), hgrn_ng.reshape(1, HGRN_DIM),
               c_hq, c_hq + 1, c_hq + 2, _tile(s, 512))

    gcol0 = (o_fox + 3 * foxw + 3 * hw) // d
    hnew, hn = _merge_out(
        h.reshape(n, d), oa.reshape(n, sbw), ob.reshape(n, foxw), oc.reshape(n, hw),
        proj.reshape(n, -1), gcol0, b_gate,
        w_sb.astype(BF16), w_fox.astype(BF16), w_hgrn.astype(BF16), w_out.astype(BF16),
        g_ffn.reshape(1, d), hn_dtype, _tile(n, 512))
    return hnew, hn


def _moe(h, hn, w_router, w_gate, w_up, w_down, final_g, final):
    n, d = h.shape
    rblk = 512
    nrows = n * TOP_K + N_EXPERTS * rblk
    nb = nrows // rblk
    nb_pad = -(-nb // LANES) * LANES
    idx, wt, cnt = _router(hn, w_router.T.astype(BF16), _tile(n, 1024))
    dest, blk = _dest(idx, cnt, rblk, nb_pad)
    d1, d2 = dest[0], dest[1]
    buf = _dispatch(hn, d1, d2, nrows, _tile(n, 256))
    yb = _experts(buf, blk[0, :nb], blk[1, :1], w_gate.astype(BF16), w_up.astype(BF16),
                  w_down.astype(BF16), rblk, w_gate.shape[2] // 2)
    return _combine(yb, d1, d2, h, wt, final_g.reshape(1, d), _tile(n, 256), final)


def kernel(x, norm_mix_g, w_in, b_fox, b_gate, hgrn_lb_logits, hgrn_norm_g, w_branch_sb, w_branch_fox, w_branch_hgrn, w_out, norm_ffn_g, w_ffn_gate, w_ffn_up, w_ffn_down, w_router, w_exp_gate, w_exp_up, w_exp_down, final_norm_g):
    b, s, d = x.shape
    n = b * s
    depth = w_in.shape[0]
    lbs = _lower_bounds(hgrn_lb_logits)
    h = x
    for l in range(depth):
        dense = l % 2 == 0
        h2, hn = _mixer(h, norm_mix_g[l], w_in[l], b_fox[l], b_gate[l], lbs[l], hgrn_norm_g[l],
                        w_branch_sb[l], w_branch_fox[l], w_branch_hgrn[l], w_out[l], norm_ffn_g[l],
                        BF16 if dense else F32)
        last = l == depth - 1
        i = l // 2
        if dense:
            h2 = _dense_ffn(hn, h2, w_ffn_gate[i].astype(BF16), w_ffn_up[i].astype(BF16),
                            w_ffn_down[i].astype(BF16), _tile(n, 1024), 768)
            if last:
                h2 = _final_norm(h2, final_norm_g.reshape(1, d), _tile(n, 1024))
        else:
            h2 = _moe(h2, hn, w_router[i], w_exp_gate[i], w_exp_up[i], w_exp_down[i],
                      final_norm_g, last)
        h = h2.reshape(b, s, d)
    return h
```

```python
import functools

import jax
import jax.numpy as jnp
from jax import lax
from jax.experimental import pallas as pl
from jax.experimental.pallas import tpu as pltpu

F32 = jnp.float32
BF16 = jnp.bfloat16
I32 = jnp.int32

RMS_EPS = 1e-6
CHUNK = 64
SUBCHUNK = 16
SB_HEADS = 4
FOX_HEADS = 4
HGRN_HEADS = 4
HEAD_DIM = 64
HGRN_DIM = 128
N_EXPERTS = 8
TOP_K = 2

LANES = 128
SUBLANES = 8
VMEM_LIMIT = 56 * 1024 * 1024

EXP_ZERO_BELOW = -105.0
NEG_BIG = -1e30
NORM_SLACK = 1.0 + 2.0 ** -7
DMA_LOOP_UNROLL = 8


def _cparams(*sem):
    return pltpu.CompilerParams(dimension_semantics=sem, vmem_limit_bytes=VMEM_LIMIT)


def _dot(a, b):
    return jnp.dot(a, b, preferred_element_type=F32)


def _dot_nt(a, b):
    return lax.dot_general(a, b, (((1,), (1,)), ((), ())), preferred_element_type=F32)


def _dot_tn(a, b):
    return lax.dot_general(a, b, (((0,), (0,)), ((), ())), preferred_element_type=F32)


def _split2(x):
    hi = x.astype(BF16)
    lo = (x - hi.astype(F32)).astype(BF16)
    return hi, lo


def _split3(x):
    hi = x.astype(BF16)
    r = x - hi.astype(F32)
    mid = r.astype(BF16)
    lo = (r - mid.astype(F32)).astype(BF16)
    return hi, mid, lo


def _log_sigmoid(x):
    return jnp.minimum(x, 0.0) - jnp.log(1.0 + jnp.exp(-jnp.abs(x)))


def _rmsnorm(x, g):
    ms = jnp.mean(x * x, axis=-1, keepdims=True)
    return x * lax.rsqrt(ms + RMS_EPS) * g


def _tri(n, fn, dtype=BF16):
    r = lax.broadcasted_iota(I32, (n, n), 0)
    c = lax.broadcasted_iota(I32, (n, n), 1)
    return fn(r, c).astype(dtype)


def _lb_kernel(x_ref, o_ref):
    x = x_ref[...]
    depth = x.shape[0]
    m = jnp.max(x, axis=0, keepdims=True)
    e = jnp.exp(x - m)
    sm = e / jnp.sum(e, axis=0, keepdims=True)
    run = sm[0:1]
    first = run
    o_ref[0:1, :] = run - first
    for l in range(1, depth):
        run = run + sm[l:l + 1]
        o_ref[l:l + 1, :] = run - first


def _lower_bounds(logits):
    return pl.pallas_call(
        _lb_kernel,
        out_shape=jax.ShapeDtypeStruct(logits.shape, F32),
        name="hgrn_lower_bounds",
    )(logits.astype(F32))


def _norm_matmul_kernel(x_ref, g_ref, w_ref, wvt_ref, o_ref, vt_ref, xn_ref):
    @pl.when(pl.program_id(1) == 0)
    def _():
        xn = _rmsnorm(x_ref[...], g_ref[...]).astype(BF16)
        xn_ref[...] = xn
        vt_ref[0] = _dot_nt(wvt_ref[...], xn).astype(vt_ref.dtype)

    o_ref[...] = _dot(xn_ref[...], w_ref[...]).astype(o_ref.dtype)


def _norm_matmul(x, g, w, wvt, seq, tm, tn):
    n, d = x.shape
    nout = w.shape[1]
    nv = wvt.shape[0]
    per = seq // tm
    return pl.pallas_call(
        _norm_matmul_kernel,
        grid=(n // tm, nout // tn),
        in_specs=[
            pl.BlockSpec((tm, d), lambda i, j: (i, 0)),
            pl.BlockSpec((1, d), lambda i, j: (0, 0)),
            pl.BlockSpec((d, tn), lambda i, j: (0, j)),
            pl.BlockSpec((nv, d), lambda i, j: (0, 0)),
        ],
        out_specs=[
            pl.BlockSpec((tm, tn), lambda i, j: (i, j)),
            pl.BlockSpec((1, nv, tm), lambda i, j: (i // per, 0, i % per)),
        ],
        out_shape=[
            jax.ShapeDtypeStruct((n, nout), BF16),
            jax.ShapeDtypeStruct((n // seq, nv, seq), BF16),
        ],
        scratch_shapes=[pltpu.VMEM((tm, d), BF16)],
        compiler_params=_cparams("parallel", "arbitrary"),
        name="norm_in_proj",
    )(x, g, w, wvt)


def _gate_cum_kernel(x_ref, g_ref, wf_ref, wc_ref, wr_ref, bc_ref, br_ref, lt_ref, ut_ref,
                     f_ref, cc_ref, cr_ref, carc_ref, carr_ref):
    tm = x_ref.shape[1]

    @pl.when(pl.program_id(1) == 0)
    def _():
        carc_ref[...] = jnp.zeros_like(carc_ref)
        carr_ref[...] = jnp.zeros_like(carr_ref)

    xn = _rmsnorm(x_ref[0], g_ref[...]).astype(BF16)
    f_ref[0] = _dot(xn, wf_ref[...])

    lc = _log_sigmoid(_dot(xn, wc_ref[...]) + bc_ref[...])
    lt = lt_ref[...]
    c_hi, c_mid, c_lo = _split3(lc)
    cum_c = _dot(lt, c_hi) + _dot(lt, c_mid) + _dot(lt, c_lo) + carc_ref[0:1, :]
    cc_ref[0] = cum_c
    carc_ref[...] = jnp.broadcast_to(cum_c[tm - 1:tm, :], carc_ref.shape)

    lr = _log_sigmoid(_dot_nt(wr_ref[...], xn) + br_ref[:, 0:1])
    ut = ut_ref[...]
    r_hi, r_mid, r_lo = _split3(lr)
    cum_r = _dot(r_hi, ut) + _dot(r_mid, ut) + _dot(r_lo, ut) + carr_ref[:, 0:1]
    cr_ref[0] = cum_r
    carr_ref[...] = jnp.broadcast_to(cum_r[:, tm - 1:tm], carr_ref.shape)


def _gate_cum(h, g, wf, wc, wr, bc, br, tm):
    b, s, d = h.shape
    nf = wf.shape[1]
    lt = _tri(tm, lambda r, c: c <= r)
    ut = _tri(tm, lambda r, c: r <= c)
    full = lambda shape: pl.BlockSpec(shape, lambda bi, si: (0,) * len(shape))
    return pl.pallas_call(
        _gate_cum_kernel,
        grid=(b, s // tm),
        in_specs=[
            pl.BlockSpec((1, tm, d), lambda bi, si: (bi, si, 0)),
            full((1, d)), full((d, nf)), full((d, LANES)), full((SUBLANES, d)),
            full((1, LANES)), full((SUBLANES, LANES)), full((tm, tm)), full((tm, tm)),
        ],
        out_specs=[
            pl.BlockSpec((1, tm, nf), lambda bi, si: (bi, si, 0)),
            pl.BlockSpec((1, tm, LANES), lambda bi, si: (bi, si, 0)),
            pl.BlockSpec((1, SUBLANES, tm), lambda bi, si: (bi, 0, si)),
        ],
        out_shape=[
            jax.ShapeDtypeStruct((b, s, nf), F32),
            jax.ShapeDtypeStruct((b, s, LANES), F32),
            jax.ShapeDtypeStruct((b, SUBLANES, s), F32),
        ],
        scratch_shapes=[pltpu.VMEM((SUBLANES, LANES), F32), pltpu.VMEM((SUBLANES, LANES), F32)],
        compiler_params=_cparams("parallel", "arbitrary"),
        name="gate_proj_cum",
    )(h, g, wf, wc, wr, bc, br, lt, ut)


def _masked_queries(q_ref, qm_ref, nheads, hd, scale):
    q = q_ref[0]
    lane = lax.broadcasted_iota(I32, q.shape, 1)
    qs = q * jnp.asarray(scale, q.dtype)
    for h in range(nheads):
        sel = jnp.logical_and(lane >= h * hd, lane < (h + 1) * hd)
        qm_ref[h] = jnp.where(sel, qs, jnp.zeros_like(qs))


def _sb_kernel(q_ref, k_ref, vt_ref, u_ref, o_ref, qm_ref, car_ref, acc_ref, e_ref, tot_ref,
               *, tq, tk, nheads, hd, vrow0):
    i = pl.program_id(1)
    _masked_queries(q_ref, qm_ref, nheads, hd, hd ** -0.5)
    car_ref[...] = jnp.zeros_like(car_ref)
    acc_ref[...] = jnp.zeros_like(acc_ref)
    u = u_ref[...]

    def log_pass(j, slot, diag, valid=None):
        start = pl.multiple_of(j * tk, tk)
        kb = k_ref[0, pl.ds(start, tk), :]
        if diag:
            srow = start + lax.broadcasted_iota(I32, (tk, tq), 0)
            tcol = i * tq + lax.broadcasted_iota(I32, (tk, tq), 1)
            strict = srow < tcol
        for h in range(nheads):
            z = _dot_nt(kb, qm_ref[h])
            lg = -(jnp.maximum(z, 0.0) + jnp.log(1.0 + jnp.exp(-jnp.abs(z))))
            if diag:
                lg = jnp.where(strict, lg, 0.0)
            l_hi, l_lo = _split2(lg)
            suf = _dot(u, l_hi) + _dot(u, l_lo)
            e = z + lg + suf
            tot = suf[0:1, :] + lg[0:1, :]
            if diag:
                e = jnp.where(strict, e, NEG_BIG)
            if valid is not None:
                e = jnp.where(valid, e, NEG_BIG)
                tot = jnp.where(valid, tot, 0.0)
            e_ref[slot * nheads + h] = e
            tot_ref[slot * nheads + h] = tot

    def value_pass(j, slot):
        start = pl.multiple_of(j * tk, tk)
        alive = None
        for h in range(nheads):
            carry = car_ref[h]
            w = jnp.exp(e_ref[slot * nheads + h] + carry)
            vt = vt_ref[0, vrow0 + h * hd:vrow0 + (h + 1) * hd, pl.ds(start, tk)]
            acc_ref[h] += _dot(vt, w.astype(BF16))
            carry = carry + tot_ref[slot * nheads + h]
            car_ref[h] = carry
            top = jnp.max(carry)
            alive = top if alive is None else jnp.maximum(alive, top)
        return alive

    j0 = (i * tq) // tk
    j1 = jnp.maximum(j0 - 1, 0)
    log_pass(j0, 0, True)
    log_pass(j1, 1, False, valid=j0 >= 1)
    value_pass(j0, 0)
    alive0 = value_pass(j1, 1)

    def cond(c):
        j, alive = c
        return jnp.logical_and(j >= 0, alive > EXP_ZERO_BELOW)

    def body(c):
        j, _ = c
        log_pass(j, 0, False)
        return j - 1, value_pass(j, 0)

    lax.while_loop(cond, body, (j0 - 2, alive0))
    o_ref[0] = acc_ref[...].reshape(nheads * hd, tq).T.astype(o_ref.dtype)


def _sb_attention(proj, vt, col0, vrow0, tq, tk):
    b, s, _ = proj.shape
    width = SB_HEADS * HEAD_DIM
    nvt = vt.shape[1]
    u = _tri(tk, lambda r, c: c > r)
    kern = functools.partial(_sb_kernel, tq=tq, tk=tk, nheads=SB_HEADS, hd=HEAD_DIM, vrow0=vrow0)
    return pl.pallas_call(
        kern,
        grid=(b, s // tq),
        in_specs=[
            pl.BlockSpec((1, tq, width), lambda bi, i: (bi, i, col0)),
            pl.BlockSpec((1, s, width), lambda bi, i: (bi, 0, col0 + 1)),
            pl.BlockSpec((1, nvt, s), lambda bi, i: (bi, 0, 0)),
            pl.BlockSpec((tk, tk), lambda bi, i: (0, 0)),
        ],
        out_specs=pl.BlockSpec((1, tq, width), lambda bi, i: (bi, i, 0)),
        out_shape=jax.ShapeDtypeStruct((b, s, width), BF16),
        scratch_shapes=[pltpu.VMEM((SB_HEADS, tq, width), BF16),
                        pltpu.VMEM((SB_HEADS, 1, tq), F32),
                        pltpu.VMEM((SB_HEADS, HEAD_DIM, tq), F32),
                        pltpu.VMEM((2 * SB_HEADS, tk, tq), F32),
                        pltpu.VMEM((2 * SB_HEADS, 1, tq), F32)],
        compiler_params=_cparams("parallel", "arbitrary"),
        name="stick_breaking_attention",
    )(proj, proj, vt, u)


def _fox_kernel(q_ref, k_ref, vt_ref, cc_ref, cr_ref, kmax_ref, cend_ref, et_ref, o_ref,
                qm_ref, s_ref, mx_ref, m_ref, l_ref, acc_ref, qn_ref, *, tq, tk, nheads, hd, vrow0):
    i = pl.program_id(1)
    _masked_queries(q_ref, qm_ref, nheads, hd, hd ** -0.5)
    m_ref[...] = jnp.full(m_ref.shape, NEG_BIG, F32)
    l_ref[...] = jnp.zeros_like(l_ref)
    acc_ref[...] = jnp.zeros_like(acc_ref)

    def score_pass(j, slot, diag):
        start = pl.multiple_of(j * tk, tk)
        kb = k_ref[0, pl.ds(start, tk), :]
        if diag:
            srow = start + lax.broadcasted_iota(I32, (tk, tq), 0)
            tcol = i * tq + lax.broadcasted_iota(I32, (tk, tq), 1)
            causal = srow <= tcol
        for h in range(nheads):
            cs = cc_ref[0, pl.ds(start, tk), h:h + 1]
            s = _dot_nt(kb, qm_ref[h]) - cs
            if diag:
                s = jnp.where(causal, s, NEG_BIG)
            s_ref[slot * nheads + h] = s
            mx_ref[slot * nheads + h] = jnp.max(s, axis=0, keepdims=True)

    def value_pass(j, slot):
        start = pl.multiple_of(j * tk, tk)
        for h in range(nheads):
            ct = cr_ref[0, h:h + 1, :]
            m = m_ref[h]
            m_new = jnp.maximum(m, mx_ref[slot * nheads + h] + ct)
            alpha = jnp.exp(m - m_new)
            p = jnp.exp(s_ref[slot * nheads + h] - (m_new - ct))
            l_ref[h] = alpha * l_ref[h] + jnp.sum(p, axis=0, keepdims=True)
            vt = vt_ref[0, vrow0 + h * hd:vrow0 + (h + 1) * hd, pl.ds(start, tk)]
            acc_ref[h] = alpha * acc_ref[h] + _dot(vt, p.astype(BF16))
            m_ref[h] = m_new

    qf = q_ref[0].astype(F32) * (hd ** -0.5)
    sq_hi, sq_lo = _split2(qf * qf)
    et = et_ref[...]
    qn_ref[...] = jnp.sqrt(_dot_nt(et, sq_hi) + _dot_nt(et, sq_lo)) * NORM_SLACK

    def bound_below(j):
        sel = lax.broadcasted_iota(I32, (SUBLANES, LANES), 1) == j
        kmax = jnp.max(jnp.where(sel, kmax_ref[0], 0.0), axis=1, keepdims=True)
        cend = jnp.sum(jnp.where(sel, cend_ref[0], 0.0), axis=1, keepdims=True)
        best = None
        for h in range(nheads):
            b = (qn_ref[h:h + 1, :] * kmax[h:h + 1, :] + cr_ref[0, h:h + 1, :]
                 - cend[h:h + 1, :] - m_ref[h])
            best = b if best is None else jnp.maximum(best, b)
        return jnp.max(best)

    j0 = (i * tq) // tk
    score_pass(j0, 0, True)

    def cond(c):
        it, alive = c
        return jnp.logical_and(it < j0, alive > EXP_ZERO_BELOW)

    def body(c):
        it, _ = c
        slot = lax.rem(it, 2)
        score_pass(j0 - 1 - it, 1 - slot, False)
        value_pass(j0 - it, slot)
        return it + 1, bound_below(j0 - 2 - it)

    done, _ = lax.while_loop(cond, body, (jnp.int32(0), jnp.float32(0.0)))
    value_pass(j0 - done, lax.rem(done, 2))
    out = acc_ref[...] / l_ref[...]
    o_ref[0] = out.reshape(nheads * hd, tq).T.astype(o_ref.dtype)


def _head_selector(nheads, hd):
    r = lax.broadcasted_iota(I32, (SUBLANES, nheads * hd), 0)
    c = lax.broadcasted_iota(I32, (SUBLANES, nheads * hd), 1)
    return (c // hd == r).astype(BF16)


def _key_bounds_kernel(k_ref, et_ref, o_ref, run_ref):
    j = pl.program_id(1)

    @pl.when(j == 0)
    def _():
        run_ref[...] = jnp.zeros_like(run_ref)
        o_ref[...] = jnp.zeros_like(o_ref)

    kf = k_ref[0].astype(F32)
    sq_hi, sq_lo = _split2(kf * kf)
    et = et_ref[...]
    n2 = _dot_nt(et, sq_hi) + _dot_nt(et, sq_lo)
    tile_max = jnp.sqrt(jnp.max(n2, axis=1, keepdims=True)) * NORM_SLACK
    run = jnp.maximum(run_ref[...], tile_max)
    run_ref[...] = run
    lane = lax.broadcasted_iota(I32, (SUBLANES, LANES), 1)
    o_ref[0] = jnp.where(lane == j, run, o_ref[0])


def _key_bounds(proj, colk, tk):
    b, s, _ = proj.shape
    width = FOX_HEADS * HEAD_DIM
    assert s // tk <= LANES
    return pl.pallas_call(
        _key_bounds_kernel,
        grid=(b, s // tk),
        in_specs=[
            pl.BlockSpec((1, tk, width), lambda bi, j: (bi, j, colk)),
            pl.BlockSpec((SUBLANES, width), lambda bi, j: (0, 0)),
        ],
        out_specs=pl.BlockSpec((1, SUBLANES, LANES), lambda bi, j: (bi, 0, 0)),
        out_shape=jax.ShapeDtypeStruct((b, SUBLANES, LANES), F32),
        scratch_shapes=[pltpu.VMEM((SUBLANES, LANES), F32)],
        compiler_params=_cparams("parallel", "arbitrary"),
        name="fox_key_norm_bounds",
    )(proj, _head_selector(FOX_HEADS, HEAD_DIM))


def _fox_attention(proj, vt, col0, vrow0, cum_col, cum_row, tq, tk):
    b, s, _ = proj.shape
    width = FOX_HEADS * HEAD_DIM
    nvt = vt.shape[1]
    kmax = _key_bounds(proj, col0 + 1, tk)
    cend = jnp.pad(cum_row[:, :, tk - 1::tk], ((0, 0), (0, 0), (0, LANES - s // tk)))
    kern = functools.partial(_fox_kernel, tq=tq, tk=tk, nheads=FOX_HEADS, hd=HEAD_DIM, vrow0=vrow0)
    return pl.pallas_call(
        kern,
        grid=(b, s // tq),
        in_specs=[
            pl.BlockSpec((1, tq, width), lambda bi, i: (bi, i, col0)),
            pl.BlockSpec((1, s, width), lambda bi, i: (bi, 0, col0 + 1)),
            pl.BlockSpec((1, nvt, s), lambda bi, i: (bi, 0, 0)),
            pl.BlockSpec((1, s, LANES), lambda bi, i: (bi, 0, 0)),
            pl.BlockSpec((1, SUBLANES, tq), lambda bi, i: (bi, 0, i)),
            pl.BlockSpec((1, SUBLANES, LANES), lambda bi, i: (bi, 0, 0)),
            pl.BlockSpec((1, SUBLANES, LANES), lambda bi, i: (bi, 0, 0)),
            pl.BlockSpec((SUBLANES, width), lambda bi, i: (0, 0)),
        ],
        out_specs=pl.BlockSpec((1, tq, width), lambda bi, i: (bi, i, 0)),
        out_shape=jax.ShapeDtypeStruct((b, s, width), BF16),
        scratch_shapes=[pltpu.VMEM((FOX_HEADS, tq, width), BF16),
                        pltpu.VMEM((2 * FOX_HEADS, tk, tq), F32),
                        pltpu.VMEM((2 * FOX_HEADS, 1, tq), F32),
                        pltpu.VMEM((FOX_HEADS, 1, tq), F32),
                        pltpu.VMEM((FOX_HEADS, 1, tq), F32),
                        pltpu.VMEM((FOX_HEADS, HEAD_DIM, tq), F32),
                        pltpu.VMEM((SUBLANES, tq), F32)],
        compiler_params=_cparams("parallel", "arbitrary"),
        name="forgetting_attention",
    )(proj, proj, vt, cum_col, cum_row, kmax, cend, _head_selector(FOX_HEADS, HEAD_DIM))


HGRN_SAFE_SPAN = 40.0
HGRN_GROUP = 4


def _hgrn_kernel(q_ref, f_ref, i_ref, g_ref, lb_ref, ng_ref, lt_ref, o_ref,
                 lf_ref, kk_ref, qq_ref, b_ref, oi_ref, st_ref, *, ts, nheads, dk):
    @pl.when(pl.program_id(1) == 0)
    def _():
        st_ref[...] = jnp.zeros_like(st_ref)

    lb = lb_ref[...]
    z = f_ref[0]
    lsz = _log_sigmoid(z)
    a = jnp.log(lb)
    c = jnp.log(1.0 - lb) + lsz
    lf_ref[...] = jnp.maximum(a, c) + jnp.log(1.0 + jnp.exp(-jnp.abs(a - c)))
    kk_ref[...] = (1.0 - lb) * jnp.exp(lsz - z)
    qin = q_ref[0].astype(F32)
    qq_ref[...] = qin / (1.0 + jnp.exp(-qin))

    lt = lt_ref[...]
    nsub = CHUNK // SUBCHUNK
    rr = lax.broadcasted_iota(I32, (SUBCHUNK, SUBCHUNK), 0)
    cc = lax.broadcasted_iota(I32, (SUBCHUNK, SUBCHUNK), 1)
    causal = cc <= rr
    sub = lax.broadcasted_iota(I32, (SUBLANES, SUBCHUNK), 0)
    causal_c = (lax.broadcasted_iota(I32, (CHUNK, CHUNK), 1)
                <= lax.broadcasted_iota(I32, (CHUNK, CHUNK), 0))
    row_blk = lax.broadcasted_iota(I32, (CHUNK, dk), 0) // SUBCHUNK
    ng = ng_ref[...]

    def prepare(r0):
        rows = pl.ds(r0, CHUNK)
        h_hi, h_mid, h_lo = _split3(lf_ref[rows, :])
        bcum = _dot(lt, h_hi) + _dot(lt, h_mid) + _dot(lt, h_lo)
        b_ref[rows, :] = bcum
        span = -bcum[SUBCHUNK - 1:SUBCHUNK, :]
        for bi in range(1, nsub):
            lo = bi * SUBCHUNK
            span = jnp.maximum(span, bcum[lo - 1:lo, :] - bcum[lo + SUBCHUNK - 1:lo + SUBCHUNK, :])
        return (r0, bcum, kk_ref[rows, :], qq_ref[rows, :], i_ref[0, rows, :]), span

    def intra_exact(ck, h, bi):
        r0, bcum, kk, qq, vv = ck
        hs = slice(h * dk, (h + 1) * dk)
        lo = bi * SUBCHUNK
        hi = lo + SUBCHUNK
        b_i = bcum[lo:hi, hs]
        kk_i = kk[lo:hi, hs]
        groups = []
        for gi in range(SUBCHUNK // SUBLANES):
            g0 = lo + gi * SUBLANES
            qg = qq[g0:g0 + SUBLANES, hs].astype(BF16)
            dg = jnp.zeros((SUBLANES, SUBCHUNK), F32)
            for r in range(SUBLANES):
                bt = bcum[g0 + r:g0 + r + 1, hs]
                ke = kk_i * jnp.exp(jnp.minimum(bt - b_i, 0.0))
                res = _dot_nt(qg, ke.astype(BF16))
                dg = jnp.where(sub == r, res, dg)
            groups.append(dg)
        dblk = jnp.where(causal, jnp.concatenate(groups, axis=0), 0.0)
        o_i = _dot(dblk.astype(BF16), vv[lo:hi, hs])
        if bi > 0:
            bref = bcum[lo - 1:lo, hs]
            q_i = (qq[lo:hi, hs] * jnp.exp(b_i - bref)).astype(BF16)
            k_p = (kk[0:lo, hs] * jnp.exp(bref - bcum[0:lo, hs])).astype(BF16)
            o_i = o_i + _dot(_dot_nt(q_i, k_p).astype(BF16), vv[0:lo, hs])
        return o_i

    def intra_fast(ck, h):
        r0, bcum, kk, qq, vv = ck
        hs = slice(h * dk, (h + 1) * dk)
        bh = bcum[:, hs]
        refs = [jnp.zeros((1, dk), F32)] + [bh[bi * SUBCHUNK - 1:bi * SUBCHUNK, :] for bi in range(1, nsub)]
        ref_rows = jnp.concatenate([jnp.broadcast_to(r, (SUBCHUNK, dk)) for r in refs], axis=0)
        q_stack = qq[:, hs] * jnp.exp(bh - ref_rows)
        q_big = jnp.concatenate(
            [jnp.where(row_blk == bi, q_stack, 0.0) for bi in range(nsub)], axis=1).astype(BF16)
        k_big = jnp.concatenate(
            [kk[:, hs] * jnp.exp(jnp.minimum(r - bh, HGRN_SAFE_SPAN)) for r in refs], axis=1).astype(BF16)
        s_all = jnp.where(causal_c, _dot_nt(q_big, k_big), 0.0)
        return _dot(s_all.astype(BF16), vv[:, hs])

    def phase1(gi, carry):
        cks, spans = [], None
        for c in range(HGRN_GROUP):
            ck, span = prepare(pl.multiple_of((gi * HGRN_GROUP + c) * CHUNK, CHUNK))
            cks.append(ck)
            spans = span if spans is None else jnp.maximum(spans, span)
        safe = jnp.max(spans) < HGRN_SAFE_SPAN

        @pl.when(safe)
        def _():
            for ck in cks:
                for h in range(nheads):
                    oi_ref[pl.ds(ck[0], CHUNK), h * dk:(h + 1) * dk] = intra_fast(ck, h)

        @pl.when(jnp.logical_not(safe))
        def _():
            for ck in cks:
                for h in range(nheads):
                    for bi in range(nsub):
                        oi_ref[pl.ds(ck[0] + bi * SUBCHUNK, SUBCHUNK), h * dk:(h + 1) * dk] = (
                            intra_exact(ck, h, bi))
        return carry

    lax.fori_loop(0, ts // (CHUNK * HGRN_GROUP), phase1, 0)

    def phase2(ci, carry):
        r0 = pl.multiple_of(ci * CHUNK, CHUNK)
        rows = pl.ds(r0, CHUNK)
        for h in range(nheads):
            hs = slice(h * dk, (h + 1) * dk)
            st = st_ref[h]
            bh = b_ref[rows, hs]
            qq = qq_ref[rows, hs]
            kk = kk_ref[rows, hs]
            vv = i_ref[0, rows, hs]
            qe = (qq * jnp.exp(bh)).astype(BF16)
            o = oi_ref[rows, hs] + _dot_nt(qe, st.astype(BF16))
            b_last = bh[CHUNK - 1:CHUNK, :]
            kd = (kk * jnp.exp(b_last - bh)).astype(BF16)
            st_ref[h] = st * jnp.exp(b_last) + _dot_tn(vv, kd)
            gate = g_ref[0, rows, hs].astype(F32)
            on = _rmsnorm(o, ng) * (gate / (1.0 + jnp.exp(-gate)))
            o_ref[0, rows, hs] = on.astype(o_ref.dtype)
        return carry

    lax.fori_loop(0, ts // CHUNK, phase2, 0, unroll=HGRN_GROUP)


def _hgrn(proj, f_hgrn, lb, ng, colq, coli, colg, ts):
    b, s, _ = proj.shape
    dk = HGRN_DIM
    hw = HGRN_HEADS * dk
    lt = _tri(CHUNK, lambda r, c: c <= r)
    kern = functools.partial(_hgrn_kernel, ts=ts, nheads=HGRN_HEADS, dk=dk)
    return pl.pallas_call(
        kern,
        grid=(b, s // ts),
        in_specs=[
            pl.BlockSpec((1, ts, hw), lambda bi, si: (bi, si, colq)),
            pl.BlockSpec((1, ts, hw), lambda bi, si: (bi, si, 0)),
            pl.BlockSpec((1, ts, hw), lambda bi, si: (bi, si, coli)),
            pl.BlockSpec((1, ts, hw), lambda bi, si: (bi, si, colg)),
            pl.BlockSpec((1, hw), lambda bi, si: (0, 0)),
            pl.BlockSpec((1, dk), lambda bi, si: (0, 0)),
            pl.BlockSpec((CHUNK, CHUNK), lambda bi, si: (0, 0)),
        ],
        out_specs=pl.BlockSpec((1, ts, hw), lambda bi, si: (bi, si, 0)),
        out_shape=jax.ShapeDtypeStruct((b, s, hw), BF16),
        scratch_shapes=[
            pltpu.VMEM((ts, hw), F32), pltpu.VMEM((ts, hw), F32), pltpu.VMEM((ts, hw), F32),
            pltpu.VMEM((ts, hw), F32), pltpu.VMEM((ts, hw), F32),
            pltpu.VMEM((HGRN_HEADS, dk, dk), F32),
        ],
        compiler_params=_cparams("parallel", "arbitrary"),
        name="hgrn2_recurrence",
    )(proj, f_hgrn, proj, proj, lb, ng, lt)


def _merge_kernel(h_ref, oa_ref, ob_ref, oc_ref, g0_ref, g1_ref, g2_ref, bg_ref,
                  wa_ref, wb_ref, wc_ref, wo_ref, ng_ref, hout_ref, hn_ref):
    def gate(gref, k):
        x = gref[...].astype(F32) + bg_ref[k:k + 1, :]
        return 1.0 / (1.0 + jnp.exp(-x))

    merged = (gate(g0_ref, 0) * _dot(oa_ref[...], wa_ref[...])
              + gate(g1_ref, 1) * _dot(ob_ref[...], wb_ref[...])
              + gate(g2_ref, 2) * _dot(oc_ref[...], wc_ref[...]))
    hnew = h_ref[...] + _dot(merged.astype(BF16), wo_ref[...])
    hout_ref[...] = hnew
    hn_ref[...] = _rmsnorm(hnew, ng_ref[...]).astype(hn_ref.dtype)


def _merge_out(h, oa, ob, oc, proj, gcol0, bg, wa, wb, wc, wo, ng, hn_dtype, tm):
    n, d = h.shape
    full = lambda shape: pl.BlockSpec(shape, lambda i: (0,) * len(shape))
    rows = lambda w: pl.BlockSpec((tm, w), lambda i: (i, 0))
    return pl.pallas_call(
        _merge_kernel,
        grid=(n // tm,),
        in_specs=[
            rows(d), rows(oa.shape[1]), rows(ob.shape[1]), rows(oc.shape[1]),
            pl.BlockSpec((tm, d), lambda i: (i, gcol0)),
            pl.BlockSpec((tm, d), lambda i: (i, gcol0 + 1)),
            pl.BlockSpec((tm, d), lambda i: (i, gcol0 + 2)),
            full(bg.shape), full(wa.shape), full(wb.shape), full(wc.shape), full(wo.shape),
            full((1, d)),
        ],
        out_specs=[rows(d), rows(d)],
        out_shape=[jax.ShapeDtypeStruct((n, d), F32), jax.ShapeDtypeStruct((n, d), hn_dtype)],
        compiler_params=_cparams("parallel"),
        name="merge_out_proj",
    )(h, oa, ob, oc, proj, proj, proj, bg, wa, wb, wc, wo, ng)


def _ffn_kernel(hn_ref, h_ref, wg_ref, wu_ref, wd_ref, o_ref, acc_ref):
    j = pl.program_id(1)

    @pl.when(j == 0)
    def _():
        acc_ref[...] = jnp.zeros_like(acc_ref)

    x = hn_ref[...]
    gt = _dot(x, wg_ref[...])
    up = _dot(x, wu_ref[...])
    act = (gt / (1.0 + jnp.exp(-gt))) * up
    acc_ref[...] += _dot(act.astype(BF16), wd_ref[...])

    @pl.when(j == pl.num_programs(1) - 1)
    def _():
        o_ref[...] = h_ref[...] + acc_ref[...]


def _dense_ffn(hn, h, wg, wu, wd, tm, tf):
    n, d = h.shape
    ff = wg.shape[1]
    return pl.pallas_call(
        _ffn_kernel,
        grid=(n // tm, ff // tf),
        in_specs=[
            pl.BlockSpec((tm, d), lambda i, j: (i, 0)),
            pl.BlockSpec((tm, d), lambda i, j: (i, 0)),
            pl.BlockSpec((d, tf), lambda i, j: (0, j)),
            pl.BlockSpec((d, tf), lambda i, j: (0, j)),
            pl.BlockSpec((tf, d), lambda i, j: (j, 0)),
        ],
        out_specs=pl.BlockSpec((tm, d), lambda i, j: (i, 0)),
        out_shape=jax.ShapeDtypeStruct((n, d), F32),
        scratch_shapes=[pltpu.VMEM((tm, d), F32)],
        compiler_params=_cparams("parallel", "arbitrary"),
        name="dense_swiglu",
    )(hn, h, wg, wu, wd)


def _router_kernel(hn_ref, wr_ref, u_ref, idx_ref, wt_ref, cnt_ref, car_ref):
    t = hn_ref.shape[0]

    @pl.when(pl.program_id(0) == 0)
    def _():
        car_ref[...] = jnp.zeros_like(car_ref)

    logits = _dot_nt(wr_ref[...], hn_ref[...].astype(BF16))
    eidx = lax.broadcasted_iota(I32, (N_EXPERTS, t), 0)
    m1 = jnp.max(logits, axis=0, keepdims=True)
    i1 = jnp.min(jnp.where(logits == m1, eidx, N_EXPERTS), axis=0, keepdims=True)
    sel1 = eidx == i1
    rest = jnp.where(sel1, -jnp.inf, logits)
    m2 = jnp.max(rest, axis=0, keepdims=True)
    i2 = jnp.min(jnp.where(rest == m2, eidx, N_EXPERTS), axis=0, keepdims=True)
    sel2 = eidx == i2
    e21 = jnp.exp(m2 - m1)
    w1 = 1.0 / (1.0 + e21)
    w2 = e21 / (1.0 + e21)

    cnt = jnp.where(jnp.logical_or(sel1, sel2), 1.0, 0.0)
    excl = _dot(cnt.astype(BF16), u_ref[...])
    rank = excl + car_ref[:, 0:1]
    p1 = jnp.sum(jnp.where(sel1, rank, 0.0), axis=0, keepdims=True)
    p2 = jnp.sum(jnp.where(sel2, rank, 0.0), axis=0, keepdims=True)
    total = rank[:, t - 1:t] + cnt[:, t - 1:t]
    car_ref[...] = jnp.broadcast_to(total, car_ref.shape)
    cnt_ref[...] = jnp.broadcast_to(total, cnt_ref.shape)

    zi = jnp.zeros((1, t), I32)
    idx_ref[...] = jnp.concatenate(
        [i1, i2, p1.astype(I32), p2.astype(I32), zi, zi, zi, zi], axis=0)
    zf = jnp.zeros((1, t), F32)
    wt_ref[...] = jnp.concatenate([w1, w2, zf, zf, zf, zf, zf, zf], axis=0)


def _router(hn, wr_t, t):
    n, d = hn.shape
    u = _tri(t, lambda r, c: r < c)
    return pl.pallas_call(
        _router_kernel,
        grid=(n // t,),
        in_specs=[
            pl.BlockSpec((t, d), lambda i: (i, 0)),
            pl.BlockSpec((N_EXPERTS, d), lambda i: (0, 0)),
            pl.BlockSpec((t, t), lambda i: (0, 0)),
        ],
        out_specs=[
            pl.BlockSpec((SUBLANES, t), lambda i: (0, i)),
            pl.BlockSpec((SUBLANES, t), lambda i: (0, i)),
            pl.BlockSpec((N_EXPERTS, LANES), lambda i: (0, 0)),
        ],
        out_shape=[
            jax.ShapeDtypeStruct((SUBLANES, n), I32),
            jax.ShapeDtypeStruct((SUBLANES, n), F32),
            jax.ShapeDtypeStruct((N_EXPERTS, LANES), F32),
        ],
        scratch_shapes=[pltpu.VMEM((N_EXPERTS, LANES), F32)],
        compiler_params=_cparams("arbitrary"),
        name="moe_router_top2",
    )(hn, wr_t, u)


def _dest_kernel(idx_ref, cnt_ref, dest_ref, blk_ref, *, rblk):
    n = idx_ref.shape[1]
    nb = blk_ref.shape[1]
    shift = rblk.bit_length() - 1
    assert rblk == 1 << shift
    e1 = idx_ref[0:1, :]
    e2 = idx_ref[1:2, :]
    d1 = idx_ref[2:3, :]
    d2 = idx_ref[3:4, :]
    bstart = lax.broadcasted_iota(I32, (1, nb), 1) * rblk
    bexp = jnp.zeros((1, nb), I32)
    pend = jnp.zeros((1, 1), I32)
    for e in range(N_EXPERTS):
        c = cnt_ref[e:e + 1, 0:1].astype(I32)
        padded = lax.shift_left(lax.shift_right_logical(c + (rblk - 1), shift), shift)
        pstart = pend
        pend = pend + padded
        d1 = d1 + jnp.where(e1 == e, pstart, 0)
        d2 = d2 + jnp.where(e2 == e, pstart, 0)
        bexp = bexp + jnp.where(bstart >= pend, 1, 0)
    zi = jnp.zeros((1, n), I32)
    dest_ref[...] = jnp.concatenate([d1, d2, zi, zi, zi, zi, zi, zi], axis=0)
    zb = jnp.zeros((1, nb), I32)
    nvalid = jnp.broadcast_to(lax.shift_right_logical(pend, shift), (1, nb))
    blk_ref[...] = jnp.concatenate(
        [jnp.minimum(bexp, N_EXPERTS - 1), nvalid, zb, zb, zb, zb, zb, zb], axis=0)


def _dest(idx, cnt, rblk, nb_pad):
    n = idx.shape[1]
    return pl.pallas_call(
        functools.partial(_dest_kernel, rblk=rblk),
        out_shape=[
            jax.ShapeDtypeStruct((SUBLANES, n), I32),
            jax.ShapeDtypeStruct((SUBLANES, nb_pad), I32),
        ],
        compiler_params=pltpu.CompilerParams(vmem_limit_bytes=VMEM_LIMIT),
        name="moe_slot_assign",
    )(idx, cnt)


def _dispatch_kernel(d1_ref, d2_ref, x_ref, buf_in, buf_hbm, sem, *, t):
    del buf_in

    def row_copy(src_row, dst_row):
        return pltpu.make_async_copy(x_ref.at[pl.ds(src_row, 1)], buf_hbm.at[pl.ds(dst_row, 1)], sem)

    def start(r, c):
        row_copy(r, d1_ref[r]).start()
        row_copy(r, d2_ref[r]).start(priority=1)
        return c

    lax.fori_loop(0, t, start, 0, unroll=DMA_LOOP_UNROLL)

    def wait(r, c):
        row_copy(r, d1_ref[r]).wait()
        row_copy(r, d2_ref[r]).wait()
        return c

    lax.fori_loop(0, t, wait, 0, unroll=DMA_LOOP_UNROLL)


def _dispatch(x, d1, d2, nrows, t):
    n, d = x.shape
    buf0 = jnp.zeros((nrows, d), x.dtype)
    smem = lambda: pl.BlockSpec((t,), lambda i: (i,), memory_space=pltpu.SMEM)
    return pl.pallas_call(
        functools.partial(_dispatch_kernel, t=t),
        grid=(n // t,),
        in_specs=[smem(), smem(),
                  pl.BlockSpec((t, d), lambda i: (i, 0)), pl.BlockSpec(memory_space=pl.ANY)],
        out_specs=pl.BlockSpec(memory_space=pl.ANY),
        out_shape=jax.ShapeDtypeStruct((nrows, d), x.dtype),
        scratch_shapes=[pltpu.SemaphoreType.DMA(())],
        input_output_aliases={3: 0},
        compiler_params=_cparams("arbitrary"),
        name="moe_dispatch_rows",
    )(d1, d2, x, buf0)


def _expert_kernel(be_ref, nv_ref, x_ref, wg_ref, wu_ref, wd_ref, o_ref, xb_ref):
    i = pl.program_id(0)
    j = pl.program_id(1)

    @pl.when(j == 0)
    def _():
        o_ref[...] = jnp.zeros_like(o_ref)
        xb_ref[...] = x_ref[...].astype(BF16)

    @pl.when(i < nv_ref[0])
    def _():
        x = xb_ref[...]
        gt = _dot(x, wg_ref[0])
        up = _dot(x, wu_ref[0])
        act = (gt / (1.0 + jnp.exp(-gt))) * up
        o_ref[...] += _dot(act.astype(BF16), wd_ref[0])


def _experts(buf, be, nv, wg, wu, wd, rblk, tf):
    nrows, d = buf.shape
    ff = wg.shape[2]
    nj = ff // tf

    def wcol(i, j, be_ref, nv_ref):
        return (be_ref[i], 0, jnp.where(i < nv_ref[0], j, nj - 1))

    def wrow(i, j, be_ref, nv_ref):
        return (be_ref[i], jnp.where(i < nv_ref[0], j, nj - 1), 0)

    grid_spec = pltpu.PrefetchScalarGridSpec(
        num_scalar_prefetch=2,
        grid=(nrows // rblk, nj),
        in_specs=[
            pl.BlockSpec((rblk, d), lambda i, j, be_ref, nv_ref: (i, 0)),
            pl.BlockSpec((1, d, tf), wcol),
            pl.BlockSpec((1, d, tf), wcol),
            pl.BlockSpec((1, tf, d), wrow),
        ],
        out_specs=pl.BlockSpec((rblk, d), lambda i, j, be_ref, nv_ref: (i, 0)),
        scratch_shapes=[pltpu.VMEM((rblk, d), BF16)],
    )
    return pl.pallas_call(
        _expert_kernel,
        grid_spec=grid_spec,
        out_shape=jax.ShapeDtypeStruct((nrows, d), F32),
        compiler_params=_cparams("parallel", "arbitrary"),
        name="moe_grouped_swiglu",
    )(be, nv, buf, wg, wu, wd)


def _combine_kernel(d1_ref, d2_ref, yb_hbm, h_ref, wt_ref, ng_ref, o_ref, y_ref, sem, *, t, final):
    def row_copy(slot, dst_row, src_row):
        return pltpu.make_async_copy(yb_hbm.at[pl.ds(src_row, 1)],
                                     y_ref.at[slot, pl.ds(dst_row, 1)], sem)

    def start(r, c):
        row_copy(0, r, d1_ref[r]).start()
        row_copy(1, r, d2_ref[r]).start(priority=1)
        return c

    lax.fori_loop(0, t, start, 0, unroll=DMA_LOOP_UNROLL)

    def wait(r, c):
        row_copy(0, r, d1_ref[r]).wait()
        row_copy(1, r, d2_ref[r]).wait()
        return c

    lax.fori_loop(0, t, wait, 0, unroll=DMA_LOOP_UNROLL)

    rr = lax.broadcasted_iota(I32, (t, t), 0)
    cc = lax.broadcasted_iota(I32, (t, t), 1)
    eye = rr == cc
    w1 = jnp.sum(jnp.where(eye, wt_ref[0:1, :], 0.0), axis=1, keepdims=True)
    w2 = jnp.sum(jnp.where(eye, wt_ref[1:2, :], 0.0), axis=1, keepdims=True)
    out = h_ref[...] + w1 * y_ref[0] + w2 * y_ref[1]
    if final:
        out = _rmsnorm(out, ng_ref[...])
    o_ref[...] = out


def _combine(yb, d1, d2, h, wt, ng, t, final):
    n, d = h.shape
    smem = lambda: pl.BlockSpec((t,), lambda i: (i,), memory_space=pltpu.SMEM)
    return pl.pallas_call(
        functools.partial(_combine_kernel, t=t, final=final),
        grid=(n // t,),
        in_specs=[
            smem(), smem(),
            pl.BlockSpec(memory_space=pl.ANY),
            pl.BlockSpec((t, d), lambda i: (i, 0)),
            pl.BlockSpec((SUBLANES, t), lambda i: (0, i)),
            pl.BlockSpec((1, d), lambda i: (0, 0)),
        ],
        out_specs=pl.BlockSpec((t, d), lambda i: (i, 0)),
        out_shape=jax.ShapeDtypeStruct((n, d), F32),
        scratch_shapes=[pltpu.VMEM((2, t, d), F32), pltpu.SemaphoreType.DMA(())],
        compiler_params=_cparams("arbitrary"),
        name="moe_combine",
    )(d1, d2, yb, h, wt, ng)


def _final_norm_kernel(x_ref, g_ref, o_ref):
    o_ref[...] = _rmsnorm(x_ref[...], g_ref[...])


def _final_norm(h, g, tm):
    n, d = h.shape
    return pl.pallas_call(
        _final_norm_kernel,
        grid=(n // tm,),
        in_specs=[pl.BlockSpec((tm, d), lambda i: (i, 0)), pl.BlockSpec((1, d), lambda i: (0, 0))],
        out_specs=pl.BlockSpec((tm, d), lambda i: (i, 0)),
        out_shape=jax.ShapeDtypeStruct((n, d), F32),
        compiler_params=_cparams("parallel"),
        name="final_rmsnorm",
    )(h, g)


def _tile(n, pref):
    t = min(n, pref)
    assert n % t == 0, (n, pref)
    return t


def _mixer(h, g_mix, w_in, b_fox, b_gate, lb, hgrn_ng, w_sb, w_fox, w_hgrn, w_out, g_ffn, hn_dtype):
    b, s, d = h.shape
    n = b * s
    sbw = SB_HEADS * HEAD_DIM
    foxw = FOX_HEADS * HEAD_DIM
    hw = HGRN_HEADS * HGRN_DIM
    o_fox = 3 * sbw
    o_fb = o_fox + 3 * foxw
    o_hq = o_fb + FOX_HEADS
    o_hf, o_hi, o_hg = o_hq + hw, o_hq + 2 * hw, o_hq + 3 * hw
    o_gate = o_hq + 4 * hw
    w_main = jnp.concatenate(
        [w_in[:, :o_fb], w_in[:, o_hq:o_hf], w_in[:, o_hi:o_gate], w_in[:, o_gate:]], axis=1).astype(BF16)
    w_vt = jnp.concatenate([w_in[:, 2 * sbw:3 * sbw], w_in[:, o_fox + 2 * foxw:o_fb]], axis=1).T.astype(BF16)
    w_f = w_in[:, o_hf:o_hi].astype(BF16)
    w_fb = w_in[:, o_fb:o_hq]
    w_fbc = jnp.pad(w_fb, ((0, 0), (0, LANES - FOX_HEADS))).astype(BF16)
    w_fbr = jnp.pad(w_fb.T, ((0, SUBLANES - FOX_HEADS), (0, 0))).astype(BF16)
    bc = jnp.pad(b_fox, (0, LANES - FOX_HEADS)).reshape(1, LANES)
    br = jnp.broadcast_to(jnp.pad(b_fox, (0, SUBLANES - FOX_HEADS)).reshape(SUBLANES, 1), (SUBLANES, LANES))

    g_mix = g_mix.reshape(1, d)
    tm = _tile(s, 1024)
    proj, vt = _norm_matmul(h.reshape(n, d), g_mix, w_main, w_vt, s, tm, _tile(w_main.shape[1], 2048))
    proj = proj.reshape(b, s, -1)
    f_hgrn, cum_col, cum_row = _gate_cum(h, g_mix, w_f, w_fbc, w_fbr, bc, br, _tile(s, 512))

    tq = _tile(s, 256)
    oa = _sb_attention(proj, vt, 0, 0, tq, tq)
    ob = _fox_attention(proj, vt, o_fox // foxw, sbw, cum_col, cum_row, tq, _tile(s, 512))
    c_hq = (o_fox + 3 * foxw) // hw
    oc = _hgrn(proj, f_hgrn, lb.reshape(1, hw), hgrn_ng.reshape(1, HGRN_DIM),
               c_hq, c_hq + 1, c_hq + 2, _tile(s, 512))

    gcol0 = (o_fox + 3 * foxw + 3 * hw) // d
    hnew, hn = _merge_out(
        h.reshape(n, d), oa.reshape(n, sbw), ob.reshape(n, foxw), oc.reshape(n, hw),
        proj.reshape(n, -1), gcol0, b_gate,
        w_sb.astype(BF16), w_fox.astype(BF16), w_hgrn.astype(BF16), w_out.astype(BF16),
        g_ffn.reshape(1, d), hn_dtype, _tile(n, 512))
    return hnew, hn


def _moe(h, hn, w_router, w_gate, w_up, w_down, final_g, final):
    n, d = h.shape
    rblk = 512
    nrows = n * TOP_K + N_EXPERTS * rblk
    nb = nrows // rblk
    nb_pad = -(-nb // LANES) * LANES
    idx, wt, cnt = _router(hn, w_router.T.astype(BF16), _tile(n, 1024))
    dest, blk = _dest(idx, cnt, rblk, nb_pad)
    d1, d2 = dest[0], dest[1]
    buf = _dispatch(hn, d1, d2, nrows, _tile(n, 256))
    yb = _experts(buf, blk[0, :nb], blk[1, :1], w_gate.astype(BF16), w_up.astype(BF16),
                  w_down.astype(BF16), rblk, w_gate.shape[2] // 2)
    return _combine(yb, d1, d2, h, wt, final_g.reshape(1, d), _tile(n, 256), final)


def kernel(x, norm_mix_g, w_in, b_fox, b_gate, hgrn_lb_logits, hgrn_norm_g, w_branch_sb, w_branch_fox, w_branch_hgrn, w_out, norm_ffn_g, w_ffn_gate, w_ffn_up, w_ffn_down, w_router, w_exp_gate, w_exp_up, w_exp_down, final_norm_g):
    b, s, d = x.shape
    n = b * s
    depth = w_in.shape[0]
    lbs = _lower_bounds(hgrn_lb_logits)
    h = x
    for l in range(depth):
        dense = l % 2 == 0
        h2, hn = _mixer(h, norm_mix_g[l], w_in[l], b_fox[l], b_gate[l], lbs[l], hgrn_norm_g[l],
                        w_branch_sb[l], w_branch_fox[l], w_branch_hgrn[l], w_out[l], norm_ffn_g[l],
                        BF16 if dense else F32)
        last = l == depth - 1
        i = l // 2
        if dense:
            h2 = _dense_ffn(hn, h2, w_ffn_gate[i].astype(BF16), w_ffn_up[i].astype(BF16),
                            w_ffn_down[i].astype(BF16), _tile(n, 1024), 768)
            if last:
                h2 = _final_norm(h2, final_norm_g.reshape(1, d), _tile(n, 1024))
        else:
            h2 = _moe(h2, hn, w_router[i], w_exp_gate[i], w_exp_up[i], w_exp_down[i],
                      final_norm_g, last)
        h = h2.reshape(b, s, d)
    return h
```

```python
import functools

import jax
import jax.numpy as jnp
from jax import lax
from jax.experimental import pallas as pl
from jax.experimental.pallas import tpu as pltpu

F32 = jnp.float32
BF16 = jnp.bfloat16
I32 = jnp.int32

RMS_EPS = 1e-6
CHUNK = 64
SUBCHUNK = 16
SB_HEADS = 4
FOX_HEADS = 4
HGRN_HEADS = 4
HEAD_DIM = 64
HGRN_DIM = 128
N_EXPERTS = 8
TOP_K = 2
N_BRANCHES = 3

LANES = 128
SUBLANES = 8
VMEM_LIMIT = 56 * 1024 * 1024

EXP_ZERO_BELOW = -105.0
NEG_BIG = -1e30
NORM_SLACK = 1.0 + 2.0 ** -7
DMA_LOOP_UNROLL = 8


def _cparams(*sem):
    return pltpu.CompilerParams(dimension_semantics=sem, vmem_limit_bytes=VMEM_LIMIT)


def _dot(a, b):
    return jnp.dot(a, b, preferred_element_type=F32)


def _dot_nt(a, b):
    return lax.dot_general(a, b, (((1,), (1,)), ((), ())), preferred_element_type=F32)


def _dot_tn(a, b):
    return lax.dot_general(a, b, (((0,), (0,)), ((), ())), preferred_element_type=F32)


def _split2(x):
    hi = x.astype(BF16)
    lo = (x - hi.astype(F32)).astype(BF16)
    return hi, lo


def _split3(x):
    hi = x.astype(BF16)
    r = x - hi.astype(F32)
    mid = r.astype(BF16)
    lo = (r - mid.astype(F32)).astype(BF16)
    return hi, mid, lo


def _log_sigmoid(x):
    return jnp.minimum(x, 0.0) - jnp.log(1.0 + jnp.exp(-jnp.abs(x)))


def _sigmoid(x):
    return 0.5 * jnp.tanh(0.5 * x) + 0.5


def _rmsnorm(x, g):
    ms = jnp.mean(x * x, axis=-1, keepdims=True)
    return x * lax.rsqrt(ms + RMS_EPS) * g


def _tri(n, fn, dtype=BF16):
    r = lax.broadcasted_iota(I32, (n, n), 0)
    c = lax.broadcasted_iota(I32, (n, n), 1)
    return fn(r, c).astype(dtype)


def _lb_kernel(x_ref, o_ref):
    x = x_ref[...]
    depth = x.shape[0]
    m = jnp.max(x, axis=0, keepdims=True)
    e = jnp.exp(x - m)
    sm = e / jnp.sum(e, axis=0, keepdims=True)
    run = sm[0:1]
    first = run
    o_ref[0:1, :] = run - first
    for l in range(1, depth):
        run = run + sm[l:l + 1]
        o_ref[l:l + 1, :] = run - first


def _lower_bounds(logits):
    return pl.pallas_call(
        _lb_kernel,
        out_shape=jax.ShapeDtypeStruct(logits.shape, F32),
        name="hgrn_lower_bounds",
    )(logits.astype(F32))


def _norm_matmul_kernel(x_ref, g_ref, w_ref, wvt_ref, o_ref, vt_ref, xn_ref):
    @pl.when(pl.program_id(1) == 0)
    def _():
        xn = _rmsnorm(x_ref[...], g_ref[...]).astype(BF16)
        xn_ref[...] = xn
        vt_ref[0] = _dot_nt(wvt_ref[...], xn).astype(vt_ref.dtype)

    o_ref[...] = _dot(xn_ref[...], w_ref[...]).astype(o_ref.dtype)


def _norm_matmul(x, g, w, wvt, seq, tm, tn):
    n, d = x.shape
    nout = w.shape[1]
    nv = wvt.shape[0]
    per = seq // tm
    return pl.pallas_call(
        _norm_matmul_kernel,
        grid=(n // tm, nout // tn),
        in_specs=[
            pl.BlockSpec((tm, d), lambda i, j: (i, 0)),
            pl.BlockSpec((1, d), lambda i, j: (0, 0)),
            pl.BlockSpec((d, tn), lambda i, j: (0, j)),
            pl.BlockSpec((nv, d), lambda i, j: (0, 0)),
        ],
        out_specs=[
            pl.BlockSpec((tm, tn), lambda i, j: (i, j)),
            pl.BlockSpec((1, nv, tm), lambda i, j: (i // per, 0, i % per)),
        ],
        out_shape=[
            jax.ShapeDtypeStruct((n, nout), BF16),
            jax.ShapeDtypeStruct((n // seq, nv, seq), BF16),
        ],
        scratch_shapes=[pltpu.VMEM((tm, d), BF16)],
        compiler_params=_cparams("parallel", "arbitrary"),
        name="norm_in_proj",
    )(x, g, w, wvt)


def _gate_cum_kernel(x_ref, g_ref, wf_ref, wc_ref, wr_ref, bc_ref, br_ref, lt_ref, ut_ref,
                     f_ref, cc_ref, cr_ref, carc_ref, carr_ref):
    tm = x_ref.shape[1]

    @pl.when(pl.program_id(1) == 0)
    def _():
        carc_ref[...] = jnp.zeros_like(carc_ref)
        carr_ref[...] = jnp.zeros_like(carr_ref)

    xn = _rmsnorm(x_ref[0], g_ref[...]).astype(BF16)
    f_ref[0] = _dot(xn, wf_ref[...])

    lc = _log_sigmoid(_dot(xn, wc_ref[...]) + bc_ref[...])
    lt = lt_ref[...]
    c_hi, c_mid, c_lo = _split3(lc)
    cum_c = _dot(lt, c_hi) + _dot(lt, c_mid) + _dot(lt, c_lo) + carc_ref[0:1, :]
    cc_ref[0] = cum_c
    carc_ref[...] = jnp.broadcast_to(cum_c[tm - 1:tm, :], carc_ref.shape)

    lr = _log_sigmoid(_dot_nt(wr_ref[...], xn) + br_ref[:, 0:1])
    ut = ut_ref[...]
    r_hi, r_mid, r_lo = _split3(lr)
    cum_r = _dot(r_hi, ut) + _dot(r_mid, ut) + _dot(r_lo, ut) + carr_ref[:, 0:1]
    cr_ref[0] = cum_r
    carr_ref[...] = jnp.broadcast_to(cum_r[:, tm - 1:tm], carr_ref.shape)


def _gate_cum(h, g, wf, wc, wr, bc, br, tm):
    b, s, d = h.shape
    nf = wf.shape[1]
    lt = _tri(tm, lambda r, c: c <= r)
    ut = _tri(tm, lambda r, c: r <= c)
    full = lambda shape: pl.BlockSpec(shape, lambda bi, si: (0,) * len(shape))
    return pl.pallas_call(
        _gate_cum_kernel,
        grid=(b, s // tm),
        in_specs=[
            pl.BlockSpec((1, tm, d), lambda bi, si: (bi, si, 0)),
            full((1, d)), full((d, nf)), full((d, LANES)), full((SUBLANES, d)),
            full((1, LANES)), full((SUBLANES, LANES)), full((tm, tm)), full((tm, tm)),
        ],
        out_specs=[
            pl.BlockSpec((1, tm, nf), lambda bi, si: (bi, si, 0)),
            pl.BlockSpec((1, tm, LANES), lambda bi, si: (bi, si, 0)),
            pl.BlockSpec((1, SUBLANES, tm), lambda bi, si: (bi, 0, si)),
        ],
        out_shape=[
            jax.ShapeDtypeStruct((b, s, nf), F32),
            jax.ShapeDtypeStruct((b, s, LANES), F32),
            jax.ShapeDtypeStruct((b, SUBLANES, s), F32),
        ],
        scratch_shapes=[pltpu.VMEM((SUBLANES, LANES), F32), pltpu.VMEM((SUBLANES, LANES), F32)],
        compiler_params=_cparams("parallel", "arbitrary"),
        name="gate_proj_cum",
    )(h, g, wf, wc, wr, bc, br, lt, ut)


def _masked_queries(q_ref, qm_ref, nheads, hd, scale):
    q = q_ref[0]
    lane = lax.broadcasted_iota(I32, q.shape, 1)
    qs = q * jnp.asarray(scale, q.dtype)
    for h in range(nheads):
        sel = jnp.logical_and(lane >= h * hd, lane < (h + 1) * hd)
        qm_ref[h] = jnp.where(sel, qs, jnp.zeros_like(qs))


def _sb_kernel(q_ref, k_ref, vt_ref, u_ref, o_ref, qm_ref, car_ref, acc_ref, e_ref, tot_ref,
               *, tq, tk, nheads, hd, vrow0):
    i = pl.program_id(1)
    _masked_queries(q_ref, qm_ref, nheads, hd, hd ** -0.5)
    car_ref[...] = jnp.zeros_like(car_ref)
    acc_ref[...] = jnp.zeros_like(acc_ref)
    u = u_ref[...]

    def log_pass(j, slot, diag, valid=None):
        start = pl.multiple_of(j * tk, tk)
        kb = k_ref[0, pl.ds(start, tk), :]
        if diag:
            srow = start + lax.broadcasted_iota(I32, (tk, tq), 0)
            tcol = i * tq + lax.broadcasted_iota(I32, (tk, tq), 1)
            strict = srow < tcol
        for h in range(nheads):
            z = _dot_nt(kb, qm_ref[h])
            lg = -(jnp.maximum(z, 0.0) + jnp.log(1.0 + jnp.exp(-jnp.abs(z))))
            if diag:
                lg = jnp.where(strict, lg, 0.0)
            l_hi, l_lo = _split2(lg)
            suf = _dot(u, l_hi) + _dot(u, l_lo)
            e = z + lg + suf
            tot = suf[0:1, :] + lg[0:1, :]
            if diag:
                e = jnp.where(strict, e, NEG_BIG)
            if valid is not None:
                e = jnp.where(valid, e, NEG_BIG)
                tot = jnp.where(valid, tot, 0.0)
            e_ref[slot * nheads + h] = e
            tot_ref[slot * nheads + h] = tot

    def value_pass(j, slot):
        start = pl.multiple_of(j * tk, tk)
        alive = None
        for h in range(nheads):
            carry = car_ref[h]
            w = jnp.exp(e_ref[slot * nheads + h] + carry)
            vt = vt_ref[0, vrow0 + h * hd:vrow0 + (h + 1) * hd, pl.ds(start, tk)]
            acc_ref[h] += _dot(vt, w.astype(BF16))
            carry = carry + tot_ref[slot * nheads + h]
            car_ref[h] = carry
            top = jnp.max(carry)
            alive = top if alive is None else jnp.maximum(alive, top)
        return alive

    j0 = (i * tq) // tk
    j1 = jnp.maximum(j0 - 1, 0)
    log_pass(j0, 0, True)
    log_pass(j1, 1, False, valid=j0 >= 1)
    value_pass(j0, 0)
    alive0 = value_pass(j1, 1)

    def cond(c):
        j, alive = c
        return jnp.logical_and(j >= 0, alive > EXP_ZERO_BELOW)

    def body(c):
        j, _ = c
        log_pass(j, 0, False)
        return j - 1, value_pass(j, 0)

    lax.while_loop(cond, body, (j0 - 2, alive0))
    o_ref[0] = acc_ref[...].reshape(nheads * hd, tq).T.astype(o_ref.dtype)


def _sb_attention(proj, vt, col0, vrow0, tq, tk):
    b, s, _ = proj.shape
    width = SB_HEADS * HEAD_DIM
    nvt = vt.shape[1]
    u = _tri(tk, lambda r, c: c > r)
    kern = functools.partial(_sb_kernel, tq=tq, tk=tk, nheads=SB_HEADS, hd=HEAD_DIM, vrow0=vrow0)
    return pl.pallas_call(
        kern,
        grid=(b, s // tq),
        in_specs=[
            pl.BlockSpec((1, tq, width), lambda bi, i: (bi, i, col0)),
            pl.BlockSpec((1, s, width), lambda bi, i: (bi, 0, col0 + 1)),
            pl.BlockSpec((1, nvt, s), lambda bi, i: (bi, 0, 0)),
            pl.BlockSpec((tk, tk), lambda bi, i: (0, 0)),
        ],
        out_specs=pl.BlockSpec((1, tq, width), lambda bi, i: (bi, i, 0)),
        out_shape=jax.ShapeDtypeStruct((b, s, width), BF16),
        scratch_shapes=[pltpu.VMEM((SB_HEADS, tq, width), BF16),
                        pltpu.VMEM((SB_HEADS, 1, tq), F32),
                        pltpu.VMEM((SB_HEADS, HEAD_DIM, tq), F32),
                        pltpu.VMEM((2 * SB_HEADS, tk, tq), F32),
                        pltpu.VMEM((2 * SB_HEADS, 1, tq), F32)],
        compiler_params=_cparams("parallel", "arbitrary"),
        name="stick_breaking_attention",
    )(proj, proj, vt, u)


def _fox_kernel(q_ref, k_ref, vt_ref, cc_ref, cr_ref, kmax_ref, cend_ref, et_ref, o_ref,
                qm_ref, s_ref, mx_ref, m_ref, l_ref, acc_ref, qn_ref, *, tq, tk, nheads, hd, vrow0):
    i = pl.program_id(1)
    _masked_queries(q_ref, qm_ref, nheads, hd, hd ** -0.5)
    m_ref[...] = jnp.full(m_ref.shape, NEG_BIG, F32)
    l_ref[...] = jnp.zeros_like(l_ref)
    acc_ref[...] = jnp.zeros_like(acc_ref)

    def score_pass(j, slot, diag):
        start = pl.multiple_of(j * tk, tk)
        kb = k_ref[0, pl.ds(start, tk), :]
        if diag:
            srow = start + lax.broadcasted_iota(I32, (tk, tq), 0)
            tcol = i * tq + lax.broadcasted_iota(I32, (tk, tq), 1)
            causal = srow <= tcol
        for h in range(nheads):
            cs = cc_ref[0, pl.ds(start, tk), h:h + 1]
            s = _dot_nt(kb, qm_ref[h]) - cs
            if diag:
                s = jnp.where(causal, s, NEG_BIG)
            s_ref[slot * nheads + h] = s
            mx_ref[slot * nheads + h] = jnp.max(s, axis=0, keepdims=True)

    def value_pass(j, slot):
        start = pl.multiple_of(j * tk, tk)
        for h in range(nheads):
            ct = cr_ref[0, h:h + 1, :]
            m = m_ref[h]
            m_new = jnp.maximum(m, mx_ref[slot * nheads + h] + ct)
            alpha = jnp.exp(m - m_new)
            p = jnp.exp(s_ref[slot * nheads + h] - (m_new - ct))
            l_ref[h] = alpha * l_ref[h] + jnp.sum(p, axis=0, keepdims=True)
            vt = vt_ref[0, vrow0 + h * hd:vrow0 + (h + 1) * hd, pl.ds(start, tk)]
            acc_ref[h] = alpha * acc_ref[h] + _dot(vt, p.astype(BF16))
            m_ref[h] = m_new

    qf = q_ref[0].astype(F32) * (hd ** -0.5)
    sq_hi, sq_lo = _split2(qf * qf)
    et = et_ref[...]
    qn_ref[...] = jnp.sqrt(_dot_nt(et, sq_hi) + _dot_nt(et, sq_lo)) * NORM_SLACK

    def bound_below(j):
        sel = lax.broadcasted_iota(I32, (SUBLANES, LANES), 1) == j
        kmax = jnp.max(jnp.where(sel, kmax_ref[0], 0.0), axis=1, keepdims=True)
        cend = jnp.sum(jnp.where(sel, cend_ref[0], 0.0), axis=1, keepdims=True)
        best = None
        for h in range(nheads):
            b = (qn_ref[h:h + 1, :] * kmax[h:h + 1, :] + cr_ref[0, h:h + 1, :]
                 - cend[h:h + 1, :] - m_ref[h])
            best = b if best is None else jnp.maximum(best, b)
        return jnp.max(best)

    j0 = (i * tq) // tk
    score_pass(j0, 0, True)

    def cond(c):
        it, alive = c
        return jnp.logical_and(it < j0, alive > EXP_ZERO_BELOW)

    def body(c):
        it, _ = c
        slot = lax.rem(it, 2)
        score_pass(j0 - 1 - it, 1 - slot, False)
        value_pass(j0 - it, slot)
        return it + 1, bound_below(j0 - 2 - it)

    done, _ = lax.while_loop(cond, body, (jnp.int32(0), jnp.float32(0.0)))
    value_pass(j0 - done, lax.rem(done, 2))
    out = acc_ref[...] / l_ref[...]
    o_ref[0] = out.reshape(nheads * hd, tq).T.astype(o_ref.dtype)


def _head_selector(nheads, hd):
    r = lax.broadcasted_iota(I32, (SUBLANES, nheads * hd), 0)
    c = lax.broadcasted_iota(I32, (SUBLANES, nheads * hd), 1)
    return (c // hd == r).astype(BF16)


def _key_bounds_kernel(k_ref, et_ref, o_ref, run_ref):
    j = pl.program_id(1)

    @pl.when(j == 0)
    def _():
        run_ref[...] = jnp.zeros_like(run_ref)
        o_ref[...] = jnp.zeros_like(o_ref)

    kf = k_ref[0].astype(F32)
    sq_hi, sq_lo = _split2(kf * kf)
    et = et_ref[...]
    n2 = _dot_nt(et, sq_hi) + _dot_nt(et, sq_lo)
    tile_max = jnp.sqrt(jnp.max(n2, axis=1, keepdims=True)) * NORM_SLACK
    run = jnp.maximum(run_ref[...], tile_max)
    run_ref[...] = run
    lane = lax.broadcasted_iota(I32, (SUBLANES, LANES), 1)
    o_ref[0] = jnp.where(lane == j, run, o_ref[0])


def _key_bounds(proj, colk, tk):
    b, s, _ = proj.shape
    width = FOX_HEADS * HEAD_DIM
    assert s // tk <= LANES
    return pl.pallas_call(
        _key_bounds_kernel,
        grid=(b, s // tk),
        in_specs=[
            pl.BlockSpec((1, tk, width), lambda bi, j: (bi, j, colk)),
            pl.BlockSpec((SUBLANES, width), lambda bi, j: (0, 0)),
        ],
        out_specs=pl.BlockSpec((1, SUBLANES, LANES), lambda bi, j: (bi, 0, 0)),
        out_shape=jax.ShapeDtypeStruct((b, SUBLANES, LANES), F32),
        scratch_shapes=[pltpu.VMEM((SUBLANES, LANES), F32)],
        compiler_params=_cparams("parallel", "arbitrary"),
        name="fox_key_norm_bounds",
    )(proj, _head_selector(FOX_HEADS, HEAD_DIM))


def _fox_attention(proj, vt, col0, vrow0, cum_col, cum_row, tq, tk):
    b, s, _ = proj.shape
    width = FOX_HEADS * HEAD_DIM
    nvt = vt.shape[1]
    kmax = _key_bounds(proj, col0 + 1, tk)
    cend = jnp.pad(cum_row[:, :, tk - 1::tk], ((0, 0), (0, 0), (0, LANES - s // tk)))
    kern = functools.partial(_fox_kernel, tq=tq, tk=tk, nheads=FOX_HEADS, hd=HEAD_DIM, vrow0=vrow0)
    return pl.pallas_call(
        kern,
        grid=(b, s // tq),
        in_specs=[
            pl.BlockSpec((1, tq, width), lambda bi, i: (bi, i, col0)),
            pl.BlockSpec((1, s, width), lambda bi, i: (bi, 0, col0 + 1)),
            pl.BlockSpec((1, nvt, s), lambda bi, i: (bi, 0, 0)),
            pl.BlockSpec((1, s, LANES), lambda bi, i: (bi, 0, 0)),
            pl.BlockSpec((1, SUBLANES, tq), lambda bi, i: (bi, 0, i)),
            pl.BlockSpec((1, SUBLANES, LANES), lambda bi, i: (bi, 0, 0)),
            pl.BlockSpec((1, SUBLANES, LANES), lambda bi, i: (bi, 0, 0)),
            pl.BlockSpec((SUBLANES, width), lambda bi, i: (0, 0)),
        ],
        out_specs=pl.BlockSpec((1, tq, width), lambda bi, i: (bi, i, 0)),
        out_shape=jax.ShapeDtypeStruct((b, s, width), BF16),
        scratch_shapes=[pltpu.VMEM((FOX_HEADS, tq, width), BF16),
                        pltpu.VMEM((2 * FOX_HEADS, tk, tq), F32),
                        pltpu.VMEM((2 * FOX_HEADS, 1, tq), F32),
                        pltpu.VMEM((FOX_HEADS, 1, tq), F32),
                        pltpu.VMEM((FOX_HEADS, 1, tq), F32),
                        pltpu.VMEM((FOX_HEADS, HEAD_DIM, tq), F32),
                        pltpu.VMEM((SUBLANES, tq), F32)],
        compiler_params=_cparams("parallel", "arbitrary"),
        name="forgetting_attention",
    )(proj, proj, vt, cum_col, cum_row, kmax, cend, _head_selector(FOX_HEADS, HEAD_DIM))


HGRN_SAFE_SPAN = 40.0
HGRN_GROUP = 4


def _hgrn_kernel(q_ref, f_ref, i_ref, g_ref, lb_ref, ng_ref, lt_ref, o_ref,
                 lf_ref, kk_ref, qq_ref, b_ref, oi_ref, st_ref, *, ts, nheads, dk):
    @pl.when(pl.program_id(1) == 0)
    def _():
        st_ref[...] = jnp.zeros_like(st_ref)

    lb = lb_ref[...]
    z = f_ref[0]
    lsz = _log_sigmoid(z)
    a = jnp.log(lb)
    c = jnp.log(1.0 - lb) + lsz
    lf_ref[...] = jnp.maximum(a, c) + jnp.log(1.0 + jnp.exp(-jnp.abs(a - c)))
    kk_ref[...] = (1.0 - lb) * jnp.exp(lsz - z)
    qin = q_ref[0].astype(F32)
    qq_ref[...] = qin * _sigmoid(qin)

    lt = lt_ref[...]
    nsub = CHUNK // SUBCHUNK
    rr = lax.broadcasted_iota(I32, (SUBCHUNK, SUBCHUNK), 0)
    cc = lax.broadcasted_iota(I32, (SUBCHUNK, SUBCHUNK), 1)
    causal = cc <= rr
    sub = lax.broadcasted_iota(I32, (SUBLANES, SUBCHUNK), 0)
    causal_c = (lax.broadcasted_iota(I32, (CHUNK, CHUNK), 1)
                <= lax.broadcasted_iota(I32, (CHUNK, CHUNK), 0))
    row_blk = lax.broadcasted_iota(I32, (CHUNK, dk), 0) // SUBCHUNK
    ng = ng_ref[...]

    def prepare(r0):
        rows = pl.ds(r0, CHUNK)
        h_hi, h_mid, h_lo = _split3(lf_ref[rows, :])
        bcum = _dot(lt, h_hi) + _dot(lt, h_mid) + _dot(lt, h_lo)
        b_ref[rows, :] = bcum
        span = -bcum[SUBCHUNK - 1:SUBCHUNK, :]
        for bi in range(1, nsub):
            lo = bi * SUBCHUNK
            span = jnp.maximum(span, bcum[lo - 1:lo, :] - bcum[lo + SUBCHUNK - 1:lo + SUBCHUNK, :])
        return (r0, bcum, kk_ref[rows, :], qq_ref[rows, :], i_ref[0, rows, :]), span

    def intra_exact(ck, h, bi):
        r0, bcum, kk, qq, vv = ck
        hs = slice(h * dk, (h + 1) * dk)
        lo = bi * SUBCHUNK
        hi = lo + SUBCHUNK
        b_i = bcum[lo:hi, hs]
        kk_i = kk[lo:hi, hs]
        groups = []
        for gi in range(SUBCHUNK // SUBLANES):
            g0 = lo + gi * SUBLANES
            qg = qq[g0:g0 + SUBLANES, hs].astype(BF16)
            dg = jnp.zeros((SUBLANES, SUBCHUNK), F32)
            for r in range(SUBLANES):
                bt = bcum[g0 + r:g0 + r + 1, hs]
                ke = kk_i * jnp.exp(jnp.minimum(bt - b_i, 0.0))
                res = _dot_nt(qg, ke.astype(BF16))
                dg = jnp.where(sub == r, res, dg)
            groups.append(dg)
        dblk = jnp.where(causal, jnp.concatenate(groups, axis=0), 0.0)
        o_i = _dot(dblk.astype(BF16), vv[lo:hi, hs])
        if bi > 0:
            bref = bcum[lo - 1:lo, hs]
            q_i = (qq[lo:hi, hs] * jnp.exp(b_i - bref)).astype(BF16)
            k_p = (kk[0:lo, hs] * jnp.exp(bref - bcum[0:lo, hs])).astype(BF16)
            o_i = o_i + _dot(_dot_nt(q_i, k_p).astype(BF16), vv[0:lo, hs])
        return o_i

    def intra_fast(ck, h):
        r0, bcum, kk, qq, vv = ck
        hs = slice(h * dk, (h + 1) * dk)
        bh = bcum[:, hs]
        refs = [jnp.zeros((1, dk), F32)] + [bh[bi * SUBCHUNK - 1:bi * SUBCHUNK, :] for bi in range(1, nsub)]
        ref_rows = jnp.concatenate([jnp.broadcast_to(r, (SUBCHUNK, dk)) for r in refs], axis=0)
        q_stack = qq[:, hs] * jnp.exp(bh - ref_rows)
        q_big = jnp.concatenate(
            [jnp.where(row_blk == bi, q_stack, 0.0) for bi in range(nsub)], axis=1).astype(BF16)
        k_big = jnp.concatenate(
            [kk[:, hs] * jnp.exp(jnp.minimum(r - bh, HGRN_SAFE_SPAN)) for r in refs], axis=1).astype(BF16)
        s_all = jnp.where(causal_c, _dot_nt(q_big, k_big), 0.0)
        return _dot(s_all.astype(BF16), vv[:, hs])

    def phase1(gi, carry):
        cks, spans = [], None
        for c in range(HGRN_GROUP):
            ck, span = prepare(pl.multiple_of((gi * HGRN_GROUP + c) * CHUNK, CHUNK))
            cks.append(ck)
            spans = span if spans is None else jnp.maximum(spans, span)
        safe = jnp.max(spans) < HGRN_SAFE_SPAN

        @pl.when(safe)
        def _():
            for ck in cks:
                for h in range(nheads):
                    oi_ref[pl.ds(ck[0], CHUNK), h * dk:(h + 1) * dk] = intra_fast(ck, h)

        @pl.when(jnp.logical_not(safe))
        def _():
            for ck in cks:
                for h in range(nheads):
                    for bi in range(nsub):
                        oi_ref[pl.ds(ck[0] + bi * SUBCHUNK, SUBCHUNK), h * dk:(h + 1) * dk] = (
                            intra_exact(ck, h, bi))
        return carry

    lax.fori_loop(0, ts // (CHUNK * HGRN_GROUP), phase1, 0)

    def phase2(ci, carry):
        r0 = pl.multiple_of(ci * CHUNK, CHUNK)
        rows = pl.ds(r0, CHUNK)
        for h in range(nheads):
            hs = slice(h * dk, (h + 1) * dk)
            st = st_ref[h]
            bh = b_ref[rows, hs]
            qq = qq_ref[rows, hs]
            kk = kk_ref[rows, hs]
            vv = i_ref[0, rows, hs]
            qe = (qq * jnp.exp(bh)).astype(BF16)
            o = oi_ref[rows, hs] + _dot_nt(qe, st.astype(BF16))
            b_last = bh[CHUNK - 1:CHUNK, :]
            kd = (kk * jnp.exp(b_last - bh)).astype(BF16)
            st_ref[h] = st * jnp.exp(b_last) + _dot_tn(vv, kd)
            gate = g_ref[0, rows, hs].astype(F32)
            on = _rmsnorm(o, ng) * (gate * _sigmoid(gate))
            o_ref[0, rows, hs] = on.astype(o_ref.dtype)
        return carry

    lax.fori_loop(0, ts // CHUNK, phase2, 0, unroll=HGRN_GROUP)


def _hgrn(proj, f_hgrn, lb, ng, colq, coli, colg, ts):
    b, s, _ = proj.shape
    dk = HGRN_DIM
    hw = HGRN_HEADS * dk
    lt = _tri(CHUNK, lambda r, c: c <= r)
    kern = functools.partial(_hgrn_kernel, ts=ts, nheads=HGRN_HEADS, dk=dk)
    return pl.pallas_call(
        kern,
        grid=(b, s // ts),
        in_specs=[
            pl.BlockSpec((1, ts, hw), lambda bi, si: (bi, si, colq)),
            pl.BlockSpec((1, ts, hw), lambda bi, si: (bi, si, 0)),
            pl.BlockSpec((1, ts, hw), lambda bi, si: (bi, si, coli)),
            pl.BlockSpec((1, ts, hw), lambda bi, si: (bi, si, colg)),
            pl.BlockSpec((1, hw), lambda bi, si: (0, 0)),
            pl.BlockSpec((1, dk), lambda bi, si: (0, 0)),
            pl.BlockSpec((CHUNK, CHUNK), lambda bi, si: (0, 0)),
        ],
        out_specs=pl.BlockSpec((1, ts, hw), lambda bi, si: (bi, si, 0)),
        out_shape=jax.ShapeDtypeStruct((b, s, hw), BF16),
        scratch_shapes=[
            pltpu.VMEM((ts, hw), F32), pltpu.VMEM((ts, hw), F32), pltpu.VMEM((ts, hw), F32),
            pltpu.VMEM((ts, hw), F32), pltpu.VMEM((ts, hw), F32),
            pltpu.VMEM((HGRN_HEADS, dk, dk), F32),
        ],
        compiler_params=_cparams("parallel", "arbitrary"),
        name="hgrn2_recurrence",
    )(proj, f_hgrn, proj, proj, lb, ng, lt)


def _merge_kernel(h_ref, oa_ref, ob_ref, oc_ref, g0_ref, g1_ref, g2_ref, bg_ref,
                  wa_ref, wb_ref, wc_ref, wo_ref, ng_ref, hout_ref, hn_ref):
    def gate(gref, k):
        x = gref[...].astype(F32) + bg_ref[k:k + 1, :]
        return _sigmoid(x)

    merged = (gate(g0_ref, 0) * _dot(oa_ref[...], wa_ref[...])
              + gate(g1_ref, 1) * _dot(ob_ref[...], wb_ref[...])
              + gate(g2_ref, 2) * _dot(oc_ref[...], wc_ref[...]))
    hnew = h_ref[...] + _dot(merged.astype(BF16), wo_ref[...])
    hout_ref[...] = hnew
    hn_ref[...] = _rmsnorm(hnew, ng_ref[...]).astype(hn_ref.dtype)


def _merge_out(h, oa, ob, oc, proj, gcol0, bg, wa, wb, wc, wo, ng, hn_dtype, tm):
    n, d = h.shape
    full = lambda shape: pl.BlockSpec(shape, lambda i: (0,) * len(shape))
    rows = lambda w: pl.BlockSpec((tm, w), lambda i: (i, 0))
    return pl.pallas_call(
        _merge_kernel,
        grid=(n // tm,),
        in_specs=[
            rows(d), rows(oa.shape[1]), rows(ob.shape[1]), rows(oc.shape[1]),
            pl.BlockSpec((tm, d), lambda i: (i, gcol0)),
            pl.BlockSpec((tm, d), lambda i: (i, gcol0 + 1)),
            pl.BlockSpec((tm, d), lambda i: (i, gcol0 + 2)),
            full(bg.shape), full(wa.shape), full(wb.shape), full(wc.shape), full(wo.shape),
            full((1, d)),
        ],
        out_specs=[rows(d), rows(d)],
        out_shape=[jax.ShapeDtypeStruct((n, d), F32), jax.ShapeDtypeStruct((n, d), hn_dtype)],
        compiler_params=_cparams("parallel"),
        name="merge_out_proj",
    )(h, oa, ob, oc, proj, proj, proj, bg, wa, wb, wc, wo, ng)


def _ffn_kernel(hn_ref, h_ref, wg_ref, wu_ref, wd_ref, o_ref, acc_ref):
    j = pl.program_id(1)

    @pl.when(j == 0)
    def _():
        acc_ref[...] = jnp.zeros_like(acc_ref)

    x = hn_ref[...]
    gt = _dot(x, wg_ref[...])
    up = _dot(x, wu_ref[...])
    act = (gt / (1.0 + jnp.exp(-gt))) * up
    acc_ref[...] += _dot(act.astype(BF16), wd_ref[...])

    @pl.when(j == pl.num_programs(1) - 1)
    def _():
        o_ref[...] = h_ref[...] + acc_ref[...]


def _dense_ffn(hn, h, wg, wu, wd, tm, tf):
    n, d = h.shape
    ff = wg.shape[1]
    return pl.pallas_call(
        _ffn_kernel,
        grid=(n // tm, ff // tf),
        in_specs=[
            pl.BlockSpec((tm, d), lambda i, j: (i, 0)),
            pl.BlockSpec((tm, d), lambda i, j: (i, 0)),
            pl.BlockSpec((d, tf), lambda i, j: (0, j)),
            pl.BlockSpec((d, tf), lambda i, j: (0, j)),
            pl.BlockSpec((tf, d), lambda i, j: (j, 0)),
        ],
        out_specs=pl.BlockSpec((tm, d), lambda i, j: (i, 0)),
        out_shape=jax.ShapeDtypeStruct((n, d), F32),
        scratch_shapes=[pltpu.VMEM((tm, d), F32)],
        compiler_params=_cparams("parallel", "arbitrary"),
        name="dense_swiglu",
    )(hn, h, wg, wu, wd)


def _router_kernel(hn_ref, wr_ref, u_ref, idx_ref, wt_ref, cnt_ref, car_ref):
    t = hn_ref.shape[0]

    @pl.when(pl.program_id(0) == 0)
    def _():
        car_ref[...] = jnp.zeros_like(car_ref)

    logits = _dot_nt(wr_ref[...], hn_ref[...].astype(BF16))
    eidx = lax.broadcasted_iota(I32, (N_EXPERTS, t), 0)
    m1 = jnp.max(logits, axis=0, keepdims=True)
    i1 = jnp.min(jnp.where(logits == m1, eidx, N_EXPERTS), axis=0, keepdims=True)
    sel1 = eidx == i1
    rest = jnp.where(sel1, -jnp.inf, logits)
    m2 = jnp.max(rest, axis=0, keepdims=True)
    i2 = jnp.min(jnp.where(rest == m2, eidx, N_EXPERTS), axis=0, keepdims=True)
    sel2 = eidx == i2
    e21 = jnp.exp(m2 - m1)
    w1 = 1.0 / (1.0 + e21)
    w2 = e21 / (1.0 + e21)

    cnt = jnp.where(jnp.logical_or(sel1, sel2), 1.0, 0.0)
    excl = _dot(cnt.astype(BF16), u_ref[...])
    rank = excl + car_ref[:, 0:1]
    p1 = jnp.sum(jnp.where(sel1, rank, 0.0), axis=0, keepdims=True)
    p2 = jnp.sum(jnp.where(sel2, rank, 0.0), axis=0, keepdims=True)
    total = rank[:, t - 1:t] + cnt[:, t - 1:t]
    car_ref[...] = jnp.broadcast_to(total, car_ref.shape)
    cnt_ref[...] = jnp.broadcast_to(total, cnt_ref.shape)

    zi = jnp.zeros((1, t), I32)
    idx_ref[...] = jnp.concatenate(
        [i1, i2, p1.astype(I32), p2.astype(I32), zi, zi, zi, zi], axis=0)
    zf = jnp.zeros((1, t), F32)
    wt_ref[...] = jnp.concatenate([w1, w2, zf, zf, zf, zf, zf, zf], axis=0)


def _router(hn, wr_t, t):
    n, d = hn.shape
    u = _tri(t, lambda r, c: r < c)
    return pl.pallas_call(
        _router_kernel,
        grid=(n // t,),
        in_specs=[
            pl.BlockSpec((t, d), lambda i: (i, 0)),
            pl.BlockSpec((N_EXPERTS, d), lambda i: (0, 0)),
            pl.BlockSpec((t, t), lambda i: (0, 0)),
        ],
        out_specs=[
            pl.BlockSpec((SUBLANES, t), lambda i: (0, i)),
            pl.BlockSpec((SUBLANES, t), lambda i: (0, i)),
            pl.BlockSpec((N_EXPERTS, LANES), lambda i: (0, 0)),
        ],
        out_shape=[
            jax.ShapeDtypeStruct((SUBLANES, n), I32),
            jax.ShapeDtypeStruct((SUBLANES, n), F32),
            jax.ShapeDtypeStruct((N_EXPERTS, LANES), F32),
        ],
        scratch_shapes=[pltpu.VMEM((N_EXPERTS, LANES), F32)],
        compiler_params=_cparams("arbitrary"),
        name="moe_router_top2",
    )(hn, wr_t, u)


def _dest_kernel(idx_ref, cnt_ref, dest_ref, blk_ref, *, rblk):
    n = idx_ref.shape[1]
    nb = blk_ref.shape[1]
    shift = rblk.bit_length() - 1
    assert rblk == 1 << shift
    e1 = idx_ref[0:1, :]
    e2 = idx_ref[1:2, :]
    d1 = idx_ref[2:3, :]
    d2 = idx_ref[3:4, :]
    bstart = lax.broadcasted_iota(I32, (1, nb), 1) * rblk
    bexp = jnp.zeros((1, nb), I32)
    pend = jnp.zeros((1, 1), I32)
    for e in range(N_EXPERTS):
        c = cnt_ref[e:e + 1, 0:1].astype(I32)
        padded = lax.shift_left(lax.shift_right_logical(c + (rblk - 1), shift), shift)
        pstart = pend
        pend = pend + padded
        d1 = d1 + jnp.where(e1 == e, pstart, 0)
        d2 = d2 + jnp.where(e2 == e, pstart, 0)
        bexp = bexp + jnp.where(bstart >= pend, 1, 0)
    zi = jnp.zeros((1, n), I32)
    dest_ref[...] = jnp.concatenate([d1, d2, zi, zi, zi, zi, zi, zi], axis=0)
    zb = jnp.zeros((1, nb), I32)
    nvalid = jnp.broadcast_to(lax.shift_right_logical(pend, shift), (1, nb))
    blk_ref[...] = jnp.concatenate(
        [jnp.minimum(bexp, N_EXPERTS - 1), nvalid, zb, zb, zb, zb, zb, zb], axis=0)


def _dest(idx, cnt, rblk, nb_pad):
    n = idx.shape[1]
    return pl.pallas_call(
        functools.partial(_dest_kernel, rblk=rblk),
        out_shape=[
            jax.ShapeDtypeStruct((SUBLANES, n), I32),
            jax.ShapeDtypeStruct((SUBLANES, nb_pad), I32),
        ],
        compiler_params=pltpu.CompilerParams(vmem_limit_bytes=VMEM_LIMIT),
        name="moe_slot_assign",
    )(idx, cnt)


def _dispatch_kernel(d1_ref, d2_ref, x_ref, buf_in, buf_hbm, sem, *, t):
    del buf_in

    def row_copy(src_row, dst_row):
        return pltpu.make_async_copy(x_ref.at[pl.ds(src_row, 1)], buf_hbm.at[pl.ds(dst_row, 1)], sem)

    def start(r, c):
        row_copy(r, d1_ref[r]).start()
        row_copy(r, d2_ref[r]).start(priority=1)
        return c

    lax.fori_loop(0, t, start, 0, unroll=DMA_LOOP_UNROLL)

    def wait(r, c):
        row_copy(r, d1_ref[r]).wait()
        row_copy(r, d2_ref[r]).wait()
        return c

    lax.fori_loop(0, t, wait, 0, unroll=DMA_LOOP_UNROLL)


def _dispatch(x, d1, d2, nrows, t):
    n, d = x.shape
    buf0 = jnp.zeros((nrows, d), x.dtype)
    smem = lambda: pl.BlockSpec((t,), lambda i: (i,), memory_space=pltpu.SMEM)
    return pl.pallas_call(
        functools.partial(_dispatch_kernel, t=t),
        grid=(n // t,),
        in_specs=[smem(), smem(),
                  pl.BlockSpec((t, d), lambda i: (i, 0)), pl.BlockSpec(memory_space=pl.ANY)],
        out_specs=pl.BlockSpec(memory_space=pl.ANY),
        out_shape=jax.ShapeDtypeStruct((nrows, d), x.dtype),
        scratch_shapes=[pltpu.SemaphoreType.DMA(())],
        input_output_aliases={3: 0},
        compiler_params=_cparams("arbitrary"),
        name="moe_dispatch_rows",
    )(d1, d2, x, buf0)


def _expert_kernel(be_ref, nv_ref, x_ref, wg_ref, wu_ref, wd_ref, o_ref, xb_ref):
    i = pl.program_id(0)
    j = pl.program_id(1)

    @pl.when(j == 0)
    def _():
        o_ref[...] = jnp.zeros_like(o_ref)
        xb_ref[...] = x_ref[...].astype(BF16)

    @pl.when(i < nv_ref[0])
    def _():
        x = xb_ref[...]
        gt = _dot(x, wg_ref[0])
        up = _dot(x, wu_ref[0])
        act = (gt / (1.0 + jnp.exp(-gt))) * up
        o_ref[...] += _dot(act.astype(BF16), wd_ref[0])


def _experts(buf, be, nv, wg, wu, wd, rblk, tf):
    nrows, d = buf.shape
    ff = wg.shape[2]
    nj = ff // tf

    def wcol(i, j, be_ref, nv_ref):
        return (be_ref[i], 0, jnp.where(i < nv_ref[0], j, nj - 1))

    def wrow(i, j, be_ref, nv_ref):
        return (be_ref[i], jnp.where(i < nv_ref[0], j, nj - 1), 0)

    grid_spec = pltpu.PrefetchScalarGridSpec(
        num_scalar_prefetch=2,
        grid=(nrows // rblk, nj),
        in_specs=[
            pl.BlockSpec((rblk, d), lambda i, j, be_ref, nv_ref: (i, 0)),
            pl.BlockSpec((1, d, tf), wcol),
            pl.BlockSpec((1, d, tf), wcol),
            pl.BlockSpec((1, tf, d), wrow),
        ],
        out_specs=pl.BlockSpec((rblk, d), lambda i, j, be_ref, nv_ref: (i, 0)),
        scratch_shapes=[pltpu.VMEM((rblk, d), BF16)],
    )
    return pl.pallas_call(
        _expert_kernel,
        grid_spec=grid_spec,
        out_shape=jax.ShapeDtypeStruct((nrows, d), F32),
        compiler_params=_cparams("parallel", "arbitrary"),
        name="moe_grouped_swiglu",
    )(be, nv, buf, wg, wu, wd)


def _combine_kernel(d1_ref, d2_ref, yb_hbm, h_ref, wt_ref, ng_ref, o_ref, y_ref, sem, *, t, final):
    def row_copy(slot, dst_row, src_row):
        return pltpu.make_async_copy(yb_hbm.at[pl.ds(src_row, 1)],
                                     y_ref.at[slot, pl.ds(dst_row, 1)], sem)

    def start(r, c):
        row_copy(0, r, d1_ref[r]).start()
        row_copy(1, r, d2_ref[r]).start(priority=1)
        return c

    lax.fori_loop(0, t, start, 0, unroll=DMA_LOOP_UNROLL)

    def wait(r, c):
        row_copy(0, r, d1_ref[r]).wait()
        row_copy(1, r, d2_ref[r]).wait()
        return c

    lax.fori_loop(0, t, wait, 0, unroll=DMA_LOOP_UNROLL)

    rr = lax.broadcasted_iota(I32, (t, t), 0)
    cc = lax.broadcasted_iota(I32, (t, t), 1)
    eye = rr == cc
    w1 = jnp.sum(jnp.where(eye, wt_ref[0:1, :], 0.0), axis=1, keepdims=True)
    w2 = jnp.sum(jnp.where(eye, wt_ref[1:2, :], 0.0), axis=1, keepdims=True)
    out = h_ref[...] + w1 * y_ref[0] + w2 * y_ref[1]
    if final:
        out = _rmsnorm(out, ng_ref[...])
    o_ref[...] = out


def _combine(yb, d1, d2, h, wt, ng, t, final):
    n, d = h.shape
    smem = lambda: pl.BlockSpec((t,), lambda i: (i,), memory_space=pltpu.SMEM)
    return pl.pallas_call(
        functools.partial(_combine_kernel, t=t, final=final),
        grid=(n // t,),
        in_specs=[
            smem(), smem(),
            pl.BlockSpec(memory_space=pl.ANY),
            pl.BlockSpec((t, d), lambda i: (i, 0)),
            pl.BlockSpec((SUBLANES, t), lambda i: (0, i)),
            pl.BlockSpec((1, d), lambda i: (0, 0)),
        ],
        out_specs=pl.BlockSpec((t, d), lambda i: (i, 0)),
        out_shape=jax.ShapeDtypeStruct((n, d), F32),
        scratch_shapes=[pltpu.VMEM((2, t, d), F32), pltpu.SemaphoreType.DMA(())],
        compiler_params=_cparams("arbitrary"),
        name="moe_combine",
    )(d1, d2, yb, h, wt, ng)


def _final_norm_kernel(x_ref, g_ref, o_ref):
    o_ref[...] = _rmsnorm(x_ref[...], g_ref[...])


def _final_norm(h, g, tm):
    n, d = h.shape
    return pl.pallas_call(
        _final_norm_kernel,
        grid=(n // tm,),
        in_specs=[pl.BlockSpec((tm, d), lambda i: (i, 0)), pl.BlockSpec((1, d), lambda i: (0, 0))],
        out_specs=pl.BlockSpec((tm, d), lambda i: (i, 0)),
        out_shape=jax.ShapeDtypeStruct((n, d), F32),
        compiler_params=_cparams("parallel"),
        name="final_rmsnorm",
    )(h, g)


def _tile(n, pref):
    t = min(n, pref)
    assert n % t == 0, (n, pref)
    return t


def _mixer(h, g_mix, w_in, b_fox, b_gate, lb, hgrn_ng, w_sb, w_fox, w_hgrn, w_out, g_ffn, hn_dtype):
    b, s, d = h.shape
    n = b * s
    sbw = SB_HEADS * HEAD_DIM
    foxw = FOX_HEADS * HEAD_DIM
    hw = HGRN_HEADS * HGRN_DIM
    o_fox = 3 * sbw
    o_fb = o_fox + 3 * foxw
    o_hq = o_fb + FOX_HEADS
    o_hf, o_hi, o_hg = o_hq + hw, o_hq + 2 * hw, o_hq + 3 * hw
    o_gate = o_hq + 4 * hw
    w_main = jnp.concatenate(
        [w_in[:, o_gate:], w_in[:, o_hq:o_hf], w_in[:, o_hi:o_gate],
         w_in[:, :2 * sbw], w_in[:, o_fox:o_fox + 2 * foxw]], axis=1).astype(BF16)
    p_hgrn = N_BRANCHES * d
    p_sb = p_hgrn + 3 * hw
    p_fox = p_sb + 2 * sbw
    w_vt = jnp.concatenate([w_in[:, 2 * sbw:3 * sbw], w_in[:, o_fox + 2 * foxw:o_fb]], axis=1).T.astype(BF16)
    w_f = w_in[:, o_hf:o_hi].astype(BF16)
    w_fb = w_in[:, o_fb:o_hq]
    w_fbc = jnp.pad(w_fb, ((0, 0), (0, LANES - FOX_HEADS))).astype(BF16)
    w_fbr = jnp.pad(w_fb.T, ((0, SUBLANES - FOX_HEADS), (0, 0))).astype(BF16)
    bc = jnp.pad(b_fox, (0, LANES - FOX_HEADS)).reshape(1, LANES)
    br = jnp.broadcast_to(jnp.pad(b_fox, (0, SUBLANES - FOX_HEADS)).reshape(SUBLANES, 1), (SUBLANES, LANES))

    g_mix = g_mix.reshape(1, d)
    tm = _tile(s, 1024)
    proj, vt = _norm_matmul(h.reshape(n, d), g_mix, w_main, w_vt, s, tm, w_main.shape[1] // 2)
    proj = proj.reshape(b, s, -1)
    f_hgrn, cum_col, cum_row = _gate_cum(h, g_mix, w_f, w_fbc, w_fbr, bc, br, _tile(s, 512))

    tq = _tile(s, 256)
    oa = _sb_attention(proj, vt, p_sb // sbw, 0, tq, tq)
    ob = _fox_attention(proj, vt, p_fox // foxw, sbw, cum_col, cum_row, tq, _tile(s, 512))
    c_hq = p_hgrn // hw
    oc = _hgrn(proj, f_hgrn, lb.reshape(1, hw), hgrn_ng.reshape(1, HGRN_DIM),
               c_hq, c_hq + 1, c_hq + 2, _tile(s, 512))

    gcol0 = 0
    hnew, hn = _merge_out(
        h.reshape(n, d), oa.reshape(n, sbw), ob.reshape(n, foxw), oc.reshape(n, hw),
        proj.reshape(n, -1), gcol0, b_gate,
        w_sb.astype(BF16), w_fox.astype(BF16), w_hgrn.astype(BF16), w_out.astype(BF16),
        g_ffn.reshape(1, d), hn_dtype, _tile(n, 512))
    return hnew, hn


def _moe(h, hn, w_router, w_gate, w_up, w_down, final_g, final):
    n, d = h.shape
    rblk = 512
    nrows = n * TOP_K + N_EXPERTS * rblk
    nb = nrows // rblk
    nb_pad = -(-nb // LANES) * LANES
    idx, wt, cnt = _router(hn, w_router.T.astype(BF16), _tile(n, 1024))
    dest, blk = _dest(idx, cnt, rblk, nb_pad)
    d1, d2 = dest[0], dest[1]
    buf = _dispatch(hn, d1, d2, nrows, _tile(n, 256))
    yb = _experts(buf, blk[0, :nb], blk[1, :1], w_gate.astype(BF16), w_up.astype(BF16),
                  w_down.astype(BF16), rblk, w_gate.shape[2] // 2)
    return _combine(yb, d1, d2, h, wt, final_g.reshape(1, d), _tile(n, 256), final)


def kernel(x, norm_mix_g, w_in, b_fox, b_gate, hgrn_lb_logits, hgrn_norm_g, w_branch_sb, w_branch_fox, w_branch_hgrn, w_out, norm_ffn_g, w_ffn_gate, w_ffn_up, w_ffn_down, w_router, w_exp_gate, w_exp_up, w_exp_down, final_norm_g):
    b, s, d = x.shape
    n = b * s
    depth = w_in.shape[0]
    lbs = _lower_bounds(hgrn_lb_logits)
    h = x
    for l in range(depth):
        dense = l % 2 == 0
        h2, hn = _mixer(h, norm_mix_g[l], w_in[l], b_fox[l], b_gate[l], lbs[l], hgrn_norm_g[l],
                        w_branch_sb[l], w_branch_fox[l], w_branch_hgrn[l], w_out[l], norm_ffn_g[l],
                        BF16 if dense else F32)
        last = l == depth - 1
        i = l // 2
        if dense:
            h2 = _dense_ffn(hn, h2, w_ffn_gate[i].astype(BF16), w_ffn_up[i].astype(BF16),
                            w_ffn_down[i].astype(BF16), _tile(n, 1024), 768)
            if last:
                h2 = _final_norm(h2, final_norm_g.reshape(1, d), _tile(n, 1024))
        else:
            h2 = _moe(h2, hn, w_router[i], w_exp_gate[i], w_exp_up[i], w_exp_down[i],
                      final_norm_g, last)
        h = h2.reshape(b, s, d)
    return h
```

```python
import functools

import jax
import jax.numpy as jnp
from jax import lax
from jax.experimental import pallas as pl
from jax.experimental.pallas import tpu as pltpu

F32 = jnp.float32
BF16 = jnp.bfloat16
I32 = jnp.int32

RMS_EPS = 1e-6
CHUNK = 64
SUBCHUNK = 16
SB_HEADS = 4
FOX_HEADS = 4
HGRN_HEADS = 4
HEAD_DIM = 64
HGRN_DIM = 128
N_EXPERTS = 8
TOP_K = 2
N_BRANCHES = 3

LANES = 128
SUBLANES = 8
VMEM_LIMIT = 56 * 1024 * 1024

EXP_ZERO_BELOW = -105.0
NEG_BIG = -1e30
NORM_SLACK = 1.0 + 2.0 ** -7
DMA_LOOP_UNROLL = 8


def _cparams(*sem):
    return pltpu.CompilerParams(dimension_semantics=sem, vmem_limit_bytes=VMEM_LIMIT)


def _dot(a, b):
    return jnp.dot(a, b, preferred_element_type=F32)


def _dot_nt(a, b):
    return lax.dot_general(a, b, (((1,), (1,)), ((), ())), preferred_element_type=F32)


def _dot_tn(a, b):
    return lax.dot_general(a, b, (((0,), (0,)), ((), ())), preferred_element_type=F32)


def _split2(x):
    hi = x.astype(BF16)
    lo = (x - hi.astype(F32)).astype(BF16)
    return hi, lo


def _split3(x):
    hi = x.astype(BF16)
    r = x - hi.astype(F32)
    mid = r.astype(BF16)
    lo = (r - mid.astype(F32)).astype(BF16)
    return hi, mid, lo


def _log_sigmoid(x):
    return jnp.minimum(x, 0.0) - jnp.log(1.0 + jnp.exp(-jnp.abs(x)))


def _sigmoid(x):
    return 0.5 * jnp.tanh(0.5 * x) + 0.5


def _rmsnorm(x, g):
    ms = jnp.mean(x * x, axis=-1, keepdims=True)
    return x * lax.rsqrt(ms + RMS_EPS) * g


def _tri(n, fn, dtype=BF16):
    r = lax.broadcasted_iota(I32, (n, n), 0)
    c = lax.broadcasted_iota(I32, (n, n), 1)
    return fn(r, c).astype(dtype)


def _lb_kernel(x_ref, o_ref):
    x = x_ref[...]
    depth = x.shape[0]
    m = jnp.max(x, axis=0, keepdims=True)
    e = jnp.exp(x - m)
    sm = e / jnp.sum(e, axis=0, keepdims=True)
    run = sm[0:1]
    first = run
    o_ref[0:1, :] = run - first
    for l in range(1, depth):
        run = run + sm[l:l + 1]
        o_ref[l:l + 1, :] = run - first


def _lower_bounds(logits):
    return pl.pallas_call(
        _lb_kernel,
        out_shape=jax.ShapeDtypeStruct(logits.shape, F32),
        name="hgrn_lower_bounds",
    )(logits.astype(F32))


def _norm_matmul_kernel(x_ref, g_ref, w_ref, wvt_ref, o_ref, vt_ref, xn_ref):
    @pl.when(pl.program_id(1) == 0)
    def _():
        xn = _rmsnorm(x_ref[...], g_ref[...]).astype(BF16)
        xn_ref[...] = xn
        vt_ref[0] = _dot_nt(wvt_ref[...], xn).astype(vt_ref.dtype)

    o_ref[...] = _dot(xn_ref[...], w_ref[...]).astype(o_ref.dtype)


def _norm_matmul(x, g, w, wvt, seq, tm, tn):
    n, d = x.shape
    nout = w.shape[1]
    nv = wvt.shape[0]
    per = seq // tm
    return pl.pallas_call(
        _norm_matmul_kernel,
        grid=(n // tm, nout // tn),
        in_specs=[
            pl.BlockSpec((tm, d), lambda i, j: (i, 0)),
            pl.BlockSpec((1, d), lambda i, j: (0, 0)),
            pl.BlockSpec((d, tn), lambda i, j: (0, j)),
            pl.BlockSpec((nv, d), lambda i, j: (0, 0)),
        ],
        out_specs=[
            pl.BlockSpec((tm, tn), lambda i, j: (i, j)),
            pl.BlockSpec((1, nv, tm), lambda i, j: (i // per, 0, i % per)),
        ],
        out_shape=[
            jax.ShapeDtypeStruct((n, nout), BF16),
            jax.ShapeDtypeStruct((n // seq, nv, seq), BF16),
        ],
        scratch_shapes=[pltpu.VMEM((tm, d), BF16)],
        compiler_params=_cparams("parallel", "arbitrary"),
        name="norm_in_proj",
    )(x, g, w, wvt)


def _gate_cum_kernel(x_ref, g_ref, wf_ref, wc_ref, wr_ref, bc_ref, br_ref, lt_ref, ut_ref,
                     f_ref, cc_ref, cr_ref, carc_ref, carr_ref):
    tm = x_ref.shape[1]

    @pl.when(pl.program_id(1) == 0)
    def _():
        carc_ref[...] = jnp.zeros_like(carc_ref)
        carr_ref[...] = jnp.zeros_like(carr_ref)

    xn = _rmsnorm(x_ref[0], g_ref[...]).astype(BF16)
    f_ref[0] = _dot(xn, wf_ref[...])

    lc = _log_sigmoid(_dot(xn, wc_ref[...]) + bc_ref[...])
    lt = lt_ref[...]
    c_hi, c_mid, c_lo = _split3(lc)
    cum_c = _dot(lt, c_hi) + _dot(lt, c_mid) + _dot(lt, c_lo) + carc_ref[0:1, :]
    cc_ref[0] = cum_c
    carc_ref[...] = jnp.broadcast_to(cum_c[tm - 1:tm, :], carc_ref.shape)

    lr = _log_sigmoid(_dot_nt(wr_ref[...], xn) + br_ref[:, 0:1])
    ut = ut_ref[...]
    r_hi, r_mid, r_lo = _split3(lr)
    cum_r = _dot(r_hi, ut) + _dot(r_mid, ut) + _dot(r_lo, ut) + carr_ref[:, 0:1]
    cr_ref[0] = cum_r
    carr_ref[...] = jnp.broadcast_to(cum_r[:, tm - 1:tm], carr_ref.shape)


def _gate_cum(h, g, wf, wc, wr, bc, br, tm):
    b, s, d = h.shape
    nf = wf.shape[1]
    lt = _tri(tm, lambda r, c: c <= r)
    ut = _tri(tm, lambda r, c: r <= c)
    full = lambda shape: pl.BlockSpec(shape, lambda bi, si: (0,) * len(shape))
    return pl.pallas_call(
        _gate_cum_kernel,
        grid=(b, s // tm),
        in_specs=[
            pl.BlockSpec((1, tm, d), lambda bi, si: (bi, si, 0)),
            full((1, d)), full((d, nf)), full((d, LANES)), full((SUBLANES, d)),
            full((1, LANES)), full((SUBLANES, LANES)), full((tm, tm)), full((tm, tm)),
        ],
        out_specs=[
            pl.BlockSpec((1, tm, nf), lambda bi, si: (bi, si, 0)),
            pl.BlockSpec((1, tm, LANES), lambda bi, si: (bi, si, 0)),
            pl.BlockSpec((1, SUBLANES, tm), lambda bi, si: (bi, 0, si)),
        ],
        out_shape=[
            jax.ShapeDtypeStruct((b, s, nf), F32),
            jax.ShapeDtypeStruct((b, s, LANES), F32),
            jax.ShapeDtypeStruct((b, SUBLANES, s), F32),
        ],
        scratch_shapes=[pltpu.VMEM((SUBLANES, LANES), F32), pltpu.VMEM((SUBLANES, LANES), F32)],
        compiler_params=_cparams("parallel", "arbitrary"),
        name="gate_proj_cum",
    )(h, g, wf, wc, wr, bc, br, lt, ut)


def _masked_queries(q_ref, qm_ref, nheads, hd, scale):
    q = q_ref[0]
    lane = lax.broadcasted_iota(I32, q.shape, 1)
    qs = q * jnp.asarray(scale, q.dtype)
    for h in range(nheads):
        sel = jnp.logical_and(lane >= h * hd, lane < (h + 1) * hd)
        qm_ref[h] = jnp.where(sel, qs, jnp.zeros_like(qs))


def _sb_kernel(q_ref, k_ref, vt_ref, u_ref, o_ref, qm_ref, car_ref, acc_ref, e_ref, tot_ref,
               *, tq, tk, nheads, hd, vrow0):
    i = pl.program_id(1)
    _masked_queries(q_ref, qm_ref, nheads, hd, hd ** -0.5)
    car_ref[...] = jnp.zeros_like(car_ref)
    acc_ref[...] = jnp.zeros_like(acc_ref)
    u = u_ref[...]

    def log_pass(j, slot, diag, valid=None):
        start = pl.multiple_of(j * tk, tk)
        kb = k_ref[0, pl.ds(start, tk), :]
        if diag:
            srow = start + lax.broadcasted_iota(I32, (tk, tq), 0)
            tcol = i * tq + lax.broadcasted_iota(I32, (tk, tq), 1)
            strict = srow < tcol
        for h in range(nheads):
            z = _dot_nt(kb, qm_ref[h])
            lg = -(jnp.maximum(z, 0.0) + jnp.log(1.0 + jnp.exp(-jnp.abs(z))))
            if diag:
                lg = jnp.where(strict, lg, 0.0)
            l_hi, l_lo = _split2(lg)
            suf = _dot(u, l_hi) + _dot(u, l_lo)
            e = z + lg + suf
            tot = suf[0:1, :] + lg[0:1, :]
            if diag:
                e = jnp.where(strict, e, NEG_BIG)
            if valid is not None:
                e = jnp.where(valid, e, NEG_BIG)
                tot = jnp.where(valid, tot, 0.0)
            e_ref[slot * nheads + h] = e
            tot_ref[slot * nheads + h] = tot

    def value_pass(j, slot):
        start = pl.multiple_of(j * tk, tk)
        alive = None
        for h in range(nheads):
            carry = car_ref[h]
            w = jnp.exp(e_ref[slot * nheads + h] + carry)
            vt = vt_ref[0, vrow0 + h * hd:vrow0 + (h + 1) * hd, pl.ds(start, tk)]
            acc_ref[h] += _dot(vt, w.astype(BF16))
            carry = carry + tot_ref[slot * nheads + h]
            car_ref[h] = carry
            top = jnp.max(carry)
            alive = top if alive is None else jnp.maximum(alive, top)
        return alive

    j0 = (i * tq) // tk
    j1 = jnp.maximum(j0 - 1, 0)
    log_pass(j0, 0, True)
    log_pass(j1, 1, False, valid=j0 >= 1)
    value_pass(j0, 0)
    alive0 = value_pass(j1, 1)

    def cond(c):
        j, alive = c
        return jnp.logical_and(j >= 0, alive > EXP_ZERO_BELOW)

    def body(c):
        j, _ = c
        log_pass(j, 0, False)
        return j - 1, value_pass(j, 0)

    lax.while_loop(cond, body, (j0 - 2, alive0))
    o_ref[0] = acc_ref[...].reshape(nheads * hd, tq).T.astype(o_ref.dtype)


def _sb_attention(proj, vt, col0, vrow0, tq, tk):
    b, s, _ = proj.shape
    width = SB_HEADS * HEAD_DIM
    nvt = vt.shape[1]
    u = _tri(tk, lambda r, c: c > r)
    kern = functools.partial(_sb_kernel, tq=tq, tk=tk, nheads=SB_HEADS, hd=HEAD_DIM, vrow0=vrow0)
    return pl.pallas_call(
        kern,
        grid=(b, s // tq),
        in_specs=[
            pl.BlockSpec((1, tq, width), lambda bi, i: (bi, i, col0)),
            pl.BlockSpec((1, s, width), lambda bi, i: (bi, 0, col0 + 1)),
            pl.BlockSpec((1, nvt, s), lambda bi, i: (bi, 0, 0)),
            pl.BlockSpec((tk, tk), lambda bi, i: (0, 0)),
        ],
        out_specs=pl.BlockSpec((1, tq, width), lambda bi, i: (bi, i, 0)),
        out_shape=jax.ShapeDtypeStruct((b, s, width), BF16),
        scratch_shapes=[pltpu.VMEM((SB_HEADS, tq, width), BF16),
                        pltpu.VMEM((SB_HEADS, 1, tq), F32),
                        pltpu.VMEM((SB_HEADS, HEAD_DIM, tq), F32),
                        pltpu.VMEM((2 * SB_HEADS, tk, tq), F32),
                        pltpu.VMEM((2 * SB_HEADS, 1, tq), F32)],
        compiler_params=_cparams("parallel", "arbitrary"),
        name="stick_breaking_attention",
    )(proj, proj, vt, u)


def _fox_kernel(q_ref, k_ref, vt_ref, cc_ref, cr_ref, kmax_ref, cend_ref, et_ref, o_ref,
                qm_ref, s_ref, mx_ref, m_ref, l_ref, acc_ref, qn_ref, *, tq, tk, nheads, hd, vrow0):
    i = pl.program_id(1)
    _masked_queries(q_ref, qm_ref, nheads, hd, hd ** -0.5)
    m_ref[...] = jnp.full(m_ref.shape, NEG_BIG, F32)
    l_ref[...] = jnp.zeros_like(l_ref)
    acc_ref[...] = jnp.zeros_like(acc_ref)

    def score_pass(j, slot, diag):
        start = pl.multiple_of(j * tk, tk)
        kb = k_ref[0, pl.ds(start, tk), :]
        if diag:
            srow = start + lax.broadcasted_iota(I32, (tk, tq), 0)
            tcol = i * tq + lax.broadcasted_iota(I32, (tk, tq), 1)
            causal = srow <= tcol
        for h in range(nheads):
            cs = cc_ref[0, pl.ds(start, tk), h:h + 1]
            s = _dot_nt(kb, qm_ref[h]) - cs
            if diag:
                s = jnp.where(causal, s, NEG_BIG)
            s_ref[slot * nheads + h] = s
            mx_ref[slot * nheads + h] = jnp.max(s, axis=0, keepdims=True)

    def value_pass(j, slot):
        start = pl.multiple_of(j * tk, tk)
        for h in range(nheads):
            ct = cr_ref[0, h:h + 1, :]
            m = m_ref[h]
            m_new = jnp.maximum(m, mx_ref[slot * nheads + h] + ct)
            alpha = jnp.exp(m - m_new)
            p = jnp.exp(s_ref[slot * nheads + h] - (m_new - ct))
            l_ref[h] = alpha * l_ref[h] + jnp.sum(p, axis=0, keepdims=True)
            vt = vt_ref[0, vrow0 + h * hd:vrow0 + (h + 1) * hd, pl.ds(start, tk)]
            acc_ref[h] = alpha * acc_ref[h] + _dot(vt, p.astype(BF16))
            m_ref[h] = m_new

    qf = q_ref[0].astype(F32) * (hd ** -0.5)
    sq_hi, sq_lo = _split2(qf * qf)
    et = et_ref[...]
    qn_ref[...] = jnp.sqrt(_dot_nt(et, sq_hi) + _dot_nt(et, sq_lo)) * NORM_SLACK

    def bound_below(j):
        sel = lax.broadcasted_iota(I32, (SUBLANES, LANES), 1) == j
        kmax = jnp.max(jnp.where(sel, kmax_ref[0], 0.0), axis=1, keepdims=True)
        cend = jnp.sum(jnp.where(sel, cend_ref[0], 0.0), axis=1, keepdims=True)
        best = None
        for h in range(nheads):
            b = (qn_ref[h:h + 1, :] * kmax[h:h + 1, :] + cr_ref[0, h:h + 1, :]
                 - cend[h:h + 1, :] - m_ref[h])
            best = b if best is None else jnp.maximum(best, b)
        return jnp.max(best)

    j0 = (i * tq) // tk
    score_pass(j0, 0, True)

    def cond(c):
        it, alive = c
        return jnp.logical_and(it < j0, alive > EXP_ZERO_BELOW)

    def body(c):
        it, _ = c
        slot = lax.rem(it, 2)
        score_pass(j0 - 1 - it, 1 - slot, False)
        value_pass(j0 - it, slot)
        return it + 1, bound_below(j0 - 2 - it)

    done, _ = lax.while_loop(cond, body, (jnp.int32(0), jnp.float32(0.0)))
    value_pass(j0 - done, lax.rem(done, 2))
    out = acc_ref[...] / l_ref[...]
    o_ref[0] = out.reshape(nheads * hd, tq).T.astype(o_ref.dtype)


def _head_selector(nheads, hd):
    r = lax.broadcasted_iota(I32, (SUBLANES, nheads * hd), 0)
    c = lax.broadcasted_iota(I32, (SUBLANES, nheads * hd), 1)
    return (c // hd == r).astype(BF16)


def _key_bounds_kernel(k_ref, et_ref, o_ref, run_ref):
    j = pl.program_id(1)

    @pl.when(j == 0)
    def _():
        run_ref[...] = jnp.zeros_like(run_ref)
        o_ref[...] = jnp.zeros_like(o_ref)

    kf = k_ref[0].astype(F32)
    sq_hi, sq_lo = _split2(kf * kf)
    et = et_ref[...]
    n2 = _dot_nt(et, sq_hi) + _dot_nt(et, sq_lo)
    tile_max = jnp.sqrt(jnp.max(n2, axis=1, keepdims=True)) * NORM_SLACK
    run = jnp.maximum(run_ref[...], tile_max)
    run_ref[...] = run
    lane = lax.broadcasted_iota(I32, (SUBLANES, LANES), 1)
    o_ref[0] = jnp.where(lane == j, run, o_ref[0])


def _key_bounds(proj, colk, tk):
    b, s, _ = proj.shape
    width = FOX_HEADS * HEAD_DIM
    assert s // tk <= LANES
    return pl.pallas_call(
        _key_bounds_kernel,
        grid=(b, s // tk),
        in_specs=[
            pl.BlockSpec((1, tk, width), lambda bi, j: (bi, j, colk)),
            pl.BlockSpec((SUBLANES, width), lambda bi, j: (0, 0)),
        ],
        out_specs=pl.BlockSpec((1, SUBLANES, LANES), lambda bi, j: (bi, 0, 0)),
        out_shape=jax.ShapeDtypeStruct((b, SUBLANES, LANES), F32),
        scratch_shapes=[pltpu.VMEM((SUBLANES, LANES), F32)],
        compiler_params=_cparams("parallel", "arbitrary"),
        name="fox_key_norm_bounds",
    )(proj, _head_selector(FOX_HEADS, HEAD_DIM))


def _fox_attention(proj, vt, col0, vrow0, cum_col, cum_row, tq, tk):
    b, s, _ = proj.shape
    width = FOX_HEADS * HEAD_DIM
    nvt = vt.shape[1]
    kmax = _key_bounds(proj, col0 + 1, tk)
    cend = jnp.pad(cum_row[:, :, tk - 1::tk], ((0, 0), (0, 0), (0, LANES - s // tk)))
    kern = functools.partial(_fox_kernel, tq=tq, tk=tk, nheads=FOX_HEADS, hd=HEAD_DIM, vrow0=vrow0)
    return pl.pallas_call(
        kern,
        grid=(b, s // tq),
        in_specs=[
            pl.BlockSpec((1, tq, width), lambda bi, i: (bi, i, col0)),
            pl.BlockSpec((1, s, width), lambda bi, i: (bi, 0, col0 + 1)),
            pl.BlockSpec((1, nvt, s), lambda bi, i: (bi, 0, 0)),
            pl.BlockSpec((1, s, LANES), lambda bi, i: (bi, 0, 0)),
            pl.BlockSpec((1, SUBLANES, tq), lambda bi, i: (bi, 0, i)),
            pl.BlockSpec((1, SUBLANES, LANES), lambda bi, i: (bi, 0, 0)),
            pl.BlockSpec((1, SUBLANES, LANES), lambda bi, i: (bi, 0, 0)),
            pl.BlockSpec((SUBLANES, width), lambda bi, i: (0, 0)),
        ],
        out_specs=pl.BlockSpec((1, tq, width), lambda bi, i: (bi, i, 0)),
        out_shape=jax.ShapeDtypeStruct((b, s, width), BF16),
        scratch_shapes=[pltpu.VMEM((FOX_HEADS, tq, width), BF16),
                        pltpu.VMEM((2 * FOX_HEADS, tk, tq), F32),
                        pltpu.VMEM((2 * FOX_HEADS, 1, tq), F32),
                        pltpu.VMEM((FOX_HEADS, 1, tq), F32),
                        pltpu.VMEM((FOX_HEADS, 1, tq), F32),
                        pltpu.VMEM((FOX_HEADS, HEAD_DIM, tq), F32),
                        pltpu.VMEM((SUBLANES, tq), F32)],
        compiler_params=_cparams("parallel", "arbitrary"),
        name="forgetting_attention",
    )(proj, proj, vt, cum_col, cum_row, kmax, cend, _head_selector(FOX_HEADS, HEAD_DIM))


HGRN_SAFE_SPAN = 40.0
HGRN_GROUP = 4


def _hgrn_kernel(q_ref, f_ref, i_ref, g_ref, lb_ref, ng_ref, lt_ref, o_ref,
                 lf_ref, kk_ref, qq_ref, b_ref, oi_ref, st_ref, *, ts, nheads, dk):
    @pl.when(pl.program_id(1) == 0)
    def _():
        st_ref[...] = jnp.zeros_like(st_ref)

    lb = lb_ref[...]
    z = f_ref[0]
    lsz = _log_sigmoid(z)
    a = jnp.log(lb)
    c = jnp.log(1.0 - lb) + lsz
    lf_ref[...] = jnp.maximum(a, c) + jnp.log(1.0 + jnp.exp(-jnp.abs(a - c)))
    kk_ref[...] = (1.0 - lb) * jnp.exp(lsz - z)
    qin = q_ref[0].astype(F32)
    qq_ref[...] = qin * _sigmoid(qin)

    lt = lt_ref[...]
    nsub = CHUNK // SUBCHUNK
    rr = lax.broadcasted_iota(I32, (SUBCHUNK, SUBCHUNK), 0)
    cc = lax.broadcasted_iota(I32, (SUBCHUNK, SUBCHUNK), 1)
    causal = cc <= rr
    sub = lax.broadcasted_iota(I32, (SUBLANES, SUBCHUNK), 0)
    causal_c = (lax.broadcasted_iota(I32, (CHUNK, CHUNK), 1)
                <= lax.broadcasted_iota(I32, (CHUNK, CHUNK), 0))
    row_blk = lax.broadcasted_iota(I32, (CHUNK, dk), 0) // SUBCHUNK
    ng = ng_ref[...]

    def prepare(r0):
        rows = pl.ds(r0, CHUNK)
        h_hi, h_mid, h_lo = _split3(lf_ref[rows, :])
        bcum = _dot(lt, h_hi) + _dot(lt, h_mid) + _dot(lt, h_lo)
        b_ref[rows, :] = bcum
        span = -bcum[SUBCHUNK - 1:SUBCHUNK, :]
        for bi in range(1, nsub):
            lo = bi * SUBCHUNK
            span = jnp.maximum(span, bcum[lo - 1:lo, :] - bcum[lo + SUBCHUNK - 1:lo + SUBCHUNK, :])
        return (r0, bcum, kk_ref[rows, :], qq_ref[rows, :], i_ref[0, rows, :]), span

    def intra_exact(ck, h, bi):
        r0, bcum, kk, qq, vv = ck
        hs = slice(h * dk, (h + 1) * dk)
        lo = bi * SUBCHUNK
        hi = lo + SUBCHUNK
        b_i = bcum[lo:hi, hs]
        kk_i = kk[lo:hi, hs]
        groups = []
        for gi in range(SUBCHUNK // SUBLANES):
            g0 = lo + gi * SUBLANES
            qg = qq[g0:g0 + SUBLANES, hs].astype(BF16)
            dg = jnp.zeros((SUBLANES, SUBCHUNK), F32)
            for r in range(SUBLANES):
                bt = bcum[g0 + r:g0 + r + 1, hs]
                ke = kk_i * jnp.exp(jnp.minimum(bt - b_i, 0.0))
                res = _dot_nt(qg, ke.astype(BF16))
                dg = jnp.where(sub == r, res, dg)
            groups.append(dg)
        dblk = jnp.where(causal, jnp.concatenate(groups, axis=0), 0.0)
        o_i = _dot(dblk.astype(BF16), vv[lo:hi, hs])
        if bi > 0:
            bref = bcum[lo - 1:lo, hs]
            q_i = (qq[lo:hi, hs] * jnp.exp(b_i - bref)).astype(BF16)
            k_p = (kk[0:lo, hs] * jnp.exp(bref - bcum[0:lo, hs])).astype(BF16)
            o_i = o_i + _dot(_dot_nt(q_i, k_p).astype(BF16), vv[0:lo, hs])
        return o_i

    def intra_fast(ck, h):
        r0, bcum, kk, qq, vv = ck
        hs = slice(h * dk, (h + 1) * dk)
        bh = bcum[:, hs]
        refs = [jnp.zeros((1, dk), F32)] + [bh[bi * SUBCHUNK - 1:bi * SUBCHUNK, :] for bi in range(1, nsub)]
        ref_rows = jnp.concatenate([jnp.broadcast_to(r, (SUBCHUNK, dk)) for r in refs], axis=0)
        q_stack = qq[:, hs] * jnp.exp(bh - ref_rows)
        q_big = jnp.concatenate(
            [jnp.where(row_blk == bi, q_stack, 0.0) for bi in range(nsub)], axis=1).astype(BF16)
        k_big = jnp.concatenate(
            [kk[:, hs] * jnp.exp(jnp.minimum(r - bh, HGRN_SAFE_SPAN)) for r in refs], axis=1).astype(BF16)
        s_all = jnp.where(causal_c, _dot_nt(q_big, k_big), 0.0)
        return _dot(s_all.astype(BF16), vv[:, hs])

    def phase1(gi, carry):
        cks, spans = [], None
        for c in range(HGRN_GROUP):
            ck, span = prepare(pl.multiple_of((gi * HGRN_GROUP + c) * CHUNK, CHUNK))
            cks.append(ck)
            spans = span if spans is None else jnp.maximum(spans, span)
        safe = jnp.max(spans) < HGRN_SAFE_SPAN

        @pl.when(safe)
        def _():
            for ck in cks:
                for h in range(nheads):
                    oi_ref[pl.ds(ck[0], CHUNK), h * dk:(h + 1) * dk] = intra_fast(ck, h)

        @pl.when(jnp.logical_not(safe))
        def _():
            for ck in cks:
                for h in range(nheads):
                    for bi in range(nsub):
                        oi_ref[pl.ds(ck[0] + bi * SUBCHUNK, SUBCHUNK), h * dk:(h + 1) * dk] = (
                            intra_exact(ck, h, bi))
        return carry

    lax.fori_loop(0, ts // (CHUNK * HGRN_GROUP), phase1, 0)

    def phase2(ci, carry):
        r0 = pl.multiple_of(ci * CHUNK, CHUNK)
        rows = pl.ds(r0, CHUNK)
        for h in range(nheads):
            hs = slice(h * dk, (h + 1) * dk)
            st = st_ref[h]
            bh = b_ref[rows, hs]
            qq = qq_ref[rows, hs]
            kk = kk_ref[rows, hs]
            vv = i_ref[0, rows, hs]
            qe = (qq * jnp.exp(bh)).astype(BF16)
            o = oi_ref[rows, hs] + _dot_nt(qe, st.astype(BF16))
            b_last = bh[CHUNK - 1:CHUNK, :]
            kd = (kk * jnp.exp(b_last - bh)).astype(BF16)
            st_ref[h] = st * jnp.exp(b_last) + _dot_tn(vv, kd)
            gate = g_ref[0, rows, hs].astype(F32)
            on = _rmsnorm(o, ng) * (gate * _sigmoid(gate))
            o_ref[0, rows, hs] = on.astype(o_ref.dtype)
        return carry

    lax.fori_loop(0, ts // CHUNK, phase2, 0, unroll=HGRN_GROUP)


def _hgrn(proj, f_hgrn, lb, ng, colq, coli, colg, ts):
    b, s, _ = proj.shape
    dk = HGRN_DIM
    hw = HGRN_HEADS * dk
    lt = _tri(CHUNK, lambda r, c: c <= r)
    kern = functools.partial(_hgrn_kernel, ts=ts, nheads=HGRN_HEADS, dk=dk)
    return pl.pallas_call(
        kern,
        grid=(b, s // ts),
        in_specs=[
            pl.BlockSpec((1, ts, hw), lambda bi, si: (bi, si, colq)),
            pl.BlockSpec((1, ts, hw), lambda bi, si: (bi, si, 0)),
            pl.BlockSpec((1, ts, hw), lambda bi, si: (bi, si, coli)),
            pl.BlockSpec((1, ts, hw), lambda bi, si: (bi, si, colg)),
            pl.BlockSpec((1, hw), lambda bi, si: (0, 0)),
            pl.BlockSpec((1, dk), lambda bi, si: (0, 0)),
            pl.BlockSpec((CHUNK, CHUNK), lambda bi, si: (0, 0)),
        ],
        out_specs=pl.BlockSpec((1, ts, hw), lambda bi, si: (bi, si, 0)),
        out_shape=jax.ShapeDtypeStruct((b, s, hw), BF16),
        scratch_shapes=[
            pltpu.VMEM((ts, hw), F32), pltpu.VMEM((ts, hw), F32), pltpu.VMEM((ts, hw), F32),
            pltpu.VMEM((ts, hw), F32), pltpu.VMEM((ts, hw), F32),
            pltpu.VMEM((HGRN_HEADS, dk, dk), F32),
        ],
        compiler_params=_cparams("parallel", "arbitrary"),
        name="hgrn2_recurrence",
    )(proj, f_hgrn, proj, proj, lb, ng, lt)


def _merge_kernel(h_ref, oa_ref, ob_ref, oc_ref, g0_ref, g1_ref, g2_ref, bg_ref,
                  wa_ref, wb_ref, wc_ref, wo_ref, ng_ref, hout_ref, hn_ref):
    def gate(gref, k):
        x = gref[...].astype(F32) + bg_ref[k:k + 1, :]
        return _sigmoid(x)

    merged = (gate(g0_ref, 0) * _dot(oa_ref[...], wa_ref[...])
              + gate(g1_ref, 1) * _dot(ob_ref[...], wb_ref[...])
              + gate(g2_ref, 2) * _dot(oc_ref[...], wc_ref[...]))
    hnew = h_ref[...] + _dot(merged.astype(BF16), wo_ref[...])
    hout_ref[...] = hnew
    hn_ref[...] = _rmsnorm(hnew, ng_ref[...]).astype(hn_ref.dtype)


def _merge_out(h, oa, ob, oc, proj, gcol0, bg, wa, wb, wc, wo, ng, hn_dtype, tm):
    n, d = h.shape
    full = lambda shape: pl.BlockSpec(shape, lambda i: (0,) * len(shape))
    rows = lambda w: pl.BlockSpec((tm, w), lambda i: (i, 0))
    return pl.pallas_call(
        _merge_kernel,
        grid=(n // tm,),
        in_specs=[
            rows(d), rows(oa.shape[1]), rows(ob.shape[1]), rows(oc.shape[1]),
            pl.BlockSpec((tm, d), lambda i: (i, gcol0)),
            pl.BlockSpec((tm, d), lambda i: (i, gcol0 + 1)),
            pl.BlockSpec((tm, d), lambda i: (i, gcol0 + 2)),
            full(bg.shape), full(wa.shape), full(wb.shape), full(wc.shape), full(wo.shape),
            full((1, d)),
        ],
        out_specs=[rows(d), rows(d)],
        out_shape=[jax.ShapeDtypeStruct((n, d), F32), jax.ShapeDtypeStruct((n, d), hn_dtype)],
        compiler_params=_cparams("parallel"),
        name="merge_out_proj",
    )(h, oa, ob, oc, proj, proj, proj, bg, wa, wb, wc, wo, ng)


def _ffn_kernel(hn_ref, h_ref, wg_ref, wu_ref, wd_ref, o_ref, acc_ref):
    j = pl.program_id(1)

    @pl.when(j == 0)
    def _():
        acc_ref[...] = jnp.zeros_like(acc_ref)

    x = hn_ref[...]
    gt = _dot(x, wg_ref[...])
    up = _dot(x, wu_ref[...])
    act = (gt / (1.0 + jnp.exp(-gt))) * up
    acc_ref[...] += _dot(act.astype(BF16), wd_ref[...])

    @pl.when(j == pl.num_programs(1) - 1)
    def _():
        o_ref[...] = h_ref[...] + acc_ref[...]


def _dense_ffn(hn, h, wg, wu, wd, tm, tf):
    n, d = h.shape
    ff = wg.shape[1]
    return pl.pallas_call(
        _ffn_kernel,
        grid=(n // tm, ff // tf),
        in_specs=[
            pl.BlockSpec((tm, d), lambda i, j: (i, 0)),
            pl.BlockSpec((tm, d), lambda i, j: (i, 0)),
            pl.BlockSpec((d, tf), lambda i, j: (0, j)),
            pl.BlockSpec((d, tf), lambda i, j: (0, j)),
            pl.BlockSpec((tf, d), lambda i, j: (j, 0)),
        ],
        out_specs=pl.BlockSpec((tm, d), lambda i, j: (i, 0)),
        out_shape=jax.ShapeDtypeStruct((n, d), F32),
        scratch_shapes=[pltpu.VMEM((tm, d), F32)],
        compiler_params=_cparams("parallel", "arbitrary"),
        name="dense_swiglu",
    )(hn, h, wg, wu, wd)


def _router_kernel(hn_ref, wr_ref, u_ref, idx_ref, wt_ref, cnt_ref, car_ref):
    t = hn_ref.shape[0]

    @pl.when(pl.program_id(0) == 0)
    def _():
        car_ref[...] = jnp.zeros_like(car_ref)

    logits = _dot_nt(wr_ref[...], hn_ref[...].astype(BF16))
    eidx = lax.broadcasted_iota(I32, (N_EXPERTS, t), 0)
    m1 = jnp.max(logits, axis=0, keepdims=True)
    i1 = jnp.min(jnp.where(logits == m1, eidx, N_EXPERTS), axis=0, keepdims=True)
    sel1 = eidx == i1
    rest = jnp.where(sel1, -jnp.inf, logits)
    m2 = jnp.max(rest, axis=0, keepdims=True)
    i2 = jnp.min(jnp.where(rest == m2, eidx, N_EXPERTS), axis=0, keepdims=True)
    sel2 = eidx == i2
    e21 = jnp.exp(m2 - m1)
    w1 = 1.0 / (1.0 + e21)
    w2 = e21 / (1.0 + e21)

    cnt = jnp.where(jnp.logical_or(sel1, sel2), 1.0, 0.0)
    excl = _dot(cnt.astype(BF16), u_ref[...])
    rank = excl + car_ref[:, 0:1]
    p1 = jnp.sum(jnp.where(sel1, rank, 0.0), axis=0, keepdims=True)
    p2 = jnp.sum(jnp.where(sel2, rank, 0.0), axis=0, keepdims=True)
    total = rank[:, t - 1:t] + cnt[:, t - 1:t]
    car_ref[...] = jnp.broadcast_to(total, car_ref.shape)
    cnt_ref[...] = jnp.broadcast_to(total, cnt_ref.shape)

    zi = jnp.zeros((1, t), I32)
    idx_ref[...] = jnp.concatenate(
        [i1, i2, p1.astype(I32), p2.astype(I32), zi, zi, zi, zi], axis=0)
    zf = jnp.zeros((1, t), F32)
    wt_ref[...] = jnp.concatenate([w1, w2, zf, zf, zf, zf, zf, zf], axis=0)


def _router(hn, wr_t, t):
    n, d = hn.shape
    u = _tri(t, lambda r, c: r < c)
    return pl.pallas_call(
        _router_kernel,
        grid=(n // t,),
        in_specs=[
            pl.BlockSpec((t, d), lambda i: (i, 0)),
            pl.BlockSpec((N_EXPERTS, d), lambda i: (0, 0)),
            pl.BlockSpec((t, t), lambda i: (0, 0)),
        ],
        out_specs=[
            pl.BlockSpec((SUBLANES, t), lambda i: (0, i)),
            pl.BlockSpec((SUBLANES, t), lambda i: (0, i)),
            pl.BlockSpec((N_EXPERTS, LANES), lambda i: (0, 0)),
        ],
        out_shape=[
            jax.ShapeDtypeStruct((SUBLANES, n), I32),
            jax.ShapeDtypeStruct((SUBLANES, n), F32),
            jax.ShapeDtypeStruct((N_EXPERTS, LANES), F32),
        ],
        scratch_shapes=[pltpu.VMEM((N_EXPERTS, LANES), F32)],
        compiler_params=_cparams("arbitrary"),
        name="moe_router_top2",
    )(hn, wr_t, u)


def _dest_kernel(idx_ref, cnt_ref, dest_ref, blk_ref, *, rblk):
    n = idx_ref.shape[1]
    nb = blk_ref.shape[1]
    shift = rblk.bit_length() - 1
    assert rblk == 1 << shift
    e1 = idx_ref[0:1, :]
    e2 = idx_ref[1:2, :]
    d1 = idx_ref[2:3, :]
    d2 = idx_ref[3:4, :]
    bstart = lax.broadcasted_iota(I32, (1, nb), 1) * rblk
    bexp = jnp.zeros((1, nb), I32)
    pend = jnp.zeros((1, 1), I32)
    for e in range(N_EXPERTS):
        c = cnt_ref[e:e + 1, 0:1].astype(I32)
        padded = lax.shift_left(lax.shift_right_logical(c + (rblk - 1), shift), shift)
        pstart = pend
        pend = pend + padded
        d1 = d1 + jnp.where(e1 == e, pstart, 0)
        d2 = d2 + jnp.where(e2 == e, pstart, 0)
        bexp = bexp + jnp.where(bstart >= pend, 1, 0)
    zi = jnp.zeros((1, n), I32)
    dest_ref[...] = jnp.concatenate([d1, d2, zi, zi, zi, zi, zi, zi], axis=0)
    zb = jnp.zeros((1, nb), I32)
    nvalid = jnp.broadcast_to(lax.shift_right_logical(pend, shift), (1, nb))
    blk_ref[...] = jnp.concatenate(
        [jnp.minimum(bexp, N_EXPERTS - 1), nvalid, zb, zb, zb, zb, zb, zb], axis=0)


def _dest(idx, cnt, rblk, nb_pad):
    n = idx.shape[1]
    return pl.pallas_call(
        functools.partial(_dest_kernel, rblk=rblk),
        out_shape=[
            jax.ShapeDtypeStruct((SUBLANES, n), I32),
            jax.ShapeDtypeStruct((SUBLANES, nb_pad), I32),
        ],
        compiler_params=pltpu.CompilerParams(vmem_limit_bytes=VMEM_LIMIT),
        name="moe_slot_assign",
    )(idx, cnt)


def _dispatch_kernel(d1_ref, d2_ref, x_ref, buf_in, buf_hbm, sem, *, t):
    del buf_in

    def row_copy(src_row, dst_row):
        return pltpu.make_async_copy(x_ref.at[pl.ds(src_row, 1)], buf_hbm.at[pl.ds(dst_row, 1)], sem)

    def start(r, c):
        row_copy(r, d1_ref[r]).start()
        row_copy(r, d2_ref[r]).start(priority=1)
        return c

    lax.fori_loop(0, t, start, 0, unroll=DMA_LOOP_UNROLL)

    def wait(r, c):
        row_copy(r, d1_ref[r]).wait()
        row_copy(r, d2_ref[r]).wait()
        return c

    lax.fori_loop(0, t, wait, 0, unroll=DMA_LOOP_UNROLL)


def _dispatch(x, d1, d2, nrows, t):
    n, d = x.shape
    buf0 = jnp.zeros((nrows, d), x.dtype)
    smem = lambda: pl.BlockSpec((t,), lambda i: (i,), memory_space=pltpu.SMEM)
    return pl.pallas_call(
        functools.partial(_dispatch_kernel, t=t),
        grid=(n // t,),
        in_specs=[smem(), smem(),
                  pl.BlockSpec((t, d), lambda i: (i, 0)), pl.BlockSpec(memory_space=pl.ANY)],
        out_specs=pl.BlockSpec(memory_space=pl.ANY),
        out_shape=jax.ShapeDtypeStruct((nrows, d), x.dtype),
        scratch_shapes=[pltpu.SemaphoreType.DMA(())],
        input_output_aliases={3: 0},
        compiler_params=_cparams("arbitrary"),
        name="moe_dispatch_rows",
    )(d1, d2, x, buf0)


def _expert_kernel(be_ref, nv_ref, x_ref, wg_ref, wu_ref, wd_ref, o_ref, xb_ref):
    i = pl.program_id(0)
    j = pl.program_id(1)

    @pl.when(j == 0)
    def _():
        o_ref[...] = jnp.zeros_like(o_ref)
        xb_ref[...] = x_ref[...].astype(BF16)

    @pl.when(i < nv_ref[0])
    def _():
        x = xb_ref[...]
        gt = _dot(x, wg_ref[0])
        up = _dot(x, wu_ref[0])
        act = (gt / (1.0 + jnp.exp(-gt))) * up
        o_ref[...] += _dot(act.astype(BF16), wd_ref[0])


def _experts(buf, be, nv, wg, wu, wd, rblk, tf):
    nrows, d = buf.shape
    ff = wg.shape[2]
    nj = ff // tf

    def wcol(i, j, be_ref, nv_ref):
        return (be_ref[i], 0, jnp.where(i < nv_ref[0], j, nj - 1))

    def wrow(i, j, be_ref, nv_ref):
        return (be_ref[i], jnp.where(i < nv_ref[0], j, nj - 1), 0)

    grid_spec = pltpu.PrefetchScalarGridSpec(
        num_scalar_prefetch=2,
        grid=(nrows // rblk, nj),
        in_specs=[
            pl.BlockSpec((rblk, d), lambda i, j, be_ref, nv_ref: (i, 0)),
            pl.BlockSpec((1, d, tf), wcol),
            pl.BlockSpec((1, d, tf), wcol),
            pl.BlockSpec((1, tf, d), wrow),
        ],
        out_specs=pl.BlockSpec((rblk, d), lambda i, j, be_ref, nv_ref: (i, 0)),
        scratch_shapes=[pltpu.VMEM((rblk, d), BF16)],
    )
    return pl.pallas_call(
        _expert_kernel,
        grid_spec=grid_spec,
        out_shape=jax.ShapeDtypeStruct((nrows, d), F32),
        compiler_params=_cparams("parallel", "arbitrary"),
        name="moe_grouped_swiglu",
    )(be, nv, buf, wg, wu, wd)


def _combine_kernel(d1_ref, d2_ref, yb_hbm, h_ref, wt_ref, ng_ref, o_ref, y_ref, sem, *, t, final):
    def row_copy(slot, dst_row, src_row):
        return pltpu.make_async_copy(yb_hbm.at[pl.ds(src_row, 1)],
                                     y_ref.at[slot, pl.ds(dst_row, 1)], sem)

    def start(r, c):
        row_copy(0, r, d1_ref[r]).start()
        row_copy(1, r, d2_ref[r]).start(priority=1)
        return c

    lax.fori_loop(0, t, start, 0, unroll=DMA_LOOP_UNROLL)

    def wait(r, c):
        row_copy(0, r, d1_ref[r]).wait()
        row_copy(1, r, d2_ref[r]).wait()
        return c

    lax.fori_loop(0, t, wait, 0, unroll=DMA_LOOP_UNROLL)

    rr = lax.broadcasted_iota(I32, (t, t), 0)
    cc = lax.broadcasted_iota(I32, (t, t), 1)
    eye = rr == cc
    w1 = jnp.sum(jnp.where(eye, wt_ref[0:1, :], 0.0), axis=1, keepdims=True)
    w2 = jnp.sum(jnp.where(eye, wt_ref[1:2, :], 0.0), axis=1, keepdims=True)
    out = h_ref[...] + w1 * y_ref[0] + w2 * y_ref[1]
    if final:
        out = _rmsnorm(out, ng_ref[...])
    o_ref[...] = out


def _combine(yb, d1, d2, h, wt, ng, t, final):
    n, d = h.shape
    smem = lambda: pl.BlockSpec((t,), lambda i: (i,), memory_space=pltpu.SMEM)
    return pl.pallas_call(
        functools.partial(_combine_kernel, t=t, final=final),
        grid=(n // t,),
        in_specs=[
            smem(), smem(),
            pl.BlockSpec(memory_space=pl.ANY),
            pl.BlockSpec((t, d), lambda i: (i, 0)),
            pl.BlockSpec((SUBLANES, t), lambda i: (0, i)),
            pl.BlockSpec((1, d), lambda i: (0, 0)),
        ],
        out_specs=pl.BlockSpec((t, d), lambda i: (i, 0)),
        out_shape=jax.ShapeDtypeStruct((n, d), F32),
        scratch_shapes=[pltpu.VMEM((2, t, d), F32), pltpu.SemaphoreType.DMA(())],
        compiler_params=_cparams("arbitrary"),
        name="moe_combine",
    )(d1, d2, yb, h, wt, ng)


def _final_norm_kernel(x_ref, g_ref, o_ref):
    o_ref[...] = _rmsnorm(x_ref[...], g_ref[...])


def _final_norm(h, g, tm):
    n, d = h.shape
    return pl.pallas_call(
        _final_norm_kernel,
        grid=(n // tm,),
        in_specs=[pl.BlockSpec((tm, d), lambda i: (i, 0)), pl.BlockSpec((1, d), lambda i: (0, 0))],
        out_specs=pl.BlockSpec((tm, d), lambda i: (i, 0)),
        out_shape=jax.ShapeDtypeStruct((n, d), F32),
        compiler_params=_cparams("parallel"),
        name="final_rmsnorm",
    )(h, g)


def _tile(n, pref):
    t = min(n, pref)
    assert n % t == 0, (n, pref)
    return t


def _mixer(h, g_mix, w_in, b_fox, b_gate, lb, hgrn_ng, w_sb, w_fox, w_hgrn, w_out, g_ffn, hn_dtype):
    b, s, d = h.shape
    n = b * s
    sbw = SB_HEADS * HEAD_DIM
    foxw = FOX_HEADS * HEAD_DIM
    hw = HGRN_HEADS * HGRN_DIM
    o_fox = 3 * sbw
    o_fb = o_fox + 3 * foxw
    o_hq = o_fb + FOX_HEADS
    o_hf, o_hi, o_hg = o_hq + hw, o_hq + 2 * hw, o_hq + 3 * hw
    o_gate = o_hq + 4 * hw
    w_main = jnp.concatenate(
        [w_in[:, o_gate:], w_in[:, o_hq:o_hf], w_in[:, o_hi:o_gate],
         w_in[:, :2 * sbw], w_in[:, o_fox:o_fox + 2 * foxw]], axis=1).astype(BF16)
    p_hgrn = N_BRANCHES * d
    p_sb = p_hgrn + 3 * hw
    p_fox = p_sb + 2 * sbw
    w_vt = jnp.concatenate([w_in[:, 2 * sbw:3 * sbw], w_in[:, o_fox + 2 * foxw:o_fb]], axis=1).T.astype(BF16)
    w_f = w_in[:, o_hf:o_hi].astype(BF16)
    w_fb = w_in[:, o_fb:o_hq]
    w_fbc = jnp.pad(w_fb, ((0, 0), (0, LANES - FOX_HEADS))).astype(BF16)
    w_fbr = jnp.pad(w_fb.T, ((0, SUBLANES - FOX_HEADS), (0, 0))).astype(BF16)
    bc = jnp.pad(b_fox, (0, LANES - FOX_HEADS)).reshape(1, LANES)
    br = jnp.broadcast_to(jnp.pad(b_fox, (0, SUBLANES - FOX_HEADS)).reshape(SUBLANES, 1), (SUBLANES, LANES))

    g_mix = g_mix.reshape(1, d)
    tm = _tile(s, 1024)
    proj, vt = _norm_matmul(h.reshape(n, d), g_mix, w_main, w_vt, s, tm, w_main.shape[1] // 2)
    proj = proj.reshape(b, s, -1)
    f_hgrn, cum_col, cum_row = _gate_cum(h, g_mix, w_f, w_fbc, w_fbr, bc, br, _tile(s, 512))

    tq = _tile(s, 256)
    oa = _sb_attention(proj, vt, p_sb // sbw, 0, tq, tq)
    ob = _fox_attention(proj, vt, p_fox // foxw, sbw, cum_col, cum_row, _tile(s, 512), _tile(s, 512))
    c_hq = p_hgrn // hw
    oc = _hgrn(proj, f_hgrn, lb.reshape(1, hw), hgrn_ng.reshape(1, HGRN_DIM),
               c_hq, c_hq + 1, c_hq + 2, _tile(s, 512))

    gcol0 = 0
    hnew, hn = _merge_out(
        h.reshape(n, d), oa.reshape(n, sbw), ob.reshape(n, foxw), oc.reshape(n, hw),
        proj.reshape(n, -1), gcol0, b_gate,
        w_sb.astype(BF16), w_fox.astype(BF16), w_hgrn.astype(BF16), w_out.astype(BF16),
        g_ffn.reshape(1, d), hn_dtype, _tile(n, 512))
    return hnew, hn


def _moe(h, hn, w_router, w_gate, w_up, w_down, final_g, final):
    n, d = h.shape
    rblk = 512
    nrows = n * TOP_K + N_EXPERTS * rblk
    nb = nrows // rblk
    nb_pad = -(-nb // LANES) * LANES
    idx, wt, cnt = _router(hn, w_router.T.astype(BF16), _tile(n, 1024))
    dest, blk = _dest(idx, cnt, rblk, nb_pad)
    d1, d2 = dest[0], dest[1]
    buf = _dispatch(hn, d1, d2, nrows, _tile(n, 256))
    yb = _experts(buf, blk[0, :nb], blk[1, :1], w_gate.astype(BF16), w_up.astype(BF16),
                  w_down.astype(BF16), rblk, w_gate.shape[2] // 2)
    return _combine(yb, d1, d2, h, wt, final_g.reshape(1, d), _tile(n, 256), final)


def kernel(x, norm_mix_g, w_in, b_fox, b_gate, hgrn_lb_logits, hgrn_norm_g, w_branch_sb, w_branch_fox, w_branch_hgrn, w_out, norm_ffn_g, w_ffn_gate, w_ffn_up, w_ffn_down, w_router, w_exp_gate, w_exp_up, w_exp_down, final_norm_g):
    b, s, d = x.shape
    n = b * s
    depth = w_in.shape[0]
    lbs = _lower_bounds(hgrn_lb_logits)
    h = x
    for l in range(depth):
        dense = l % 2 == 0
        h2, hn = _mixer(h, norm_mix_g[l], w_in[l], b_fox[l], b_gate[l], lbs[l], hgrn_norm_g[l],
                        w_branch_sb[l], w_branch_fox[l], w_branch_hgrn[l], w_out[l], norm_ffn_g[l],
                        BF16 if dense else F32)
        last = l == depth - 1
        i = l // 2
        if dense:
            h2 = _dense_ffn(hn, h2, w_ffn_gate[i].astype(BF16), w_ffn_up[i].astype(BF16),
                            w_ffn_down[i].astype(BF16), _tile(n, 1024), 768)
            if last:
                h2 = _final_norm(h2, final_norm_g.reshape(1, d), _tile(n, 1024))
        else:
            h2 = _moe(h2, hn, w_router[i], w_exp_gate[i], w_exp_up[i], w_exp_down[i],
                      final_norm_g, last)
        h = h2.reshape(b, s, d)
    return h
```
